```python
import numpy as np
import jax
import jax.numpy as jnp
from jax import lax

D_MODEL = 2048
BATCH = 2
SEQ = 8192
DEPTH = 2

HEAD_DIM = 64
MIX_WIDTH = D_MODEL
A_WIDTH = MIX_WIDTH // 4
A_HEADS = A_WIDTH // HEAD_DIM
A_KV_HEADS = 2
CMP_BLOCK = 32
CMP_STRIDE = 16
SEL_BLOCK = 64
SEL_TOPK = 16
SEL_LOCAL = 2
NSA_WINDOW = 512
Q_CHUNK = 128
FORCE_BONUS = 1.0e4
B_WIDTH = MIX_WIDTH // 2
B_HEADS = 4
B_HEAD_DIM = B_WIDTH // B_HEADS
CONV_WIDTH = 4
MLSTM_CHUNK = 64
C_WIDTH = MIX_WIDTH - A_WIDTH - B_WIDTH
C_HEADS = C_WIDTH // HEAD_DIM
C_KV_HEADS = 2
SWA_WINDOW = 128
BAND_BLOCK = 128
NEG_INF = -1.0e30
EPS = 1.0e-6
IN_SPLITS = (
    A_WIDTH,
    A_KV_HEADS * HEAD_DIM,
    A_KV_HEADS * HEAD_DIM,
    A_KV_HEADS * HEAD_DIM,
    A_KV_HEADS * HEAD_DIM,
    A_KV_HEADS * HEAD_DIM,
    A_KV_HEADS * HEAD_DIM,
    A_HEADS * 3,
    A_WIDTH,
    2 * B_WIDTH,
    B_WIDTH,
    B_HEADS,
    B_HEADS,
    B_WIDTH,
    B_WIDTH,
    C_WIDTH,
    C_KV_HEADS * HEAD_DIM,
    C_KV_HEADS * HEAD_DIM,
    C_WIDTH,
)
IN_COLS = sum(IN_SPLITS)

kernel_name = 'hymba_nsa_mlstm_swa_hybrid'


def rmsnorm(x, g):
    xf = x.astype(jnp.float32)
    y = xf * lax.rsqrt(jnp.mean(xf * xf, axis=-1, keepdims=True) + EPS)
    return (y * g.astype(jnp.float32)).astype(x.dtype)


def alibi_slopes(n_heads):
    return 2.0 ** (-8.0 * jnp.arange(1, n_heads + 1, dtype=jnp.float32) / n_heads)


def causal_depthwise_conv(x, w, b):
    T = x.shape[1]
    xp = jnp.pad(x, ((0, 0), (w.shape[0] - 1, 0), (0, 0)))
    y = b
    for i in range(w.shape[0]):
        y = y + xp[:, i:i + T] * w[i]
    return y


def banded_attention(q, k, v, slopes, window, sinks):
    bsz, T, H, d = q.shape
    G = k.shape[2]
    hpg = H // G
    blk = BAND_BLOCK
    nblk = T // blk
    span = blk + window
    scale = d ** -0.5
    kp = jnp.pad(k, ((0, 0), (window, 0), (0, 0), (0, 0)))
    vp = jnp.pad(v, ((0, 0), (window, 0), (0, 0), (0, 0)))
    qb = q.reshape(bsz, nblk, blk, G, hpg, d).transpose(1, 0, 2, 3, 4, 5)
    key_off = jnp.arange(span)[None, :]
    dist = (jnp.arange(blk)[:, None] + window) - key_off
    band = (dist >= 0) & (dist < window)
    slope_term = slopes.reshape(G, hpg)[:, :, None, None] * dist.astype(jnp.float32)

    def one_block(args):
        qj, j = args
        kj = lax.dynamic_slice_in_dim(kp, j * blk, span, axis=1)
        vj = lax.dynamic_slice_in_dim(vp, j * blk, span, axis=1)
        sc = jnp.einsum('bqghd,bsgd->bghqs', qj, kj).astype(jnp.float32) * scale - slope_term
        mask = band & (key_off >= window - j * blk)
        sc = jnp.where(mask, sc, NEG_INF)
        if sinks is None:
            p = jax.nn.softmax(sc, axis=-1)
        else:
            sk = sinks.astype(jnp.float32).reshape(G, hpg)[:, :, None, None]
            mx = jnp.maximum(sc.max(axis=-1, keepdims=True), sk)
            e = jnp.exp(sc - mx)
            p = e / (e.sum(axis=-1, keepdims=True) + jnp.exp(sk - mx))
        return jnp.einsum('bghqs,bsgd->bqghd', p.astype(v.dtype), vj)

    o = lax.map(one_block, (qb, jnp.arange(nblk)))
    return o.transpose(1, 0, 2, 3, 4, 5).reshape(bsz, T, H, d)


def nsa_compress(k, pe, w1, w2):
    bsz, T, G, d = k.shape
    n_cmp = (T - CMP_BLOCK) // CMP_STRIDE + 1
    idx = jnp.arange(n_cmp)[:, None] * CMP_STRIDE + jnp.arange(CMP_BLOCK)[None, :]
    blocks = k[:, idx] + pe[None, None, :, None, :]
    blocks = blocks.transpose(0, 1, 3, 2, 4).reshape(bsz, n_cmp, G, CMP_BLOCK * d)
    return jax.nn.silu(blocks @ w1) @ w2


def nsa_compressed_and_selected(q, k_cmp, v_cmp, k_slc, v_slc, slopes):
    bsz, T, H, d = q.shape
    G = k_cmp.shape[2]
    hpg = H // G
    n_cmp = k_cmp.shape[1]
    n_sel = T // SEL_BLOCK
    n_top = min(SEL_TOPK, n_sel)
    nq = T // Q_CHUNK
    scale = d ** -0.5
    cmp_start = jnp.arange(n_cmp) * CMP_STRIDE
    cmp_end = cmp_start + CMP_BLOCK - 1
    sel_start = jnp.arange(n_sel) * SEL_BLOCK
    overlap = ((cmp_start[:, None] < sel_start[None, :] + SEL_BLOCK)
               & (cmp_start[:, None] + CMP_BLOCK > sel_start[None, :])).astype(jnp.float32)
    kb = k_slc.reshape(bsz, n_sel, SEL_BLOCK, G, d).transpose(0, 3, 1, 2, 4)
    vb = v_slc.reshape(bsz, n_sel, SEL_BLOCK, G, d).transpose(0, 3, 1, 2, 4)
    qc = q.reshape(bsz, nq, Q_CHUNK, G, hpg, d).transpose(1, 0, 2, 3, 4, 5)
    sl = slopes.reshape(G, hpg)[None, :, :, None, None]
    b_idx = jnp.arange(bsz)[:, None, None, None]
    g_idx = jnp.arange(G)[None, :, None, None]
    r_idx = jnp.arange(Q_CHUNK)[None, None, :, None]
    sel_ids = jnp.arange(n_sel)[None, :]

    def one_chunk(args):
        qi, c = args
        t = c * Q_CHUNK + jnp.arange(Q_CHUNK)
        s_cmp = jnp.einsum('bqghd,bigd->bghqi', qi, k_cmp).astype(jnp.float32) * scale
        vis = cmp_end[None, :] <= t[:, None]
        p_cmp = jax.nn.softmax(jnp.where(vis, s_cmp, NEG_INF), axis=-1)
        p_cmp = p_cmp * vis.any(axis=-1)[:, None].astype(jnp.float32)
        o_cmp = jnp.einsum('bghqi,bigd->bqghd', p_cmp.astype(v_cmp.dtype), v_cmp)
        imp = jnp.einsum('bghqi,ij->bgqj', p_cmp, overlap)
        cur = (t // SEL_BLOCK)[:, None]
        valid = sel_ids <= cur
        forced = (sel_ids == 0) | (valid & (sel_ids > cur - SEL_LOCAL))
        imp = jnp.where(forced, FORCE_BONUS, jnp.where(valid, imp, -1.0))
        _, top = lax.top_k(imp, n_top)
        chosen = valid[r_idx, top]
        ks = kb[b_idx, g_idx, top].reshape(bsz, G, Q_CHUNK, n_top * SEL_BLOCK, d)
        vs = vb[b_idx, g_idx, top].reshape(bsz, G, Q_CHUNK, n_top * SEL_BLOCK, d)
        pos = (top[..., None] * SEL_BLOCK + jnp.arange(SEL_BLOCK)).reshape(bsz, G, Q_CHUNK, n_top * SEL_BLOCK)
        dist = t[None, None, :, None] - pos
        mask = jnp.repeat(chosen, SEL_BLOCK, axis=-1) & (dist >= 0)
        s_slc = (jnp.einsum('bqghd,bgqsd->bghqs', qi, ks).astype(jnp.float32) * scale
                 - sl * dist[:, :, None].astype(jnp.float32))
        p_slc = jax.nn.softmax(jnp.where(mask[:, :, None], s_slc, NEG_INF), axis=-1)
        o_slc = jnp.einsum('bghqs,bgqsd->bqghd', p_slc.astype(vs.dtype), vs)
        return o_cmp, o_slc

    o_cmp, o_slc = lax.map(one_chunk, (qc, jnp.arange(nq)))
    o_cmp = o_cmp.transpose(1, 0, 2, 3, 4, 5).reshape(bsz, T, H, d)
    o_slc = o_slc.transpose(1, 0, 2, 3, 4, 5).reshape(bsz, T, H, d)
    return o_cmp, o_slc


def mlstm_chunkwise(q, k, v, i_pre, f_pre):
    bsz, T, H, d = q.shape
    L = MLSTM_CHUNK
    nC = T // L

    def chunks(a):
        return a.reshape(bsz, nC, L, H, -1).transpose(1, 0, 3, 2, 4)

    qc, kc, vc = chunks(q), chunks(k), chunks(v)
    ic = i_pre.reshape(bsz, nC, L, H).transpose(1, 0, 3, 2)
    lfc = jax.nn.log_sigmoid(f_pre).reshape(bsz, nC, L, H).transpose(1, 0, 3, 2)
    causal = jnp.tril(jnp.ones((L, L), dtype=bool))

    def step(carry, xs):
        C, n, m = carry
        q_, k_, v_, i_, lf_ = xs
        b = jnp.cumsum(lf_, axis=-1)
        b_last = b[..., -1]
        log_d = jnp.where(causal, b[..., :, None] - b[..., None, :] + i_[..., None, :], -jnp.inf)
        log_inter = b + m[..., None]
        m_t = jnp.maximum(log_inter, log_d.max(axis=-1))
        w_intra = jnp.exp(log_d - m_t[..., None])
        w_inter = jnp.exp(log_inter - m_t)
        qk = jnp.einsum('bhtd,bhsd->bhts', q_, k_) * w_intra
        num = (w_inter[..., None] * jnp.einsum('bhtd,bhde->bhte', q_, C)
               + jnp.einsum('bhts,bhse->bhte', qk, v_))
        den = w_inter * jnp.einsum('bhtd,bhd->bht', q_, n) + qk.sum(axis=-1)
        h = num / jnp.maximum(jnp.abs(den), jnp.exp(-m_t))[..., None]
        log_g = b_last[..., None] - b + i_
        m_new = jnp.maximum(b_last + m, log_g.max(axis=-1))
        w_g = jnp.exp(log_g - m_new[..., None])
        decay = jnp.exp(b_last + m - m_new)
        C = decay[..., None, None] * C + jnp.einsum('bhsd,bhse->bhde', k_ * w_g[..., None], v_)
        n = decay[..., None] * n + jnp.einsum('bhs,bhsd->bhd', w_g, k_)
        return (C, n, m_new), h

    init = (jnp.zeros((bsz, H, d, v.shape[-1]), jnp.float32),
            jnp.zeros((bsz, H, d), jnp.float32),
            jnp.zeros((bsz, H), jnp.float32))
    _, h = lax.scan(step, init, (qc, kc, vc, ic, lfc))
    return h.transpose(1, 0, 3, 2, 4).reshape(bsz, T, H, -1)


def hybrid_layer(x, norm_g, w_in, w_out, cmp_pe_k, cmp_w1_k, cmp_w2_k, cmp_pe_v, cmp_w1_v, cmp_w2_v,
                 conv_w, conv_b, i_bias, f_bias, mnorm_g, sinks):
    bsz, T, _ = x.shape
    f32 = jnp.float32
    h = rmsnorm(x, norm_g)
    proj = h @ w_in
    split_at = [int(p) for p in np.cumsum(IN_SPLITS)[:-1]]
    (a_q, a_kc, a_vc, a_ks, a_vs, a_kw, a_vw, a_gate, a_z,
     b_qk, b_v, b_i, b_f, b_o, b_z,
     c_q, c_k, c_v, c_z) = jnp.split(proj, split_at, axis=-1)

    def heads(a, n):
        return a.reshape(bsz, T, n, -1)

    slopes_a = alibi_slopes(A_HEADS)
    q_a = heads(a_q, A_HEADS)
    k_cmp = nsa_compress(heads(a_kc, A_KV_HEADS), cmp_pe_k, cmp_w1_k, cmp_w2_k)
    v_cmp = nsa_compress(heads(a_vc, A_KV_HEADS), cmp_pe_v, cmp_w1_v, cmp_w2_v)
    o_cmp, o_slc = nsa_compressed_and_selected(q_a, k_cmp, v_cmp, heads(a_ks, A_KV_HEADS),
                                               heads(a_vs, A_KV_HEADS), slopes_a)
    o_win = banded_attention(q_a, heads(a_kw, A_KV_HEADS), heads(a_vw, A_KV_HEADS), slopes_a, NSA_WINDOW, None)
    gate = jax.nn.sigmoid(heads(a_gate, A_HEADS))
    o_a = gate[..., 0:1] * o_cmp + gate[..., 1:2] * o_slc + gate[..., 2:3] * o_win
    y_a = o_a.reshape(bsz, T, A_WIDTH) * jax.nn.silu(a_z)

    qk_b = jax.nn.silu(causal_depthwise_conv(b_qk, conv_w, conv_b))
    q_b, k_b = jnp.split(qk_b, 2, axis=-1)
    h_b = mlstm_chunkwise(heads(q_b, B_HEADS).astype(f32),
                          heads(k_b, B_HEADS).astype(f32) * B_HEAD_DIM ** -0.5,
                          heads(b_v, B_HEADS).astype(f32),
                          (b_i + i_bias).astype(f32),
                          (b_f + f_bias).astype(f32))
    h_b = jax.nn.sigmoid(heads(b_o, B_HEADS).astype(f32)) * h_b
    h_b = rmsnorm(h_b, mnorm_g.reshape(B_HEADS, B_HEAD_DIM)).astype(x.dtype)
    y_b = h_b.reshape(bsz, T, B_WIDTH) * jax.nn.silu(b_z)

    o_c = banded_attention(heads(c_q, C_HEADS), heads(c_k, C_KV_HEADS), heads(c_v, C_KV_HEADS),
                           alibi_slopes(C_HEADS), SWA_WINDOW, sinks)
    y_c = o_c.reshape(bsz, T, C_WIDTH) * jax.nn.silu(c_z)

    mix = jnp.concatenate([y_a, y_b, y_c], axis=-1)
    return x + mix @ w_out


def setup_inputs(seed: int = 0) -> dict:
    key = jax.random.key(seed)
    ks = jax.random.split(key, 17)
    f32 = jnp.float32

    def nrm(k, shape, s):
        return jax.random.normal(k, shape, f32) * s

    cmp_in = CMP_BLOCK * HEAD_DIM
    return {
        'x': nrm(ks[0], (BATCH, SEQ, D_MODEL), 1.0),
        'norm_g': 1.0 + nrm(ks[1], (DEPTH, D_MODEL), 0.02),
        'w_in': nrm(ks[2], (DEPTH, D_MODEL, IN_COLS), D_MODEL ** -0.5),
        'w_out': nrm(ks[3], (DEPTH, MIX_WIDTH, D_MODEL), MIX_WIDTH ** -0.5),
        'cmp_pe_k': nrm(ks[4], (DEPTH, CMP_BLOCK, HEAD_DIM), 0.02),
        'cmp_w1_k': nrm(ks[5], (DEPTH, cmp_in, HEAD_DIM), cmp_in ** -0.5),
        'cmp_w2_k': nrm(ks[6], (DEPTH, HEAD_DIM, HEAD_DIM), HEAD_DIM ** -0.5),
        'cmp_pe_v': nrm(ks[7], (DEPTH, CMP_BLOCK, HEAD_DIM), 0.02),
        'cmp_w1_v': nrm(ks[8], (DEPTH, cmp_in, HEAD_DIM), cmp_in ** -0.5),
        'cmp_w2_v': nrm(ks[9], (DEPTH, HEAD_DIM, HEAD_DIM), HEAD_DIM ** -0.5),
        'mlstm_conv_w': nrm(ks[10], (DEPTH, CONV_WIDTH, 2 * B_WIDTH), CONV_WIDTH ** -0.5),
        'mlstm_conv_b': nrm(ks[11], (DEPTH, 2 * B_WIDTH), 0.01),
        'mlstm_i_bias': nrm(ks[12], (DEPTH, B_HEADS), 0.1),
        'mlstm_f_bias': jnp.linspace(3.0, 6.0, B_HEADS, dtype=f32)[None, :] + nrm(ks[13], (DEPTH, B_HEADS), 0.1),
        'mlstm_norm_g': 1.0 + nrm(ks[14], (DEPTH, B_WIDTH), 0.02),
        'swa_sinks': nrm(ks[15], (DEPTH, C_HEADS), 0.5),
        'final_norm_g': 1.0 + nrm(ks[16], (D_MODEL,), 0.02),
    }


def reference(x, norm_g, w_in, w_out, cmp_pe_k, cmp_w1_k, cmp_w2_k, cmp_pe_v, cmp_w1_v, cmp_w2_v,
              mlstm_conv_w, mlstm_conv_b, mlstm_i_bias, mlstm_f_bias, mlstm_norm_g, swa_sinks, final_norm_g):
    for l in range(DEPTH):
        x = hybrid_layer(x, norm_g[l], w_in[l], w_out[l],
                         cmp_pe_k[l], cmp_w1_k[l], cmp_w2_k[l], cmp_pe_v[l], cmp_w1_v[l], cmp_w2_v[l],
                         mlstm_conv_w[l], mlstm_conv_b[l], mlstm_i_bias[l], mlstm_f_bias[l],
                         mlstm_norm_g[l], swa_sinks[l])
    return rmsnorm(x, final_norm_g)
```

```python
import functools

import numpy as np
import jax
import jax.numpy as jnp
from jax import lax
from jax.experimental import pallas as pl
from jax.experimental.pallas import tpu as pltpu

F32 = jnp.float32
BF16 = jnp.bfloat16

HEAD_DIM = 64
A_HEADS = 8
A_KV_HEADS = 2
HEADS_PER_GROUP = A_HEADS // A_KV_HEADS
GROUP_WIDTH = HEADS_PER_GROUP * HEAD_DIM
A_WIDTH = A_HEADS * HEAD_DIM
C_HEADS = 8
C_KV_HEADS = 2
C_WIDTH = C_HEADS * HEAD_DIM
B_HEADS = 4
B_HEAD_DIM = 256
B_WIDTH = B_HEADS * B_HEAD_DIM
CMP_BLOCK = 32
CMP_STRIDE = 16
SEL_BLOCK = 64
SEL_TOPK = 16
SEL_LOCAL = 2
NSA_WINDOW = 512
SWA_WINDOW = 128
CONV_WIDTH = 4
FORCE_BONUS = 1.0e4
NEG_INF = -1.0e30
EPS = 1.0e-6
SCALE = HEAD_DIM ** -0.5
LANES = 128
MAX_SEL_BLOCKS = LANES

OFF_B_QK = 0
OFF_B_V = 2048
OFF_B_O = 3072
OFF_B_Z = 4096
OFF_A_Q = 5120
OFF_A_Z = 5632
OFF_C_Q = 6144
OFF_C_Z = 6656
OFF_A_KV = 7168
OFF_C_KV = 7936
OFF_A_GATE = 8192
OFF_B_IF = 8320
PROJ_COLS = 8448

VMEM_LIMIT = 48 * 1024 * 1024


def _cparams(sem):
    return pltpu.CompilerParams(dimension_semantics=sem, vmem_limit_bytes=VMEM_LIMIT)


def _dot(a, b):
    return jnp.dot(a, b, preferred_element_type=F32)


def _dot_nt(a, b):
    return lax.dot_general(a, b, (((1,), (1,)), ((), ())), preferred_element_type=F32)


def _dot_tn(a, b):
    return lax.dot_general(a, b, (((0,), (0,)), ((), ())), preferred_element_type=F32)


def _sigmoid(x):
    return 1.0 / (1.0 + jnp.exp(-x))


def _silu(x):
    return x * _sigmoid(x)


def _norm_matmul_kernel(x_ref, g_ref, w_ref, o_ref, h_ref):
    @pl.when(pl.program_id(1) == 0)
    def _():
        x = x_ref[...]
        ms = jnp.mean(x * x, axis=-1, keepdims=True)
        h_ref[...] = (x * lax.rsqrt(ms + EPS) * g_ref[...]).astype(BF16)

    o_ref[...] = _dot(h_ref[...], w_ref[...])


def norm_matmul(x, g, w, *, tm, tn):
    n, d = x.shape
    cols = w.shape[1]
    return pl.pallas_call(
        _norm_matmul_kernel,
        grid=(n // tm, cols // tn),
        in_specs=[
            pl.BlockSpec((tm, d), lambda i, j: (i, 0)),
            pl.BlockSpec((1, d), lambda i, j: (0, 0)),
            pl.BlockSpec((d, tn), lambda i, j: (0, j)),
        ],
        out_specs=pl.BlockSpec((tm, tn), lambda i, j: (i, j)),
        out_shape=jax.ShapeDtypeStruct((n, cols), F32),
        scratch_shapes=[pltpu.VMEM((tm, d), BF16)],
        compiler_params=_cparams(("parallel", "arbitrary")),
        name="norm_in_proj",
    )(x, g, w)


def _out_proj_kernel(ya_ref, yb_ref, yc_ref, x_ref, wa_ref, wb_ref, wc_ref, g_ref, o_ref, *, final_norm):
    acc = _dot(ya_ref[...], wa_ref[...]) + _dot(yb_ref[...], wb_ref[...]) + _dot(yc_ref[...], wc_ref[...])
    y = x_ref[...] + acc
    if final_norm:
        ms = jnp.mean(y * y, axis=-1, keepdims=True)
        y = y * lax.rsqrt(ms + EPS) * g_ref[...]
    o_ref[...] = y


def out_proj(ya, yb, yc, x, wa, wb, wc, g, *, tm, final_norm):
    n, d = x.shape
    row = lambda w: pl.BlockSpec((tm, w), lambda i: (i, 0))
    full = lambda a: pl.BlockSpec(a.shape, lambda i: (0, 0))
    return pl.pallas_call(
        functools.partial(_out_proj_kernel, final_norm=final_norm),
        grid=(n // tm,),
        in_specs=[row(ya.shape[1]), row(yb.shape[1]), row(yc.shape[1]), row(d),
                  full(wa), full(wb), full(wc), full(g)],
        out_specs=row(d),
        out_shape=jax.ShapeDtypeStruct((n, d), F32),
        compiler_params=_cparams(("parallel",)),
        name="out_proj",
    )(ya, yb, yc, x, wa, wb, wc, g)


def _compress_kernel(seg_ref, pea_ref, peb_ref, w1a_ref, w1b_ref, w2_ref, o_ref, shift_ref):
    nseg = seg_ref.shape[3]
    seg = seg_ref[0, 0, 0]
    u1 = _dot((seg + pea_ref[0]).astype(BF16), w1a_ref[0])
    u2 = _dot((seg + peb_ref[0]).astype(BF16), w1b_ref[0])
    shift_ref[0:nseg, :] = u2
    shift_ref[nseg:nseg + 8, :] = jnp.zeros((8, HEAD_DIM), F32)
    pre = u1 + shift_ref[1:nseg + 1, :]
    o_ref[0, 0, 0] = _dot(_silu(pre).astype(BF16), w2_ref[0]).astype(BF16)


def nsa_compress(seg, pea, peb, w1a, w1b, w2):
    _, bsz, G, nseg, width = seg.shape
    par = lambda a: pl.BlockSpec((1,) + a.shape[1:], lambda s, b, g: (s,) + (0,) * (a.ndim - 1))
    return pl.pallas_call(
        _compress_kernel,
        grid=(2, bsz, G),
        in_specs=[pl.BlockSpec((1, 1, 1, nseg, width), lambda s, b, g: (s, b, g, 0, 0)),
                  par(pea), par(peb), par(w1a), par(w1b), par(w2)],
        out_specs=pl.BlockSpec((1, 1, 1, nseg, HEAD_DIM), lambda s, b, g: (s, b, g, 0, 0)),
        out_shape=jax.ShapeDtypeStruct((2, bsz, G, nseg, HEAD_DIM), BF16),
        scratch_shapes=[pltpu.VMEM((nseg + 8, HEAD_DIM), F32)],
        compiler_params=_cparams(("parallel", "parallel", "parallel")),
        name="nsa_compress",
    )(seg, pea, peb, w1a, w1b, w2)


def _cmp_select_kernel(q_ref, kc_ref, vc_ref, ov_ref, o_ref, selb_ref, *, qc, n_top):
    ncmp = kc_ref.shape[2]
    t0 = pl.program_id(1) * qc
    row_t = t0 + lax.broadcasted_iota(jnp.int32, (qc, 1), 0)
    col_i = lax.broadcasted_iota(jnp.int32, (1, ncmp), 1)
    vis = (col_i * CMP_STRIDE + (CMP_BLOCK - 1)) <= row_t
    any_vis = (row_t >= CMP_BLOCK - 1).astype(F32)
    cur = jnp.right_shift(row_t, 6)
    sel_ids = lax.broadcasted_iota(jnp.int32, (1, LANES), 1)
    valid = sel_ids <= cur
    forced = (sel_ids == 0) | (valid & (sel_ids > cur - SEL_LOCAL))
    blk_iota = lax.broadcasted_iota(jnp.int32, (LANES, qc), 0)
    ov = ov_ref[...]
    for g in range(A_KV_HEADS):
        kc = kc_ref[0, g]
        vc = vc_ref[0, g]
        psum = jnp.zeros((qc, ncmp), F32)
        for hh in range(HEADS_PER_GROUP):
            lo = (g * HEADS_PER_GROUP + hh) * HEAD_DIM
            qh = q_ref[0, :, lo:lo + HEAD_DIM].astype(BF16)
            s = _dot_nt(qh, kc) * SCALE
            s = jnp.where(vis, s, NEG_INF)
            e = jnp.exp(s - jnp.max(s, axis=-1, keepdims=True))
            p = e * (any_vis / jnp.sum(e, axis=-1, keepdims=True))
            o_ref[0, :, lo:lo + HEAD_DIM] = _dot(p.astype(BF16), vc)
            psum = psum + p
        p1 = psum.astype(BF16)
        r1 = psum - p1.astype(F32)
        p2 = r1.astype(BF16)
        p3 = (r1 - p2.astype(F32)).astype(BF16)
        imp = _dot(p1, ov) + _dot(p2, ov) + _dot(p3, ov)
        val = jnp.where(forced, FORCE_BONUS, jnp.where(valid, imp, -1.0))
        val_t = val.T
        sel_t = jnp.zeros((LANES, qc), F32)
        for _ in range(n_top):
            mx = jnp.max(val_t, axis=0, keepdims=True)
            first = jnp.min(jnp.where(val_t == mx, blk_iota, LANES), axis=0, keepdims=True)
            hit = blk_iota == first
            sel_t = jnp.where(hit, 1.0, sel_t)
            val_t = jnp.where(hit, -2.0, val_t)
        chosen = (sel_t.T > 0.5) & valid
        selb_ref[0, g] = jnp.where(chosen, 0.0, NEG_INF).astype(BF16)


def cmp_select(proj3, kcmp, vcmp, overlap, *, qc, n_top):
    bsz, T, _ = proj3.shape
    ncmp = kcmp.shape[2]
    return pl.pallas_call(
        functools.partial(_cmp_select_kernel, qc=qc, n_top=n_top),
        grid=(bsz, T // qc),
        in_specs=[
            pl.BlockSpec((1, qc, A_WIDTH), lambda b, c: (b, c, OFF_A_Q // A_WIDTH)),
            pl.BlockSpec((1, A_KV_HEADS, ncmp, HEAD_DIM), lambda b, c: (b, 0, 0, 0)),
            pl.BlockSpec((1, A_KV_HEADS, ncmp, HEAD_DIM), lambda b, c: (b, 0, 0, 0)),
            pl.BlockSpec((ncmp, LANES), lambda b, c: (0, 0)),
        ],
        out_specs=[
            pl.BlockSpec((1, qc, A_WIDTH), lambda b, c: (b, c, 0)),
            pl.BlockSpec((1, A_KV_HEADS, qc, LANES), lambda b, c: (b, 0, c, 0)),
        ],
        out_shape=[jax.ShapeDtypeStruct((bsz, T, A_WIDTH), F32),
                   jax.ShapeDtypeStruct((bsz, A_KV_HEADS, T, LANES), BF16)],
        compiler_params=_cparams(("parallel", "parallel")),
        name="nsa_cmp_select",
    )(proj3, kcmp, vcmp, overlap)


def _slc_kernel(q_ref, selb_ref, kaug_ref, v_ref, slope_ref, o_ref, qaug_ref, m_ref, l_ref, acc_ref, *, qc, tk):
    t0 = pl.program_id(2) * qc
    selb = selb_ref[0, 0]
    for hh in range(HEADS_PER_GROUP):
        qaug_ref[hh, :, 0:LANES] = selb
        qaug_ref[hh, :, LANES:LANES + HEAD_DIM] = (
            q_ref[0, :, hh * HEAD_DIM:(hh + 1) * HEAD_DIM] * SCALE).astype(BF16)
    m_ref[...] = jnp.full(m_ref.shape, NEG_INF, F32)
    l_ref[...] = jnp.zeros(l_ref.shape, F32)
    acc_ref[...] = jnp.zeros(acc_ref.shape, F32)
    key_iota = lax.broadcasted_iota(jnp.int32, (1, tk), 1)

    def tile(kt, causal):
        s0 = pl.multiple_of(kt * tk, tk)
        ka = kaug_ref[0, 0, pl.ds(s0, tk), :]
        vv = v_ref[0, 0, pl.ds(s0, tk), :]
        rel = (key_iota + (s0 - t0)).astype(F32)
        if causal:
            row_iota = lax.broadcasted_iota(jnp.int32, (qc, 1), 0)
            future = (key_iota + (s0 - t0)) > row_iota
        for hh in range(HEADS_PER_GROUP):
            s = _dot_nt(qaug_ref[hh], ka) + slope_ref[0, hh][:, 0:1] * rel
            if causal:
                s = jnp.where(future, NEG_INF, s)
            m_old = m_ref[hh]
            m_new = jnp.maximum(m_old, jnp.max(s, axis=-1, keepdims=True))
            alpha = jnp.exp(m_old - m_new)
            p = jnp.exp(s - m_new)
            l_ref[hh] = alpha * l_ref[hh] + jnp.sum(p, axis=-1, keepdims=True)
            acc_ref[hh] = alpha * acc_ref[hh] + _dot(p.astype(BF16), vv)
            m_ref[hh] = m_new

    n_full = lax.div(t0, tk)

    def body(kt, carry):
        tile(kt, False)
        return carry

    lax.fori_loop(0, n_full, body, 0)
    tile(n_full, True)
    for hh in range(HEADS_PER_GROUP):
        o_ref[0, :, hh * HEAD_DIM:(hh + 1) * HEAD_DIM] = acc_ref[hh] / l_ref[hh]


def slc_attention(proj3, selb, kaug, v, slopes, *, qc, tk):
    bsz, T, _ = proj3.shape
    G = A_KV_HEADS
    return pl.pallas_call(
        functools.partial(_slc_kernel, qc=qc, tk=tk),
        grid=(bsz, G, T // qc),
        in_specs=[
            pl.BlockSpec((1, qc, GROUP_WIDTH), lambda b, g, c: (b, c, OFF_A_Q // GROUP_WIDTH + g)),
            pl.BlockSpec((1, 1, qc, LANES), lambda b, g, c: (b, g, c, 0)),
            pl.BlockSpec((1, 1, T, LANES + HEAD_DIM), lambda b, g, c: (b, g, 0, 0)),
            pl.BlockSpec((1, 1, T, HEAD_DIM), lambda b, g, c: (b, g, 0, 0)),
            pl.BlockSpec((1, HEADS_PER_GROUP, 1, LANES), lambda b, g, c: (g, 0, 0, 0)),
        ],
        out_specs=pl.BlockSpec((1, qc, GROUP_WIDTH), lambda b, g, c: (b, c, g)),
        out_shape=jax.ShapeDtypeStruct((bsz, T, A_WIDTH), F32),
        scratch_shapes=[
            pltpu.VMEM((HEADS_PER_GROUP, qc, LANES + HEAD_DIM), BF16),
            pltpu.VMEM((HEADS_PER_GROUP, qc, 1), F32),
            pltpu.VMEM((HEADS_PER_GROUP, qc, 1), F32),
            pltpu.VMEM((HEADS_PER_GROUP, qc, HEAD_DIM), F32),
        ],
        compiler_params=_cparams(("parallel", "parallel", "arbitrary")),
        name="nsa_slc_attention",
    )(proj3, selb, kaug, v, slopes)


def _band_kernel(*refs, qc, window, has_sinks, gated):
    q_ref, k_ref, v_ref, slope_ref = refs[:4]
    pos = 4
    sink_ref = z_ref = None
    if has_sinks:
        sink_ref = refs[pos]
        pos += 1
    if gated:
        z_ref = refs[pos]
        pos += 1
    o_ref = refs[pos]
    span = qc + window
    t0 = pl.multiple_of(pl.program_id(2) * qc, qc)
    kk = k_ref[0, 0, pl.ds(t0, span), :]
    vv = v_ref[0, 0, pl.ds(t0, span), :]
    key_rel = lax.broadcasted_iota(jnp.int32, (1, span), 1) - window
    row_iota = lax.broadcasted_iota(jnp.int32, (qc, 1), 0)
    dist = row_iota - key_rel
    band = (dist >= 0) & (dist < window) & ((key_rel + t0) >= 0)
    rel = (-dist).astype(F32) if has_sinks else key_rel.astype(F32)
    for hh in range(HEADS_PER_GROUP):
        lo = hh * HEAD_DIM
        qh = q_ref[0, :, lo:lo + HEAD_DIM].astype(BF16)
        s = _dot_nt(qh, kk) * SCALE + slope_ref[0, hh][:, 0:1] * rel
        s = jnp.where(band, s, NEG_INF)
        mx = jnp.max(s, axis=-1, keepdims=True)
        if has_sinks:
            sk = sink_ref[0, hh][:, 0:1]
            mx = jnp.maximum(mx, sk)
        e = jnp.exp(s - mx)
        den = jnp.sum(e, axis=-1, keepdims=True)
        if has_sinks:
            den = den + jnp.exp(sk - mx)
        o = _dot(e.astype(BF16), vv) / den
        if gated:
            o = o * _silu(z_ref[0, :, lo:lo + HEAD_DIM])
        o_ref[0, :, lo:lo + HEAD_DIM] = o.astype(o_ref.dtype)


def band_attention(proj3, kpad, vpad, slopes, sinks, *, q_off, z_off, qc, window, out_dtype):
    bsz, T, _ = proj3.shape
    G = kpad.shape[1]
    has_sinks = sinks is not None
    gated = z_off is not None
    small = pl.BlockSpec((1, HEADS_PER_GROUP, 1, LANES), lambda b, g, c: (g, 0, 0, 0))
    kv_spec = pl.BlockSpec((1, 1, T + window, HEAD_DIM), lambda b, g, c: (b, g, 0, 0))
    in_specs = [pl.BlockSpec((1, qc, GROUP_WIDTH), lambda b, g, c: (b, c, q_off // GROUP_WIDTH + g)),
                kv_spec, kv_spec, small]
    args = [proj3, kpad, vpad, slopes]
    if has_sinks:
        in_specs.append(small)
        args.append(sinks)
    if gated:
        in_specs.append(pl.BlockSpec((1, qc, GROUP_WIDTH), lambda b, g, c: (b, c, z_off // GROUP_WIDTH + g)))
        args.append(proj3)
    return pl.pallas_call(
        functools.partial(_band_kernel, qc=qc, window=window, has_sinks=has_sinks, gated=gated),
        grid=(bsz, G, T // qc),
        in_specs=in_specs,
        out_specs=pl.BlockSpec((1, qc, GROUP_WIDTH), lambda b, g, c: (b, c, g)),
        out_shape=jax.ShapeDtypeStruct((bsz, T, G * GROUP_WIDTH), out_dtype),
        compiler_params=_cparams(("parallel", "parallel", "parallel")),
        name="band_attention_w%d" % window,
    )(*args)


def _mix_a_kernel(ocmp_ref, oslc_ref, owin_ref, gate_ref, z_ref, y_ref):
    gate = _sigmoid(gate_ref[...])
    for h in range(A_HEADS):
        lo = h * HEAD_DIM
        sl = slice(lo, lo + HEAD_DIM)
        o = (gate[:, 3 * h:3 * h + 1] * ocmp_ref[:, sl]
             + gate[:, 3 * h + 1:3 * h + 2] * oslc_ref[:, sl]
             + gate[:, 3 * h + 2:3 * h + 3] * owin_ref[:, sl])
        y_ref[:, sl] = (o * _silu(z_ref[:, sl])).astype(y_ref.dtype)


def mix_a(ocmp, oslc, owin, proj, *, tm):
    n = proj.shape[0]
    row = pl.BlockSpec((tm, A_WIDTH), lambda i: (i, 0))
    return pl.pallas_call(
        _mix_a_kernel,
        grid=(n // tm,),
        in_specs=[row, row, row,
                  pl.BlockSpec((tm, LANES), lambda i: (i, OFF_A_GATE // LANES)),
                  pl.BlockSpec((tm, A_WIDTH), lambda i: (i, OFF_A_Z // A_WIDTH))],
        out_specs=row,
        out_shape=jax.ShapeDtypeStruct((n, A_WIDTH), BF16),
        compiler_params=_cparams(("parallel",)),
        name="nsa_gate_mix",
    )(ocmp, oslc, owin, proj, proj)


def _log_sigmoid(x):
    return jnp.minimum(x, 0.0) - jnp.log(1.0 + jnp.exp(-jnp.abs(x)))


def _mlstm_kernel(q_ref, k_ref, v_ref, og_ref, z_ref, ic_ref, fc_ref, ir_ref, fr_ref, ib_ref, fb_ref,
                  cwq_ref, cwk_ref, cbq_ref, cbk_ref, g_ref, y_ref,
                  xq_ref, xk_ref, c_ref, n_ref, m_ref, *, L):
    @pl.when(pl.program_id(2) == 0)
    def _():
        xq_ref[0:8, :] = jnp.zeros((8, B_HEAD_DIM), F32)
        xk_ref[0:8, :] = jnp.zeros((8, B_HEAD_DIM), F32)
        c_ref[...] = jnp.zeros(c_ref.shape, F32)
        n_ref[...] = jnp.zeros(n_ref.shape, F32)
        m_ref[...] = jnp.zeros(m_ref.shape, F32)

    def conv_silu(x_ref, hist_ref, w_ref, b_ref):
        hist_ref[8:8 + L, :] = x_ref[0]
        y = b_ref[...]
        for i in range(CONV_WIDTH):
            y = y + w_ref[i:i + 1, :] * hist_ref[5 + i:5 + i + L, :]
        hist_ref[0:8, :] = hist_ref[L:L + 8, :]
        return _silu(y)

    q = conv_silu(q_ref, xq_ref, cwq_ref, cbq_ref)
    k = conv_silu(k_ref, xk_ref, cwk_ref, cbk_ref) * (B_HEAD_DIM ** -0.5)
    v = v_ref[0]
    i_col = ic_ref[0, 0] + ib_ref[0]
    i_row = ir_ref[0, 0] + ib_ref[0]
    lf_col = _log_sigmoid(fc_ref[0, 0] + fb_ref[0])
    lf_row = _log_sigmoid(fr_ref[0, 0] + fb_ref[0])
    r_i = lax.broadcasted_iota(jnp.int32, (L, L), 0)
    c_i = lax.broadcasted_iota(jnp.int32, (L, L), 1)
    causal = c_i <= r_i
    b_col = jnp.sum(jnp.where(causal, lf_row, 0.0), axis=1, keepdims=True)
    b_row = jnp.sum(jnp.where(r_i <= c_i, lf_col, 0.0), axis=0, keepdims=True)
    b_last = jnp.sum(lf_row, axis=1, keepdims=True)
    m_prev = m_ref[...]
    log_d = jnp.where(causal, b_col - b_row + i_row, NEG_INF)
    log_inter = b_col + m_prev
    m_t = jnp.maximum(log_inter, jnp.max(log_d, axis=1, keepdims=True))
    w_intra = jnp.exp(log_d - m_t)
    w_inter = jnp.exp(log_inter - m_t)
    qb = q.astype(BF16)
    vb = v.astype(BF16)
    qk = _dot_nt(qb, k.astype(BF16)) * w_intra
    num = w_inter * _dot(qb, c_ref[...].astype(BF16)) + _dot(qk.astype(BF16), vb)
    den = (w_inter * jnp.sum(q * n_ref[...], axis=1, keepdims=True)
           + jnp.sum(qk, axis=1, keepdims=True))
    h = num / jnp.maximum(jnp.abs(den), jnp.exp(-m_t))
    log_g_row = b_last - b_row + i_row
    m_new = jnp.maximum(b_last + m_prev, jnp.max(log_g_row, axis=1, keepdims=True))
    w_g = jnp.exp(b_last - b_col + i_col - m_new)
    decay = jnp.exp(b_last + m_prev - m_new)
    kw = k * w_g
    c_ref[...] = decay * c_ref[...] + _dot_tn(kw.astype(BF16), vb)
    n_ref[...] = decay * n_ref[...] + jnp.sum(kw, axis=0, keepdims=True)
    m_ref[...] = m_new
    hb = _sigmoid(og_ref[0]) * h
    ms = jnp.mean(hb * hb, axis=-1, keepdims=True)
    hb = hb * lax.rsqrt(ms + EPS) * g_ref[...]
    y_ref[0] = (hb * _silu(z_ref[0])).astype(y_ref.dtype)


def mlstm(proj3, if_col, if_row, i_bias, f_bias, conv_w, conv_b, norm_g, *, L):
    bsz, T, _ = proj3.shape
    H = B_HEADS
    D = B_HEAD_DIM

    def col(off, k_half=False):
        base = off // D + (H if k_half else 0)
        return pl.BlockSpec((1, L, D), lambda b, h, c: (b, c, base + h))

    def par(rows, k_half=False):
        base = H if k_half else 0
        return pl.BlockSpec((rows, D), lambda b, h, c: (0, base + h))

    in_specs = [
        col(OFF_B_QK), col(OFF_B_QK, True), col(OFF_B_V), col(OFF_B_O), col(OFF_B_Z),
        pl.BlockSpec((1, 1, L, 1), lambda b, h, c: (b, h, c, 0)),
        pl.BlockSpec((1, 1, L, 1), lambda b, h, c: (b, H + h, c, 0)),
        pl.BlockSpec((1, 1, 1, L), lambda b, h, c: (b, h, 0, c)),
        pl.BlockSpec((1, 1, 1, L), lambda b, h, c: (b, H + h, 0, c)),
        pl.BlockSpec((1, 1, 1), lambda b, h, c: (h, 0, 0)),
        pl.BlockSpec((1, 1, 1), lambda b, h, c: (h, 0, 0)),
        par(CONV_WIDTH), par(CONV_WIDTH, True), par(1), par(1, True),
        pl.BlockSpec((1, D), lambda b, h, c: (0, h)),
    ]
    return pl.pallas_call(
        functools.partial(_mlstm_kernel, L=L),
        grid=(bsz, H, T // L),
        in_specs=in_specs,
        out_specs=pl.BlockSpec((1, L, D), lambda b, h, c: (b, c, h)),
        out_shape=jax.ShapeDtypeStruct((bsz, T, B_WIDTH), BF16),
        scratch_shapes=[
            pltpu.VMEM((L + 8, D), F32), pltpu.VMEM((L + 8, D), F32),
            pltpu.VMEM((D, D), F32), pltpu.VMEM((1, D), F32), pltpu.VMEM((1, 1), F32),
        ],
        compiler_params=_cparams(("parallel", "parallel", "arbitrary")),
        name="mlstm_chunkwise",
    )(proj3, proj3, proj3, proj3, proj3, if_col, if_col, if_row, if_row, i_bias, f_bias,
      conv_w, conv_w, conv_b, conv_b, norm_g)


IN_SPLITS = (A_WIDTH, 128, 128, 128, 128, 128, 128, A_HEADS * 3, A_WIDTH,
             2 * B_WIDTH, B_WIDTH, B_HEADS, B_HEADS, B_WIDTH, B_WIDTH,
             C_WIDTH, 128, 128, C_WIDTH)


def _permute_w_in(w_in):
    starts = np.concatenate([[0], np.cumsum(IN_SPLITS)])
    seg = lambda i: w_in[:, int(starts[i]):int(starts[i + 1])]
    (a_q, a_kc, a_vc, a_ks, a_vs, a_kw, a_vw, a_gate, a_z,
     b_qk, b_v, b_i, b_f, b_o, b_z, c_q, c_k, c_v, c_z) = [seg(i) for i in range(len(IN_SPLITS))]
    d = w_in.shape[0]
    pad = lambda a: jnp.pad(a, ((0, 0), (0, LANES - a.shape[1])))
    cols = [b_qk, b_v, b_o, b_z, a_q, a_z, c_q, c_z,
            a_kc, a_vc, a_ks, a_vs, a_kw, a_vw, c_k, c_v,
            pad(a_gate), pad(jnp.concatenate([b_i, b_f], axis=1))]
    out = jnp.concatenate(cols, axis=1)
    assert out.shape == (d, PROJ_COLS)
    return out.astype(BF16)


def _alibi(n_heads):
    return 2.0 ** (-8.0 * jnp.arange(1, n_heads + 1, dtype=F32) / n_heads)


def _lane_rep(v, groups):
    return jnp.broadcast_to(v.astype(F32).reshape(groups, -1, 1, 1), (groups, v.shape[0] // groups, 1, LANES))


def _kv_heads(a, bsz, T, G):
    return a.reshape(bsz, T, G, HEAD_DIM).transpose(0, 2, 1, 3)


def _mixers(x2d, bsz, T, p, cfg):
    n, d = x2d.shape
    G = A_KV_HEADS
    proj = norm_matmul(x2d, p["norm_g"].reshape(1, d), _permute_w_in(p["w_in"]), tm=cfg["tm_in"], tn=cfg["tn_in"])
    proj3 = proj.reshape(bsz, T, PROJ_COLS)
    akv = proj3[:, :, OFF_A_KV:OFF_A_KV + 6 * 128]
    a_kc, a_vc, a_ks, a_vs, a_kw, a_vw = [akv[:, :, i * 128:(i + 1) * 128] for i in range(6)]
    c_k = proj3[:, :, OFF_C_KV:OFF_C_KV + 128]
    c_v = proj3[:, :, OFF_C_KV + 128:OFF_C_KV + 256]

    nseg = T // CMP_STRIDE
    seg = jnp.stack([a_kc, a_vc]).reshape(2, bsz, nseg, CMP_STRIDE, G, HEAD_DIM)
    seg = seg.transpose(0, 1, 4, 2, 3, 5).reshape(2, bsz, G, nseg, CMP_STRIDE * HEAD_DIM)
    pe = jnp.stack([p["cmp_pe_k"], p["cmp_pe_v"]])
    w1 = jnp.stack([p["cmp_w1_k"], p["cmp_w1_v"]]).astype(BF16)
    w2 = jnp.stack([p["cmp_w2_k"], p["cmp_w2_v"]]).astype(BF16)
    half = CMP_STRIDE * HEAD_DIM
    kv_cmp = nsa_compress(seg, pe[:, :CMP_STRIDE].reshape(2, 1, half), pe[:, CMP_STRIDE:].reshape(2, 1, half),
                          w1[:, :half], w1[:, half:], w2)
    k_cmp, v_cmp = kv_cmp[0], kv_cmp[1]

    n_sel = T // SEL_BLOCK
    assert n_sel <= MAX_SEL_BLOCKS
    ci = np.arange(nseg)[:, None] * CMP_STRIDE
    sj = np.arange(LANES)[None, :] * SEL_BLOCK
    overlap = jnp.asarray(((ci < sj + SEL_BLOCK) & (ci + CMP_BLOCK > sj)).astype(np.float32), dtype=BF16)
    o_cmp, selb = cmp_select(proj3, k_cmp, v_cmp, overlap, qc=cfg["qc_cmp"], n_top=min(SEL_TOPK, n_sel))

    slopes_a = _lane_rep(_alibi(A_HEADS), G)
    onehot = jnp.asarray((np.arange(T)[:, None] // SEL_BLOCK == np.arange(LANES)[None, :]).astype(np.float32),
                         dtype=BF16)
    kaug = jnp.concatenate([jnp.broadcast_to(onehot, (bsz, G, T, LANES)), _kv_heads(a_ks, bsz, T, G).astype(BF16)],
                           axis=-1)
    o_slc = slc_attention(proj3, selb, kaug, _kv_heads(a_vs, bsz, T, G).astype(BF16), slopes_a,
                          qc=cfg["qc_slc"], tk=cfg["tk_slc"])

    def padded(a, window):
        return jnp.pad(_kv_heads(a, bsz, T, a.shape[-1] // HEAD_DIM).astype(BF16),
                       ((0, 0), (0, 0), (window, 0), (0, 0)))

    o_win = band_attention(proj3, padded(a_kw, NSA_WINDOW), padded(a_vw, NSA_WINDOW), slopes_a, None,
                           q_off=OFF_A_Q, z_off=None, qc=cfg["qc_win"], window=NSA_WINDOW, out_dtype=F32)
    y_a = mix_a(o_cmp.reshape(n, A_WIDTH), o_slc.reshape(n, A_WIDTH), o_win.reshape(n, A_WIDTH), proj,
                tm=cfg["tm_mix"])

    if_t = proj3[:, :, OFF_B_IF:OFF_B_IF + 2 * B_HEADS].transpose(0, 2, 1)
    y_b = mlstm(proj3, if_t[..., None], if_t[:, :, None, :],
                p["mlstm_i_bias"].reshape(B_HEADS, 1, 1), p["mlstm_f_bias"].reshape(B_HEADS, 1, 1),
                p["mlstm_conv_w"], p["mlstm_conv_b"].reshape(1, -1), p["mlstm_norm_g"].reshape(1, -1),
                L=cfg["mlstm_chunk"])

    y_c = band_attention(proj3, padded(c_k, SWA_WINDOW), padded(c_v, SWA_WINDOW), _lane_rep(_alibi(C_HEADS), C_KV_HEADS),
                         _lane_rep(p["swa_sinks"], C_KV_HEADS), q_off=OFF_C_Q, z_off=OFF_C_Z,
                         qc=cfg["qc_win"], window=SWA_WINDOW, out_dtype=BF16)

    return y_a, y_b.reshape(n, B_WIDTH), y_c.reshape(n, C_WIDTH), (o_cmp, o_slc, o_win, selb)


def _layer(x2d, bsz, T, p, final_g, cfg):
    n, d = x2d.shape
    y_a, y_b, y_c, _ = _mixers(x2d, bsz, T, p, cfg)
    w_out = p["w_out"].astype(BF16)
    g = final_g if final_g is not None else p["norm_g"]
    return out_proj(y_a, y_b, y_c, x2d,
                    w_out[:A_WIDTH], w_out[A_WIDTH:A_WIDTH + B_WIDTH], w_out[A_WIDTH + B_WIDTH:],
                    g.reshape(1, d), tm=cfg["tm_out"], final_norm=final_g is not None)


def _config(T):
    return dict(tm_in=min(1024, T), tn_in=768, tm_out=min(512, T), tm_mix=min(512, T),
                qc_cmp=128, qc_slc=256, tk_slc=512, qc_win=128, mlstm_chunk=min(256, T))


def kernel(x, norm_g, w_in, w_out, cmp_pe_k, cmp_w1_k, cmp_w2_k, cmp_pe_v, cmp_w1_v, cmp_w2_v, mlstm_conv_w, mlstm_conv_b, mlstm_i_bias, mlstm_f_bias, mlstm_norm_g, swa_sinks, final_norm_g):
    bsz, T, d = x.shape
    depth = w_in.shape[0]
    cfg = _config(T)
    x2d = x.reshape(bsz * T, d)
    for l in range(depth):
        p = dict(norm_g=norm_g[l], w_in=w_in[l], w_out=w_out[l],
                 cmp_pe_k=cmp_pe_k[l], cmp_w1_k=cmp_w1_k[l], cmp_w2_k=cmp_w2_k[l],
                 cmp_pe_v=cmp_pe_v[l], cmp_w1_v=cmp_w1_v[l], cmp_w2_v=cmp_w2_v[l],
                 mlstm_conv_w=mlstm_conv_w[l], mlstm_conv_b=mlstm_conv_b[l],
                 mlstm_i_bias=mlstm_i_bias[l], mlstm_f_bias=mlstm_f_bias[l],
                 mlstm_norm_g=mlstm_norm_g[l], swa_sinks=swa_sinks[l])
        x2d = _layer(x2d, bsz, T, p, final_norm_g if l == depth - 1 else None, cfg)
    return x2d.reshape(bsz, T, d)
```

```python
import functools

import numpy as np
import jax
import jax.numpy as jnp
from jax import lax
from jax.experimental import pallas as pl
from jax.experimental.pallas import tpu as pltpu

F32 = jnp.float32
BF16 = jnp.bfloat16

HEAD_DIM = 64
A_HEADS = 8
A_KV_HEADS = 2
HEADS_PER_GROUP = A_HEADS // A_KV_HEADS
GROUP_WIDTH = HEADS_PER_GROUP * HEAD_DIM
A_WIDTH = A_HEADS * HEAD_DIM
C_HEADS = 8
C_KV_HEADS = 2
C_WIDTH = C_HEADS * HEAD_DIM
B_HEADS = 4
B_HEAD_DIM = 256
B_WIDTH = B_HEADS * B_HEAD_DIM
CMP_BLOCK = 32
CMP_STRIDE = 16
SEL_BLOCK = 64
SEL_TOPK = 16
SEL_LOCAL = 2
NSA_WINDOW = 512
SWA_WINDOW = 128
CONV_WIDTH = 4
FORCE_BONUS = 1.0e4
NEG_INF = -1.0e30
EPS = 1.0e-6
SCALE = HEAD_DIM ** -0.5
LANES = 128
MAX_SEL_BLOCKS = LANES
AUG_DEPTH = 256

OFF_B_QK = 0
OFF_B_V = 2048
OFF_B_O = 3072
OFF_B_Z = 4096
OFF_A_Q = 5120
OFF_A_Z = 5632
OFF_C_Q = 6144
OFF_C_Z = 6656
OFF_A_KV = 7168
OFF_C_KV = 7936
OFF_A_GATE = 8192
OFF_B_IF = 8320
PROJ_COLS = 8448

VMEM_LIMIT = 48 * 1024 * 1024


def _cparams(sem):
    return pltpu.CompilerParams(dimension_semantics=sem, vmem_limit_bytes=VMEM_LIMIT)


def _dot(a, b):
    return jnp.dot(a, b, preferred_element_type=F32)


def _dot_nt(a, b):
    return lax.dot_general(a, b, (((1,), (1,)), ((), ())), preferred_element_type=F32)


def _dot_tn(a, b):
    return lax.dot_general(a, b, (((0,), (0,)), ((), ())), preferred_element_type=F32)


def _sigmoid(x):
    return 1.0 / (1.0 + jnp.exp(-x))


def _silu(x):
    return x * _sigmoid(x)


def _norm_matmul_kernel(x_ref, g_ref, w_ref, o_ref, h_ref):
    @pl.when(pl.program_id(1) == 0)
    def _():
        x = x_ref[...]
        ms = jnp.mean(x * x, axis=-1, keepdims=True)
        h_ref[...] = (x * lax.rsqrt(ms + EPS) * g_ref[...]).astype(BF16)

    o_ref[...] = _dot(h_ref[...], w_ref[...])


def norm_matmul(x, g, w, *, tm, tn):
    n, d = x.shape
    cols = w.shape[1]
    return pl.pallas_call(
        _norm_matmul_kernel,
        grid=(n // tm, cols // tn),
        in_specs=[
            pl.BlockSpec((tm, d), lambda i, j: (i, 0)),
            pl.BlockSpec((1, d), lambda i, j: (0, 0)),
            pl.BlockSpec((d, tn), lambda i, j: (0, j)),
        ],
        out_specs=pl.BlockSpec((tm, tn), lambda i, j: (i, j)),
        out_shape=jax.ShapeDtypeStruct((n, cols), F32),
        scratch_shapes=[pltpu.VMEM((tm, d), BF16)],
        compiler_params=_cparams(("parallel", "arbitrary")),
        name="norm_in_proj",
    )(x, g, w)


def _out_proj_kernel(ya_ref, yb_ref, yc_ref, x_ref, wa_ref, wb_ref, wc_ref, g_ref, o_ref, *, final_norm):
    acc = _dot(ya_ref[...], wa_ref[...]) + _dot(yb_ref[...], wb_ref[...]) + _dot(yc_ref[...], wc_ref[...])
    y = x_ref[...] + acc
    if final_norm:
        ms = jnp.mean(y * y, axis=-1, keepdims=True)
        y = y * lax.rsqrt(ms + EPS) * g_ref[...]
    o_ref[...] = y


def out_proj(ya, yb, yc, x, wa, wb, wc, g, *, tm, final_norm):
    n, d = x.shape
    row = lambda w: pl.BlockSpec((tm, w), lambda i: (i, 0))
    full = lambda a: pl.BlockSpec(a.shape, lambda i: (0, 0))
    return pl.pallas_call(
        functools.partial(_out_proj_kernel, final_norm=final_norm),
        grid=(n // tm,),
        in_specs=[row(ya.shape[1]), row(yb.shape[1]), row(yc.shape[1]), row(d),
                  full(wa), full(wb), full(wc), full(g)],
        out_specs=row(d),
        out_shape=jax.ShapeDtypeStruct((n, d), F32),
        compiler_params=_cparams(("parallel",)),
        name="out_proj",
    )(ya, yb, yc, x, wa, wb, wc, g)


def _compress_kernel(seg_ref, pea_ref, peb_ref, w1a_ref, w1b_ref, w2_ref, o_ref, shift_ref):
    nseg = seg_ref.shape[3]
    seg = seg_ref[0, 0, 0]
    u1 = _dot((seg + pea_ref[0]).astype(BF16), w1a_ref[0])
    u2 = _dot((seg + peb_ref[0]).astype(BF16), w1b_ref[0])
    shift_ref[0:nseg, :] = u2
    shift_ref[nseg:nseg + 8, :] = jnp.zeros((8, HEAD_DIM), F32)
    pre = u1 + shift_ref[1:nseg + 1, :]
    o_ref[0, 0, 0] = _dot(_silu(pre).astype(BF16), w2_ref[0]).astype(BF16)


def nsa_compress(seg, pea, peb, w1a, w1b, w2):
    _, bsz, G, nseg, width = seg.shape
    par = lambda a: pl.BlockSpec((1,) + a.shape[1:], lambda s, b, g: (s,) + (0,) * (a.ndim - 1))
    return pl.pallas_call(
        _compress_kernel,
        grid=(2, bsz, G),
        in_specs=[pl.BlockSpec((1, 1, 1, nseg, width), lambda s, b, g: (s, b, g, 0, 0)),
                  par(pea), par(peb), par(w1a), par(w1b), par(w2)],
        out_specs=pl.BlockSpec((1, 1, 1, nseg, HEAD_DIM), lambda s, b, g: (s, b, g, 0, 0)),
        out_shape=jax.ShapeDtypeStruct((2, bsz, G, nseg, HEAD_DIM), BF16),
        scratch_shapes=[pltpu.VMEM((nseg + 8, HEAD_DIM), F32)],
        compiler_params=_cparams(("parallel", "parallel", "parallel")),
        name="nsa_compress",
    )(seg, pea, peb, w1a, w1b, w2)


def _cmp_select_kernel(q_ref, kc_ref, vc_ref, ov_ref, o_ref, selbt_ref, *, qc, n_top):
    ncmp = kc_ref.shape[2]
    t0 = pl.program_id(1) * qc
    row_t = t0 + lax.broadcasted_iota(jnp.int32, (qc, 1), 0)
    col_i = lax.broadcasted_iota(jnp.int32, (1, ncmp), 1)
    vis = (col_i * CMP_STRIDE + (CMP_BLOCK - 1)) <= row_t
    any_vis = (row_t >= CMP_BLOCK - 1).astype(F32)
    cur = jnp.right_shift(row_t, 6)
    sel_ids = lax.broadcasted_iota(jnp.int32, (1, LANES), 1)
    valid = sel_ids <= cur
    forced = (sel_ids == 0) | (valid & (sel_ids > cur - SEL_LOCAL))
    blk_iota = lax.broadcasted_iota(jnp.int32, (LANES, qc), 0)
    cur_row = jnp.right_shift(t0 + lax.broadcasted_iota(jnp.int32, (1, qc), 1), 6)
    ov = ov_ref[...]
    for g in range(A_KV_HEADS):
        kc = kc_ref[0, g]
        vc = vc_ref[0, g]
        psum = jnp.zeros((qc, ncmp), F32)
        for hh in range(HEADS_PER_GROUP):
            lo = (g * HEADS_PER_GROUP + hh) * HEAD_DIM
            qh = q_ref[0, :, lo:lo + HEAD_DIM].astype(BF16)
            s = _dot_nt(qh, kc) * SCALE
            s = jnp.where(vis, s, NEG_INF)
            e = jnp.exp(s - jnp.max(s, axis=-1, keepdims=True))
            p = e * (any_vis / jnp.sum(e, axis=-1, keepdims=True))
            o_ref[0, :, lo:lo + HEAD_DIM] = _dot(p.astype(BF16), vc)
            psum = psum + p
        p1 = psum.astype(BF16)
        r1 = psum - p1.astype(F32)
        p2 = r1.astype(BF16)
        p3 = (r1 - p2.astype(F32)).astype(BF16)
        imp = _dot(p1, ov) + _dot(p2, ov) + _dot(p3, ov)
        val = jnp.where(forced, FORCE_BONUS, jnp.where(valid, imp, -1.0))
        val_t = val.T
        sel_t = jnp.zeros((LANES, qc), F32)
        for _ in range(n_top):
            mx = jnp.max(val_t, axis=0, keepdims=True)
            first = jnp.min(jnp.where(val_t == mx, blk_iota, LANES), axis=0, keepdims=True)
            hit = blk_iota == first
            sel_t = jnp.where(hit, 1.0, sel_t)
            val_t = jnp.where(hit, -2.0, val_t)
        chosen_t = (sel_t > 0.5) & (blk_iota < cur_row)
        selbt_ref[0, g] = jnp.where(chosen_t, 0.0, NEG_INF).astype(BF16)


def cmp_select(proj3, kcmp, vcmp, overlap, *, qc, n_top):
    bsz, T, _ = proj3.shape
    ncmp = kcmp.shape[2]
    return pl.pallas_call(
        functools.partial(_cmp_select_kernel, qc=qc, n_top=n_top),
        grid=(bsz, T // qc),
        in_specs=[
            pl.BlockSpec((1, qc, A_WIDTH), lambda b, c: (b, c, OFF_A_Q // A_WIDTH)),
            pl.BlockSpec((1, A_KV_HEADS, ncmp, HEAD_DIM), lambda b, c: (b, 0, 0, 0)),
            pl.BlockSpec((1, A_KV_HEADS, ncmp, HEAD_DIM), lambda b, c: (b, 0, 0, 0)),
            pl.BlockSpec((ncmp, LANES), lambda b, c: (0, 0)),
        ],
        out_specs=[
            pl.BlockSpec((1, qc, A_WIDTH), lambda b, c: (b, c, 0)),
            pl.BlockSpec((1, A_KV_HEADS, LANES, qc), lambda b, c: (b, 0, 0, c)),
        ],
        out_shape=[jax.ShapeDtypeStruct((bsz, T, A_WIDTH), F32),
                   jax.ShapeDtypeStruct((bsz, A_KV_HEADS, LANES, T), BF16)],
        compiler_params=_cparams(("parallel", "parallel")),
        name="nsa_cmp_select",
    )(proj3, kcmp, vcmp, overlap)


def _slc_kernel(q_ref, selbt_ref, kaug_ref, vaugt_ref, arow_ref, slope_ref, dbias_ref, o_ref,
                qaug_ref, m_ref, acc_ref, *s_refs, qc, tk):
    t0 = pl.multiple_of(pl.program_id(2) * qc, qc)
    selbt = selbt_ref[0, 0]
    qt = (q_ref[0] * SCALE).T.astype(BF16)
    for hh in range(HEADS_PER_GROUP):
        cols = slice(hh * qc, (hh + 1) * qc)
        qaug_ref[0:LANES, cols] = selbt
        qaug_ref[LANES:LANES + HEAD_DIM, cols] = qt[hh * HEAD_DIM:(hh + 1) * HEAD_DIM, :]
    qaug_ref[LANES + HEAD_DIM:AUG_DEPTH, :] = arow_ref[0]
    m_ref[...] = jnp.full(m_ref.shape, NEG_INF, F32)
    acc_ref[...] = jnp.zeros(acc_ref.shape, F32)

    bufs = (s_refs[:HEADS_PER_GROUP], s_refs[HEADS_PER_GROUP:])

    def scores(kt, hh, dst):
        s0 = pl.multiple_of(kt * tk, tk)
        dst[hh][...] = _dot(kaug_ref[0, 0, pl.ds(s0, tk), :], qaug_ref[:, hh * qc:(hh + 1) * qc])

    def diag_scores(hh, dst):
        ka = kaug_ref[0, 0, pl.ds(t0, qc), LANES:AUG_DEPTH]
        st = _dot(ka, qaug_ref[LANES:AUG_DEPTH, hh * qc:(hh + 1) * qc])
        dst[hh][0:qc, :] = st + dbias_ref[...]

    def softmax_pv(hh, src, rows, vt, off):
        c = slope_ref[0, hh] * off
        m_old = m_ref[hh]
        m_new = jnp.maximum(m_old, jnp.max(src[hh][0:rows, :], axis=0, keepdims=True) + c)
        alpha = jnp.exp(m_old - m_new)
        p = jnp.exp(src[hh][0:rows, :] - (m_new - c)).astype(BF16)
        acc_ref[hh] = alpha * acc_ref[hh] + _dot(vt, p)
        m_ref[hh] = m_new

    def step(kt, src, next_scores):
        s0 = pl.multiple_of(kt * tk, tk)
        vt = vaugt_ref[0, 0, :, pl.ds(s0, tk)]
        off = (s0 - t0).astype(F32)
        for hh in range(HEADS_PER_GROUP):
            next_scores(hh)
            softmax_pv(hh, src, tk, vt, off)

    def main_step(kt, src, dst):
        step(kt, src, lambda hh: scores(kt + 1, hh, dst))

    n_full = lax.div(t0, tk)
    odd = lax.rem(n_full, 2)

    @pl.when(odd == 0)
    def _():
        for hh in range(HEADS_PER_GROUP):
            scores(0, hh, bufs[0])

    @pl.when(odd == 1)
    def _():
        for hh in range(HEADS_PER_GROUP):
            scores(0, hh, bufs[1])
        main_step(0, bufs[1], bufs[0])

    def body(i, carry):
        kt = odd + 2 * i
        main_step(kt, bufs[0], bufs[1])
        main_step(kt + 1, bufs[1], bufs[0])
        return carry

    lax.fori_loop(0, lax.div(n_full, 2), body, 0)
    step(n_full, bufs[0], lambda hh: diag_scores(hh, bufs[1]))
    base = pl.multiple_of(n_full * tk, tk)
    vt_d = vaugt_ref[0, 0, :, pl.ds(t0, qc)]
    for hh in range(HEADS_PER_GROUP):
        softmax_pv(hh, bufs[1], qc, vt_d, (base - t0).astype(F32))
    ot = jnp.concatenate([acc_ref[hh, 0:HEAD_DIM, :] / acc_ref[hh, HEAD_DIM:HEAD_DIM + 1, :]
                          for hh in range(HEADS_PER_GROUP)], axis=0)
    o_ref[0] = ot.T


def slc_attention(proj3, selbt, kaug, vaugt, arows, slopes, *, qc, tk):
    bsz, T, _ = proj3.shape
    G = A_KV_HEADS
    assert tk % qc == 0 and qc % SEL_BLOCK == 0
    ki = np.arange(qc)[:, None]
    qi = np.arange(qc)[None, :]
    dbias = jnp.asarray(np.where((ki // SEL_BLOCK == qi // SEL_BLOCK) & (ki <= qi), 0.0, NEG_INF), dtype=F32)
    return pl.pallas_call(
        functools.partial(_slc_kernel, qc=qc, tk=tk),
        grid=(bsz, G, T // qc),
        in_specs=[
            pl.BlockSpec((1, qc, GROUP_WIDTH), lambda b, g, c: (b, c, OFF_A_Q // GROUP_WIDTH + g)),
            pl.BlockSpec((1, 1, LANES, qc), lambda b, g, c: (b, g, 0, c)),
            pl.BlockSpec((1, 1, T, AUG_DEPTH), lambda b, g, c: (b, g, 0, 0)),
            pl.BlockSpec((1, 1, LANES, T), lambda b, g, c: (b, g, 0, 0)),
            pl.BlockSpec((1, AUG_DEPTH - LANES - HEAD_DIM, HEADS_PER_GROUP * qc), lambda b, g, c: (g, 0, 0)),
            pl.BlockSpec((1, HEADS_PER_GROUP, 1, qc), lambda b, g, c: (g, 0, 0, 0)),
            pl.BlockSpec((qc, qc), lambda b, g, c: (0, 0)),
        ],
        out_specs=pl.BlockSpec((1, qc, GROUP_WIDTH), lambda b, g, c: (b, c, g)),
        out_shape=jax.ShapeDtypeStruct((bsz, T, A_WIDTH), F32),
        scratch_shapes=[
            pltpu.VMEM((AUG_DEPTH, HEADS_PER_GROUP * qc), BF16),
            pltpu.VMEM((HEADS_PER_GROUP, 1, qc), F32),
            pltpu.VMEM((HEADS_PER_GROUP, LANES, qc), F32),
        ] + [pltpu.VMEM((tk, qc), F32)] * (2 * HEADS_PER_GROUP),
        compiler_params=_cparams(("parallel", "parallel", "arbitrary")),
        name="nsa_slc_attention",
    )(proj3, selbt, kaug, vaugt, arows, slopes, dbias)


def _band_kernel(*refs, qc, window, has_sinks, gated):
    q_ref, k_ref, v_ref, slope_ref = refs[:4]
    pos = 4
    sink_ref = z_ref = None
    if has_sinks:
        sink_ref = refs[pos]
        pos += 1
    if gated:
        z_ref = refs[pos]
        pos += 1
    o_ref = refs[pos]
    span = qc + window
    t0 = pl.multiple_of(pl.program_id(2) * qc, qc)
    kk = k_ref[0, 0, pl.ds(t0, span), :]
    vv = v_ref[0, 0, pl.ds(t0, span), :]
    key_rel = lax.broadcasted_iota(jnp.int32, (1, span), 1) - window
    row_iota = lax.broadcasted_iota(jnp.int32, (qc, 1), 0)
    dist = row_iota - key_rel
    band = (dist >= 0) & (dist < window) & ((key_rel + t0) >= 0)
    rel = (-dist).astype(F32) if has_sinks else key_rel.astype(F32)
    for hh in range(HEADS_PER_GROUP):
        lo = hh * HEAD_DIM
        qh = q_ref[0, :, lo:lo + HEAD_DIM].astype(BF16)
        s = _dot_nt(qh, kk) * SCALE + slope_ref[0, hh][:, 0:1] * rel
        s = jnp.where(band, s, NEG_INF)
        mx = jnp.max(s, axis=-1, keepdims=True)
        if has_sinks:
            sk = sink_ref[0, hh][:, 0:1]
            mx = jnp.maximum(mx, sk)
        e = jnp.exp(s - mx)
        den = jnp.sum(e, axis=-1, keepdims=True)
        if has_sinks:
            den = den + jnp.exp(sk - mx)
        o = _dot(e.astype(BF16), vv) / den
        if gated:
            o = o * _silu(z_ref[0, :, lo:lo + HEAD_DIM])
        o_ref[0, :, lo:lo + HEAD_DIM] = o.astype(o_ref.dtype)


def band_attention(proj3, kpad, vpad, slopes, sinks, *, q_off, z_off, qc, window, out_dtype):
    bsz, T, _ = proj3.shape
    G = kpad.shape[1]
    has_sinks = sinks is not None
    gated = z_off is not None
    small = pl.BlockSpec((1, HEADS_PER_GROUP, 1, LANES), lambda b, g, c: (g, 0, 0, 0))
    kv_spec = pl.BlockSpec((1, 1, T + window, HEAD_DIM), lambda b, g, c: (b, g, 0, 0))
    in_specs = [pl.BlockSpec((1, qc, GROUP_WIDTH), lambda b, g, c: (b, c, q_off // GROUP_WIDTH + g)),
                kv_spec, kv_spec, small]
    args = [proj3, kpad, vpad, slopes]
    if has_sinks:
        in_specs.append(small)
        args.append(sinks)
    if gated:
        in_specs.append(pl.BlockSpec((1, qc, GROUP_WIDTH), lambda b, g, c: (b, c, z_off // GROUP_WIDTH + g)))
        args.append(proj3)
    return pl.pallas_call(
        functools.partial(_band_kernel, qc=qc, window=window, has_sinks=has_sinks, gated=gated),
        grid=(bsz, G, T // qc),
        in_specs=in_specs,
        out_specs=pl.BlockSpec((1, qc, GROUP_WIDTH), lambda b, g, c: (b, c, g)),
        out_shape=jax.ShapeDtypeStruct((bsz, T, G * GROUP_WIDTH), out_dtype),
        compiler_params=_cparams(("parallel", "parallel", "parallel")),
        name="band_attention_w%d" % window,
    )(*args)


def _mix_a_kernel(ocmp_ref, oslc_ref, owin_ref, gate_ref, z_ref, y_ref):
    gate = _sigmoid(gate_ref[...])
    for h in range(A_HEADS):
        lo = h * HEAD_DIM
        sl = slice(lo, lo + HEAD_DIM)
        o = (gate[:, 3 * h:3 * h + 1] * ocmp_ref[:, sl]
             + gate[:, 3 * h + 1:3 * h + 2] * oslc_ref[:, sl]
             + gate[:, 3 * h + 2:3 * h + 3] * owin_ref[:, sl])
        y_ref[:, sl] = (o * _silu(z_ref[:, sl])).astype(y_ref.dtype)


def mix_a(ocmp, oslc, owin, proj, *, tm):
    n = proj.shape[0]
    row = pl.BlockSpec((tm, A_WIDTH), lambda i: (i, 0))
    return pl.pallas_call(
        _mix_a_kernel,
        grid=(n // tm,),
        in_specs=[row, row, row,
                  pl.BlockSpec((tm, LANES), lambda i: (i, OFF_A_GATE // LANES)),
                  pl.BlockSpec((tm, A_WIDTH), lambda i: (i, OFF_A_Z // A_WIDTH))],
        out_specs=row,
        out_shape=jax.ShapeDtypeStruct((n, A_WIDTH), BF16),
        compiler_params=_cparams(("parallel",)),
        name="nsa_gate_mix",
    )(ocmp, oslc, owin, proj, proj)


def _log_sigmoid(x):
    return jnp.minimum(x, 0.0) - jnp.log(1.0 + jnp.exp(-jnp.abs(x)))


def _mlstm_kernel(q_ref, k_ref, v_ref, og_ref, z_ref, ic_ref, fc_ref, ir_ref, fr_ref, ib_ref, fb_ref,
                  cwq_ref, cwk_ref, cbq_ref, cbk_ref, g_ref, y_ref,
                  xq_ref, xk_ref, c_ref, n_ref, m_ref, *, L):
    @pl.when(pl.program_id(2) == 0)
    def _():
        xq_ref[0:8, :] = jnp.zeros((8, B_HEAD_DIM), F32)
        xk_ref[0:8, :] = jnp.zeros((8, B_HEAD_DIM), F32)
        c_ref[...] = jnp.zeros(c_ref.shape, F32)
        n_ref[...] = jnp.zeros(n_ref.shape, F32)
        m_ref[...] = jnp.zeros(m_ref.shape, F32)

    def conv_silu(x_ref, hist_ref, w_ref, b_ref):
        hist_ref[8:8 + L, :] = x_ref[0]
        y = b_ref[...]
        for i in range(CONV_WIDTH):
            y = y + w_ref[i:i + 1, :] * hist_ref[5 + i:5 + i + L, :]
        hist_ref[0:8, :] = hist_ref[L:L + 8, :]
        return _silu(y)

    q = conv_silu(q_ref, xq_ref, cwq_ref, cbq_ref)
    k = conv_silu(k_ref, xk_ref, cwk_ref, cbk_ref) * (B_HEAD_DIM ** -0.5)
    v = v_ref[0]
    i_col = ic_ref[0, 0] + ib_ref[0]
    i_row = ir_ref[0, 0] + ib_ref[0]
    lf_col = _log_sigmoid(fc_ref[0, 0] + fb_ref[0])
    lf_row = _log_sigmoid(fr_ref[0, 0] + fb_ref[0])
    r_i = lax.broadcasted_iota(jnp.int32, (L, L), 0)
    c_i = lax.broadcasted_iota(jnp.int32, (L, L), 1)
    causal = c_i <= r_i
    b_col = jnp.sum(jnp.where(causal, lf_row, 0.0), axis=1, keepdims=True)
    b_row = jnp.sum(jnp.where(r_i <= c_i, lf_col, 0.0), axis=0, keepdims=True)
    b_last = jnp.sum(lf_row, axis=1, keepdims=True)
    m_prev = m_ref[...]
    log_d = jnp.where(causal, b_col - b_row + i_row, NEG_INF)
    log_inter = b_col + m_prev
    m_t = jnp.maximum(log_inter, jnp.max(log_d, axis=1, keepdims=True))
    w_intra = jnp.exp(log_d - m_t)
    w_inter = jnp.exp(log_inter - m_t)
    qb = q.astype(BF16)
    vb = v.astype(BF16)
    qk = _dot_nt(qb, k.astype(BF16)) * w_intra
    num = w_inter * _dot(qb, c_ref[...].astype(BF16)) + _dot(qk.astype(BF16), vb)
    den = (w_inter * jnp.sum(q * n_ref[...], axis=1, keepdims=True)
           + jnp.sum(qk, axis=1, keepdims=True))
    h = num / jnp.maximum(jnp.abs(den), jnp.exp(-m_t))
    log_g_row = b_last - b_row + i_row
    m_new = jnp.maximum(b_last + m_prev, jnp.max(log_g_row, axis=1, keepdims=True))
    w_g = jnp.exp(b_last - b_col + i_col - m_new)
    decay = jnp.exp(b_last + m_prev - m_new)
    kw = k * w_g
    c_ref[...] = decay * c_ref[...] + _dot_tn(kw.astype(BF16), vb)
    n_ref[...] = decay * n_ref[...] + jnp.sum(kw, axis=0, keepdims=True)
    m_ref[...] = m_new
    hb = _sigmoid(og_ref[0]) * h
    ms = jnp.mean(hb * hb, axis=-1, keepdims=True)
    hb = hb * lax.rsqrt(ms + EPS) * g_ref[...]
    y_ref[0] = (hb * _silu(z_ref[0])).astype(y_ref.dtype)


def mlstm(proj3, if_col, if_row, i_bias, f_bias, conv_w, conv_b, norm_g, *, L):
    bsz, T, _ = proj3.shape
    H = B_HEADS
    D = B_HEAD_DIM

    def col(off, k_half=False):
        base = off // D + (H if k_half else 0)
        return pl.BlockSpec((1, L, D), lambda b, h, c: (b, c, base + h))

    def par(rows, k_half=False):
        base = H if k_half else 0
        return pl.BlockSpec((rows, D), lambda b, h, c: (0, base + h))

    in_specs = [
        col(OFF_B_QK), col(OFF_B_QK, True), col(OFF_B_V), col(OFF_B_O), col(OFF_B_Z),
        pl.BlockSpec((1, 1, L, 1), lambda b, h, c: (b, h, c, 0)),
        pl.BlockSpec((1, 1, L, 1), lambda b, h, c: (b, H + h, c, 0)),
        pl.BlockSpec((1, 1, 1, L), lambda b, h, c: (b, h, 0, c)),
        pl.BlockSpec((1, 1, 1, L), lambda b, h, c: (b, H + h, 0, c)),
        pl.BlockSpec((1, 1, 1), lambda b, h, c: (h, 0, 0)),
        pl.BlockSpec((1, 1, 1), lambda b, h, c: (h, 0, 0)),
        par(CONV_WIDTH), par(CONV_WIDTH, True), par(1), par(1, True),
        pl.BlockSpec((1, D), lambda b, h, c: (0, h)),
    ]
    return pl.pallas_call(
        functools.partial(_mlstm_kernel, L=L),
        grid=(bsz, H, T // L),
        in_specs=in_specs,
        out_specs=pl.BlockSpec((1, L, D), lambda b, h, c: (b, c, h)),
        out_shape=jax.ShapeDtypeStruct((bsz, T, B_WIDTH), BF16),
        scratch_shapes=[
            pltpu.VMEM((L + 8, D), F32), pltpu.VMEM((L + 8, D), F32),
            pltpu.VMEM((D, D), F32), pltpu.VMEM((1, D), F32), pltpu.VMEM((1, 1), F32),
        ],
        compiler_params=_cparams(("parallel", "parallel", "arbitrary")),
        name="mlstm_chunkwise",
    )(proj3, proj3, proj3, proj3, proj3, if_col, if_col, if_row, if_row, i_bias, f_bias,
      conv_w, conv_w, conv_b, conv_b, norm_g)


IN_SPLITS = (A_WIDTH, 128, 128, 128, 128, 128, 128, A_HEADS * 3, A_WIDTH,
             2 * B_WIDTH, B_WIDTH, B_HEADS, B_HEADS, B_WIDTH, B_WIDTH,
             C_WIDTH, 128, 128, C_WIDTH)


def _permute_w_in(w_in):
    starts = np.concatenate([[0], np.cumsum(IN_SPLITS)])
    seg = lambda i: w_in[:, int(starts[i]):int(starts[i + 1])]
    (a_q, a_kc, a_vc, a_ks, a_vs, a_kw, a_vw, a_gate, a_z,
     b_qk, b_v, b_i, b_f, b_o, b_z, c_q, c_k, c_v, c_z) = [seg(i) for i in range(len(IN_SPLITS))]
    d = w_in.shape[0]
    pad = lambda a: jnp.pad(a, ((0, 0), (0, LANES - a.shape[1])))
    cols = [b_qk, b_v, b_o, b_z, a_q, a_z, c_q, c_z,
            a_kc, a_vc, a_ks, a_vs, a_kw, a_vw, c_k, c_v,
            pad(a_gate), pad(jnp.concatenate([b_i, b_f], axis=1))]
    out = jnp.concatenate(cols, axis=1)
    assert out.shape == (d, PROJ_COLS)
    return out.astype(BF16)


def _alibi(n_heads):
    return 2.0 ** (-8.0 * jnp.arange(1, n_heads + 1, dtype=F32) / n_heads)


def _alibi_np(n_heads):
    s = (2.0 ** (-8.0 * np.arange(1, n_heads + 1, dtype=np.float64) / n_heads)).astype(np.float32)
    for v in (s, SEL_BLOCK * s):
        assert np.array_equal(v.astype(jnp.bfloat16).astype(np.float32), v), "slopes not bf16-exact"
    return s


def _lane_rep(v, groups, width=LANES):
    return jnp.broadcast_to(v.astype(F32).reshape(groups, -1, 1, 1), (groups, v.shape[0] // groups, 1, width))


def _kv_heads(a, bsz, T, G):
    return a.reshape(bsz, T, G, HEAD_DIM).transpose(0, 2, 1, 3)


def _mixers(x2d, bsz, T, p, cfg):
    n, d = x2d.shape
    G = A_KV_HEADS
    proj = norm_matmul(x2d, p["norm_g"].reshape(1, d), _permute_w_in(p["w_in"]), tm=cfg["tm_in"], tn=cfg["tn_in"])
    proj3 = proj.reshape(bsz, T, PROJ_COLS)
    akv = proj3[:, :, OFF_A_KV:OFF_A_KV + 6 * 128]
    a_kc, a_vc, a_ks, a_vs, a_kw, a_vw = [akv[:, :, i * 128:(i + 1) * 128] for i in range(6)]
    c_k = proj3[:, :, OFF_C_KV:OFF_C_KV + 128]
    c_v = proj3[:, :, OFF_C_KV + 128:OFF_C_KV + 256]

    nseg = T // CMP_STRIDE
    seg = jnp.stack([a_kc, a_vc]).reshape(2, bsz, nseg, CMP_STRIDE, G, HEAD_DIM)
    seg = seg.transpose(0, 1, 4, 2, 3, 5).reshape(2, bsz, G, nseg, CMP_STRIDE * HEAD_DIM)
    pe = jnp.stack([p["cmp_pe_k"], p["cmp_pe_v"]])
    w1 = jnp.stack([p["cmp_w1_k"], p["cmp_w1_v"]]).astype(BF16)
    w2 = jnp.stack([p["cmp_w2_k"], p["cmp_w2_v"]]).astype(BF16)
    half = CMP_STRIDE * HEAD_DIM
    kv_cmp = nsa_compress(seg, pe[:, :CMP_STRIDE].reshape(2, 1, half), pe[:, CMP_STRIDE:].reshape(2, 1, half),
                          w1[:, :half], w1[:, half:], w2)
    k_cmp, v_cmp = kv_cmp[0], kv_cmp[1]

    n_sel = T // SEL_BLOCK
    assert n_sel <= MAX_SEL_BLOCKS
    ci = np.arange(nseg)[:, None] * CMP_STRIDE
    sj = np.arange(LANES)[None, :] * SEL_BLOCK
    overlap = jnp.asarray(((ci < sj + SEL_BLOCK) & (ci + CMP_BLOCK > sj)).astype(np.float32), dtype=BF16)
    o_cmp, selb = cmp_select(proj3, k_cmp, v_cmp, overlap, qc=cfg["qc_cmp"], n_top=min(SEL_TOPK, n_sel))

    slopes_a = _lane_rep(_alibi(A_HEADS), G)
    qc_s, tk_s = cfg["qc_slc"], cfg["tk_slc"]
    pos = np.arange(T)
    key_feat = np.zeros((T, AUG_DEPTH - LANES - HEAD_DIM), np.float32)
    key_feat[:, 0] = pos % SEL_BLOCK
    key_feat[:, 1] = (pos // SEL_BLOCK) % (tk_s // SEL_BLOCK)
    onehot = (pos[:, None] // SEL_BLOCK == np.arange(LANES)[None, :]).astype(np.float32)
    bcast = lambda a: jnp.broadcast_to(jnp.asarray(a, dtype=BF16), (bsz, G) + a.shape)
    kaug = jnp.concatenate([bcast(onehot), _kv_heads(a_ks, bsz, T, G).astype(BF16), bcast(key_feat)], axis=-1)
    v_t = a_vs.reshape(bsz, T, G, HEAD_DIM).transpose(0, 2, 3, 1).astype(BF16)
    vaugt = jnp.concatenate([v_t, jnp.ones((bsz, G, 1, T), BF16),
                             jnp.zeros((bsz, G, LANES - HEAD_DIM - 1, T), BF16)], axis=2)
    sl_np = _alibi_np(A_HEADS)
    arows = np.zeros((A_HEADS, AUG_DEPTH - LANES - HEAD_DIM, qc_s), np.float32)
    arows[:, 0, :] = sl_np[:, None]
    arows[:, 1, :] = SEL_BLOCK * sl_np[:, None]
    arows = arows.reshape(G, HEADS_PER_GROUP, -1, qc_s).transpose(0, 2, 1, 3).reshape(G, -1, HEADS_PER_GROUP * qc_s)
    o_slc = slc_attention(proj3, selb, kaug, vaugt, jnp.asarray(arows, dtype=BF16),
                          _lane_rep(jnp.asarray(sl_np), G, qc_s), qc=qc_s, tk=tk_s)

    def padded(a, window):
        return jnp.pad(_kv_heads(a, bsz, T, a.shape[-1] // HEAD_DIM).astype(BF16),
                       ((0, 0), (0, 0), (window, 0), (0, 0)))

    o_win = band_attention(proj3, padded(a_kw, NSA_WINDOW), padded(a_vw, NSA_WINDOW), slopes_a, None,
                           q_off=OFF_A_Q, z_off=None, qc=cfg["qc_win"], window=NSA_WINDOW, out_dtype=F32)
    y_a = mix_a(o_cmp.reshape(n, A_WIDTH), o_slc.reshape(n, A_WIDTH), o_win.reshape(n, A_WIDTH), proj,
                tm=cfg["tm_mix"])

    if_t = proj3[:, :, OFF_B_IF:OFF_B_IF + 2 * B_HEADS].transpose(0, 2, 1)
    y_b = mlstm(proj3, if_t[..., None], if_t[:, :, None, :],
                p["mlstm_i_bias"].reshape(B_HEADS, 1, 1), p["mlstm_f_bias"].reshape(B_HEADS, 1, 1),
                p["mlstm_conv_w"], p["mlstm_conv_b"].reshape(1, -1), p["mlstm_norm_g"].reshape(1, -1),
                L=cfg["mlstm_chunk"])

    y_c = band_attention(proj3, padded(c_k, SWA_WINDOW), padded(c_v, SWA_WINDOW), _lane_rep(_alibi(C_HEADS), C_KV_HEADS),
                         _lane_rep(p["swa_sinks"], C_KV_HEADS), q_off=OFF_C_Q, z_off=OFF_C_Z,
                         qc=cfg["qc_win"], window=SWA_WINDOW, out_dtype=BF16)

    return y_a, y_b.reshape(n, B_WIDTH), y_c.reshape(n, C_WIDTH), (o_cmp, o_slc, o_win, selb)


def _layer(x2d, bsz, T, p, final_g, cfg):
    n, d = x2d.shape
    y_a, y_b, y_c, _ = _mixers(x2d, bsz, T, p, cfg)
    w_out = p["w_out"].astype(BF16)
    g = final_g if final_g is not None else p["norm_g"]
    return out_proj(y_a, y_b, y_c, x2d,
                    w_out[:A_WIDTH], w_out[A_WIDTH:A_WIDTH + B_WIDTH], w_out[A_WIDTH + B_WIDTH:],
                    g.reshape(1, d), tm=cfg["tm_out"], final_norm=final_g is not None)


def _config(T):
    return dict(tm_in=min(1024, T), tn_in=768, tm_out=min(512, T), tm_mix=min(512, T),
                qc_cmp=128, qc_slc=256, tk_slc=512, qc_win=128, mlstm_chunk=min(256, T))


def kernel(x, norm_g, w_in, w_out, cmp_pe_k, cmp_w1_k, cmp_w2_k, cmp_pe_v, cmp_w1_v, cmp_w2_v, mlstm_conv_w, mlstm_conv_b, mlstm_i_bias, mlstm_f_bias, mlstm_norm_g, swa_sinks, final_norm_g):
    bsz, T, d = x.shape
    depth = w_in.shape[0]
    cfg = _config(T)
    x2d = x.reshape(bsz * T, d)
    for l in range(depth):
        p = dict(norm_g=norm_g[l], w_in=w_in[l], w_out=w_out[l],
                 cmp_pe_k=cmp_pe_k[l], cmp_w1_k=cmp_w1_k[l], cmp_w2_k=cmp_w2_k[l],
                 cmp_pe_v=cmp_pe_v[l], cmp_w1_v=cmp_w1_v[l], cmp_w2_v=cmp_w2_v[l],
                 mlstm_conv_w=mlstm_conv_w[l], mlstm_conv_b=mlstm_conv_b[l],
                 mlstm_i_bias=mlstm_i_bias[l], mlstm_f_bias=mlstm_f_bias[l],
                 mlstm_norm_g=mlstm_norm_g[l], swa_sinks=swa_sinks[l])
        x2d = _layer(x2d, bsz, T, p, final_norm_g if l == depth - 1 else None, cfg)
    return x2d.reshape(bsz, T, d)
```

```python
import functools

import numpy as np
import jax
import jax.numpy as jnp
from jax import lax
from jax.experimental import pallas as pl
from jax.experimental.pallas import tpu as pltpu

F32 = jnp.float32
BF16 = jnp.bfloat16

HEAD_DIM = 64
A_HEADS = 8
A_KV_HEADS = 2
HEADS_PER_GROUP = A_HEADS // A_KV_HEADS
GROUP_WIDTH = HEADS_PER_GROUP * HEAD_DIM
A_WIDTH = A_HEADS * HEAD_DIM
C_HEADS = 8
C_KV_HEADS = 2
C_WIDTH = C_HEADS * HEAD_DIM
B_HEADS = 4
B_HEAD_DIM = 256
B_WIDTH = B_HEADS * B_HEAD_DIM
CMP_BLOCK = 32
CMP_STRIDE = 16
SEL_BLOCK = 64
SEL_TOPK = 16
SEL_LOCAL = 2
NSA_WINDOW = 512
SWA_WINDOW = 128
CONV_WIDTH = 4
FORCE_BONUS = 1.0e4
NEG_INF = -1.0e30
EPS = 1.0e-6
SCALE = HEAD_DIM ** -0.5
LANES = 128
MAX_SEL_BLOCKS = LANES
AUG_DEPTH = 256

IN_SPLITS = (A_WIDTH, 128, 128, 128, 128, 128, 128, A_HEADS * 3, A_WIDTH,
             2 * B_WIDTH, B_WIDTH, B_HEADS, B_HEADS, B_WIDTH, B_WIDTH,
             C_WIDTH, 128, 128, C_WIDTH)
IN_COLS = sum(IN_SPLITS)
_SRC = np.concatenate([[0], np.cumsum(IN_SPLITS)]).astype(int)
SRC_GATE, SRC_A_Z, SRC_B_IF, SRC_B_O = int(_SRC[7]), int(_SRC[8]), int(_SRC[11]), int(_SRC[13])
N_GATE = A_HEADS * 3
N_IF = 2 * B_HEADS
OFF_A_Q = 0
OFF_A_KV = 512
OFF_A_Z = 1280
OFF_B_QK = 1792
OFF_B_V = 3840
OFF_B_O = 4864
OFF_B_Z = 5888
OFF_C_Q = 6912
OFF_C_KV = 7424
OFF_C_Z = 7680
OFF_A_GATE = 8192
OFF_B_IF = 8320
PROJ_COLS = 8448
W_RUNS = ((0, OFF_A_Q, SRC_GATE), (SRC_A_Z, OFF_A_Z, SRC_B_IF - SRC_A_Z), (SRC_B_O, OFF_B_O, IN_COLS - SRC_B_O))
assert OFF_A_Z + (SRC_B_IF - SRC_A_Z) == OFF_B_O and OFF_B_O + (IN_COLS - SRC_B_O) == OFF_A_GATE

VMEM_LIMIT = 48 * 1024 * 1024


def _cparams(sem):
    return pltpu.CompilerParams(dimension_semantics=sem, vmem_limit_bytes=VMEM_LIMIT)


def _dot(a, b):
    return jnp.dot(a, b, preferred_element_type=F32)


def _dot_nt(a, b):
    return lax.dot_general(a, b, (((1,), (1,)), ((), ())), preferred_element_type=F32)


def _dot_tn(a, b):
    return lax.dot_general(a, b, (((0,), (0,)), ((), ())), preferred_element_type=F32)


def _sigmoid(x):
    return 1.0 / (1.0 + jnp.exp(-x))


def _silu(x):
    return x * _sigmoid(x)


def _heads_to_rows(ot, qc):
    return jnp.concatenate([ot[:, hh * qc:(hh + 1) * qc] for hh in range(HEADS_PER_GROUP)], axis=0).T


def _repack_kernel(w_ref, o_ref):
    for src, dst, width in W_RUNS:
        o_ref[:, dst:dst + width] = w_ref[:, src:src + width].astype(BF16)
    o_ref[:, OFF_A_GATE:PROJ_COLS] = jnp.zeros((o_ref.shape[0], PROJ_COLS - OFF_A_GATE), BF16)
    o_ref[:, OFF_A_GATE:OFF_A_GATE + N_GATE] = w_ref[:, SRC_GATE:SRC_GATE + N_GATE].astype(BF16)
    o_ref[:, OFF_B_IF:OFF_B_IF + N_IF] = w_ref[:, SRC_B_IF:SRC_B_IF + N_IF].astype(BF16)


def repack_w_in(w_in, *, tr):
    d = w_in.shape[0]
    return pl.pallas_call(
        _repack_kernel,
        grid=(d // tr,),
        in_specs=[pl.BlockSpec((tr, IN_COLS), lambda i: (i, 0))],
        out_specs=pl.BlockSpec((tr, PROJ_COLS), lambda i: (i, 0)),
        out_shape=jax.ShapeDtypeStruct((d, PROJ_COLS), BF16),
        compiler_params=_cparams(("parallel",)),
        name="repack_w_in",
    )(w_in)


def _norm_matmul_kernel(x_ref, g_ref, w_ref, o_ref, h_ref):
    @pl.when(pl.program_id(1) == 0)
    def _():
        x = x_ref[...]
        ms = jnp.mean(x * x, axis=-1, keepdims=True)
        h_ref[...] = (x * lax.rsqrt(ms + EPS) * g_ref[...]).astype(BF16)

    o_ref[...] = _dot(h_ref[...], w_ref[...])


def norm_matmul(x, g, w, *, tm, tn):
    n, d = x.shape
    cols = w.shape[1]
    return pl.pallas_call(
        _norm_matmul_kernel,
        grid=(n // tm, cols // tn),
        in_specs=[
            pl.BlockSpec((tm, d), lambda i, j: (i, 0)),
            pl.BlockSpec((1, d), lambda i, j: (0, 0)),
            pl.BlockSpec((d, tn), lambda i, j: (0, j)),
        ],
        out_specs=pl.BlockSpec((tm, tn), lambda i, j: (i, j)),
        out_shape=jax.ShapeDtypeStruct((n, cols), F32),
        scratch_shapes=[pltpu.VMEM((tm, d), BF16)],
        compiler_params=_cparams(("parallel", "arbitrary")),
        name="norm_in_proj",
    )(x, g, w)


def _out_proj_kernel(ya_ref, yb_ref, yc_ref, x_ref, wa_ref, wb_ref, wc_ref, g_ref, o_ref, *, final_norm):
    acc = _dot(ya_ref[...], wa_ref[...]) + _dot(yb_ref[...], wb_ref[...]) + _dot(yc_ref[...], wc_ref[...])
    y = x_ref[...] + acc
    if final_norm:
        ms = jnp.mean(y * y, axis=-1, keepdims=True)
        y = y * lax.rsqrt(ms + EPS) * g_ref[...]
    o_ref[...] = y


def out_proj(ya, yb, yc, x, wa, wb, wc, g, *, tm, final_norm):
    n, d = x.shape
    row = lambda w: pl.BlockSpec((tm, w), lambda i: (i, 0))
    full = lambda a: pl.BlockSpec(a.shape, lambda i: (0, 0))
    return pl.pallas_call(
        functools.partial(_out_proj_kernel, final_norm=final_norm),
        grid=(n // tm,),
        in_specs=[row(ya.shape[1]), row(yb.shape[1]), row(yc.shape[1]), row(d),
                  full(wa), full(wb), full(wc), full(g)],
        out_specs=row(d),
        out_shape=jax.ShapeDtypeStruct((n, d), F32),
        compiler_params=_cparams(("parallel",)),
        name="out_proj",
    )(ya, yb, yc, x, wa, wb, wc, g)


def _compress_kernel(seg_ref, pea_ref, peb_ref, w1a_ref, w1b_ref, w2_ref, o_ref, ot_ref, shift_ref):
    nseg = seg_ref.shape[3]
    seg = seg_ref[0, 0, 0]
    u1 = _dot((seg + pea_ref[0]).astype(BF16), w1a_ref[0])
    u2 = _dot((seg + peb_ref[0]).astype(BF16), w1b_ref[0])
    shift_ref[0:nseg, :] = u2
    shift_ref[nseg:nseg + 8, :] = jnp.zeros((8, HEAD_DIM), F32)
    pre = u1 + shift_ref[1:nseg + 1, :]
    out = _dot(_silu(pre).astype(BF16), w2_ref[0])
    o_ref[0, 0, 0] = out.astype(BF16)
    ot_ref[0, 0, 0] = out.T.astype(BF16)


def nsa_compress(seg, pea, peb, w1a, w1b, w2):
    _, bsz, G, nseg, width = seg.shape
    par = lambda a: pl.BlockSpec((1,) + a.shape[1:], lambda s, b, g: (s,) + (0,) * (a.ndim - 1))
    return pl.pallas_call(
        _compress_kernel,
        grid=(2, bsz, G),
        in_specs=[pl.BlockSpec((1, 1, 1, nseg, width), lambda s, b, g: (s, b, g, 0, 0)),
                  par(pea), par(peb), par(w1a), par(w1b), par(w2)],
        out_specs=[pl.BlockSpec((1, 1, 1, nseg, HEAD_DIM), lambda s, b, g: (s, b, g, 0, 0)),
                   pl.BlockSpec((1, 1, 1, HEAD_DIM, nseg), lambda s, b, g: (s, b, g, 0, 0))],
        out_shape=[jax.ShapeDtypeStruct((2, bsz, G, nseg, HEAD_DIM), BF16),
                   jax.ShapeDtypeStruct((2, bsz, G, HEAD_DIM, nseg), BF16)],
        scratch_shapes=[pltpu.VMEM((nseg + 8, HEAD_DIM), F32)],
        compiler_params=_cparams(("parallel", "parallel", "parallel")),
        name="nsa_compress",
    )(seg, pea, peb, w1a, w1b, w2)


def _cmp_select_kernel(q_ref, kc_ref, vct_ref, ovt_ref, o_ref, selbt_ref, w_ref, s_ref, p_ref, *, qc, n_top):
    ncmp = kc_ref.shape[2]
    lanes = HEADS_PER_GROUP * qc
    t0 = pl.program_id(1) * qc
    t_row = t0 + lax.broadcasted_iota(jnp.int32, (1, qc), 1)
    cmp_i = lax.broadcasted_iota(jnp.int32, (ncmp, 1), 0)
    vis = (cmp_i * CMP_STRIDE + (CMP_BLOCK - 1)) <= t_row
    any_vis = (t_row >= CMP_BLOCK - 1).astype(F32)
    cur = jnp.right_shift(t_row, 6)
    blk = lax.broadcasted_iota(jnp.int32, (MAX_SEL_BLOCKS, qc), 0)
    valid = blk <= cur
    forced = (blk == 0) | (valid & (blk > cur - SEL_LOCAL))
    qt = (q_ref[0] * SCALE).T.astype(BF16)
    for h in range(A_HEADS):
        g, hh = divmod(h, HEADS_PER_GROUP)
        w_ref[g, :, hh * qc:(hh + 1) * qc] = qt[h * HEAD_DIM:(h + 1) * HEAD_DIM, :]
    for g in range(A_KV_HEADS):
        s_ref[g] = _dot(kc_ref[0, g], w_ref[g])
    ovt = ovt_ref[...]
    for g in range(A_KV_HEADS):
        psum = jnp.zeros((ncmp, qc), F32)
        for hh in range(HEADS_PER_GROUP):
            cols = slice(hh * qc, (hh + 1) * qc)
            s = jnp.where(vis, s_ref[g, :, cols], NEG_INF)
            e = jnp.exp(s - jnp.max(s, axis=0, keepdims=True))
            p = e * (any_vis / jnp.sum(e, axis=0, keepdims=True))
            p_ref[g, :, cols] = p.astype(BF16)
            psum = psum + p
        ot = _dot(vct_ref[0, g], p_ref[g])
        o_ref[0, :, g * GROUP_WIDTH:(g + 1) * GROUP_WIDTH] = _heads_to_rows(ot, qc)
        p1 = psum.astype(BF16)
        r1 = psum - p1.astype(F32)
        p2 = r1.astype(BF16)
        p3 = (r1 - p2.astype(F32)).astype(BF16)
        imp = _dot(ovt, p1) + _dot(ovt, p2) + _dot(ovt, p3)
        val = jnp.where(forced, -2.0, jnp.where(valid, imp, -1.0))
        sel = forced
        for _ in range(n_top - (SEL_LOCAL + 1)):
            mx = jnp.max(val, axis=0, keepdims=True)
            first = jnp.min(jnp.where(val == mx, blk, MAX_SEL_BLOCKS), axis=0, keepdims=True)
            hit = blk == first
            sel = sel | hit
            val = jnp.where(hit, -2.0, val)
        chosen = sel & (blk < cur)
        selbt_ref[0, g] = jnp.where(chosen, 0.0, NEG_INF).astype(BF16)


def cmp_select(proj3, kcmp, vcmp_t, overlap_t, *, qc, n_top):
    bsz, T, _ = proj3.shape
    ncmp = kcmp.shape[2]
    G = A_KV_HEADS
    assert n_top > SEL_LOCAL + 1
    return pl.pallas_call(
        functools.partial(_cmp_select_kernel, qc=qc, n_top=n_top),
        grid=(bsz, T // qc),
        in_specs=[
            pl.BlockSpec((1, qc, A_WIDTH), lambda b, c: (b, c, OFF_A_Q // A_WIDTH)),
            pl.BlockSpec((1, G, ncmp, HEAD_DIM), lambda b, c: (b, 0, 0, 0)),
            pl.BlockSpec((1, G, HEAD_DIM, ncmp), lambda b, c: (b, 0, 0, 0)),
            pl.BlockSpec((MAX_SEL_BLOCKS, ncmp), lambda b, c: (0, 0)),
        ],
        out_specs=[
            pl.BlockSpec((1, qc, A_WIDTH), lambda b, c: (b, c, 0)),
            pl.BlockSpec((1, G, MAX_SEL_BLOCKS, qc), lambda b, c: (b, 0, 0, c)),
        ],
        out_shape=[jax.ShapeDtypeStruct((bsz, T, A_WIDTH), F32),
                   jax.ShapeDtypeStruct((bsz, G, MAX_SEL_BLOCKS, T), BF16)],
        scratch_shapes=[
            pltpu.VMEM((G, HEAD_DIM, HEADS_PER_GROUP * qc), BF16),
            pltpu.VMEM((G, ncmp, HEADS_PER_GROUP * qc), F32),
            pltpu.VMEM((G, ncmp, HEADS_PER_GROUP * qc), BF16),
        ],
        compiler_params=_cparams(("parallel", "parallel")),
        name="nsa_cmp_select",
    )(proj3, kcmp, vcmp_t, overlap_t)


def _slc_kernel(q_ref, selbt_ref, kaug_ref, vaugt_ref, arow_ref, slope_ref, dbias_ref, o_ref,
                qaug_ref, m_ref, acc_ref, *s_refs, qc, tk):
    t0 = pl.multiple_of(pl.program_id(2) * qc, qc)
    selbt = selbt_ref[0, 0]
    qt = (q_ref[0] * SCALE).T.astype(BF16)
    for hh in range(HEADS_PER_GROUP):
        cols = slice(hh * qc, (hh + 1) * qc)
        qaug_ref[0:LANES, cols] = selbt
        qaug_ref[LANES:LANES + HEAD_DIM, cols] = qt[hh * HEAD_DIM:(hh + 1) * HEAD_DIM, :]
    qaug_ref[LANES + HEAD_DIM:AUG_DEPTH, :] = arow_ref[0]
    m_ref[...] = jnp.full(m_ref.shape, NEG_INF, F32)
    acc_ref[...] = jnp.zeros(acc_ref.shape, F32)

    bufs = (s_refs[:HEADS_PER_GROUP], s_refs[HEADS_PER_GROUP:])

    def scores(kt, hh, dst):
        s0 = pl.multiple_of(kt * tk, tk)
        dst[hh][...] = _dot(kaug_ref[0, 0, pl.ds(s0, tk), :], qaug_ref[:, hh * qc:(hh + 1) * qc])

    def diag_scores(hh, dst):
        ka = kaug_ref[0, 0, pl.ds(t0, qc), LANES:AUG_DEPTH]
        st = _dot(ka, qaug_ref[LANES:AUG_DEPTH, hh * qc:(hh + 1) * qc])
        dst[hh][0:qc, :] = st + dbias_ref[...]

    def softmax_pv(hh, src, rows, vt, off):
        c = slope_ref[0, hh] * off
        m_old = m_ref[hh]
        m_new = jnp.maximum(m_old, jnp.max(src[hh][0:rows, :], axis=0, keepdims=True) + c)
        alpha = jnp.exp(m_old - m_new)
        p = jnp.exp(src[hh][0:rows, :] - (m_new - c)).astype(BF16)
        acc_ref[hh] = alpha * acc_ref[hh] + _dot(vt, p)
        m_ref[hh] = m_new

    def step(kt, src, next_scores):
        s0 = pl.multiple_of(kt * tk, tk)
        vt = vaugt_ref[0, 0, :, pl.ds(s0, tk)]
        off = (s0 - t0).astype(F32)
        for hh in range(HEADS_PER_GROUP):
            next_scores(hh)
            softmax_pv(hh, src, tk, vt, off)

    def main_step(kt, src, dst):
        step(kt, src, lambda hh: scores(kt + 1, hh, dst))

    n_full = lax.div(t0, tk)
    odd = lax.rem(n_full, 2)

    @pl.when(odd == 0)
    def _():
        for hh in range(HEADS_PER_GROUP):
            scores(0, hh, bufs[0])

    @pl.when(odd == 1)
    def _():
        for hh in range(HEADS_PER_GROUP):
            scores(0, hh, bufs[1])
        main_step(0, bufs[1], bufs[0])

    def body(i, carry):
        kt = odd + 2 * i
        main_step(kt, bufs[0], bufs[1])
        main_step(kt + 1, bufs[1], bufs[0])
        return carry

    lax.fori_loop(0, lax.div(n_full, 2), body, 0)
    step(n_full, bufs[0], lambda hh: diag_scores(hh, bufs[1]))
    base = pl.multiple_of(n_full * tk, tk)
    vt_d = vaugt_ref[0, 0, :, pl.ds(t0, qc)]
    for hh in range(HEADS_PER_GROUP):
        softmax_pv(hh, bufs[1], qc, vt_d, (base - t0).astype(F32))
    ot = jnp.concatenate([acc_ref[hh, 0:HEAD_DIM, :] / acc_ref[hh, HEAD_DIM:HEAD_DIM + 1, :]
                          for hh in range(HEADS_PER_GROUP)], axis=0)
    o_ref[0] = ot.T


def slc_attention(proj3, selbt, kaug, vaugt, arows, slopes, *, qc, tk):
    bsz, T, _ = proj3.shape
    G = A_KV_HEADS
    assert tk % qc == 0 and qc % SEL_BLOCK == 0
    ki = np.arange(qc)[:, None]
    qi = np.arange(qc)[None, :]
    dbias = jnp.asarray(np.where((ki // SEL_BLOCK == qi // SEL_BLOCK) & (ki <= qi), 0.0, NEG_INF), dtype=F32)
    return pl.pallas_call(
        functools.partial(_slc_kernel, qc=qc, tk=tk),
        grid=(bsz, G, T // qc),
        in_specs=[
            pl.BlockSpec((1, qc, GROUP_WIDTH), lambda b, g, c: (b, c, OFF_A_Q // GROUP_WIDTH + g)),
            pl.BlockSpec((1, 1, LANES, qc), lambda b, g, c: (b, g, 0, c)),
            pl.BlockSpec((1, 1, T, AUG_DEPTH), lambda b, g, c: (b, g, 0, 0)),
            pl.BlockSpec((1, 1, LANES, T), lambda b, g, c: (b, g, 0, 0)),
            pl.BlockSpec((1, AUG_DEPTH - LANES - HEAD_DIM, HEADS_PER_GROUP * qc), lambda b, g, c: (g, 0, 0)),
            pl.BlockSpec((1, HEADS_PER_GROUP, 1, qc), lambda b, g, c: (g, 0, 0, 0)),
            pl.BlockSpec((qc, qc), lambda b, g, c: (0, 0)),
        ],
        out_specs=pl.BlockSpec((1, qc, GROUP_WIDTH), lambda b, g, c: (b, c, g)),
        out_shape=jax.ShapeDtypeStruct((bsz, T, A_WIDTH), F32),
        scratch_shapes=[
            pltpu.VMEM((AUG_DEPTH, HEADS_PER_GROUP * qc), BF16),
            pltpu.VMEM((HEADS_PER_GROUP, 1, qc), F32),
            pltpu.VMEM((HEADS_PER_GROUP, LANES, qc), F32),
        ] + [pltpu.VMEM((tk, qc), F32)] * (2 * HEADS_PER_GROUP),
        compiler_params=_cparams(("parallel", "parallel", "arbitrary")),
        name="nsa_slc_attention",
    )(proj3, selbt, kaug, vaugt, arows, slopes, dbias)


def _band_kernel(*refs, qc, window, has_sinks, mix):
    q_refs, kv_ref, bias_ref = refs[0:2], refs[2], refs[3]
    pos = 4
    sink_ref = None
    if has_sinks:
        sink_ref = refs[pos]
        pos += 1
    z_refs = refs[pos:pos + 2]
    pos += 2
    if mix:
        ocmp_ref, oslc_ref, gate_ref = refs[pos:pos + 3]
        pos += 3
    o_ref, w_ref, kwin_ref, vt_ref, p_ref = refs[pos:pos + 5]
    s_refs = refs[pos + 5:pos + 7]
    G = len(q_refs)
    span = qc + window
    lanes = HEADS_PER_GROUP * qc
    n_pad_chunks = window // qc
    c = pl.program_id(1)
    ones_row = (lax.broadcasted_iota(jnp.int32, (LANES - HEAD_DIM, span), 0) == 0).astype(BF16)

    def assemble(kv):
        kwin_ref[...] = kv[:, 0:LANES].astype(BF16)
        vt = kv[:, LANES:2 * LANES].T.astype(BF16)
        for g in range(G):
            vt_ref[g, 0:HEAD_DIM, :] = vt[g * HEAD_DIM:(g + 1) * HEAD_DIM, :]
            vt_ref[g, HEAD_DIM:LANES, :] = ones_row

    @pl.when(c >= n_pad_chunks)
    def _():
        start = pl.multiple_of(c * qc - window, qc)
        assemble(kv_ref[0, pl.ds(start, span), :])

    for j in range(n_pad_chunks):
        @pl.when(c == j)
        def _(j=j):
            n_pad = window - j * qc
            assemble(jnp.concatenate([jnp.zeros((n_pad, 2 * LANES), F32), kv_ref[0, 0:span - n_pad, :]], axis=0))

    zero = jnp.zeros((HEAD_DIM, qc), BF16)
    for g in range(G):
        qt = (q_refs[g][0] * SCALE).T.astype(BF16)
        for hh in range(HEADS_PER_GROUP):
            cols = slice(hh * qc, (hh + 1) * qc)
            w_ref[g, g * HEAD_DIM:(g + 1) * HEAD_DIM, cols] = qt[hh * HEAD_DIM:(hh + 1) * HEAD_DIM, :]
            w_ref[g, (1 - g) * HEAD_DIM:(2 - g) * HEAD_DIM, cols] = zero
    pieces = [slice(r, r + LANES) for r in range(0, span, LANES)]
    for g in range(G):
        for rows in pieces:
            s_refs[g][rows, :] = _dot(kwin_ref[rows, :], w_ref[g])
    if mix:
        gate = _sigmoid(gate_ref[0])
    for g in range(G):
        mx8 = None
        for rows in pieces:
            t = s_refs[g][rows, :] + bias_ref[0, rows, g * lanes:(g + 1) * lanes]
            s_refs[g][rows, :] = t
            m8 = jnp.max(t.reshape(LANES // 8, 8, lanes), axis=0)
            mx8 = m8 if mx8 is None else jnp.maximum(mx8, m8)
        mx = jnp.max(mx8, axis=0, keepdims=True)
        if has_sinks:
            sk = sink_ref[:, g * lanes:(g + 1) * lanes]
            mx = jnp.maximum(mx, sk)
        for rows in pieces:
            p_ref[rows, :] = jnp.exp(s_refs[g][rows, :] - mx).astype(BF16)
        ot = _dot(vt_ref[g], p_ref[...])
        den = ot[HEAD_DIM:HEAD_DIM + 1, :]
        if has_sinks:
            den = den + jnp.exp(sk - mx)
        o = _heads_to_rows(ot[0:HEAD_DIM, :] / den, qc)
        z = z_refs[g][0]
        for hh in range(HEADS_PER_GROUP):
            h = g * HEADS_PER_GROUP + hh
            loc = slice(hh * HEAD_DIM, (hh + 1) * HEAD_DIM)
            glob = slice(h * HEAD_DIM, (h + 1) * HEAD_DIM)
            oh = o[:, loc]
            if mix:
                oh = (gate[:, 3 * h:3 * h + 1] * ocmp_ref[0, :, glob]
                      + gate[:, 3 * h + 1:3 * h + 2] * oslc_ref[0, :, glob]
                      + gate[:, 3 * h + 2:3 * h + 3] * oh)
            o_ref[0, :, glob] = (oh * _silu(z[:, loc])).astype(o_ref.dtype)


def _band_bias(qc, window, n_heads):
    span = qc + window
    n_var = window // qc + 1
    slopes = jnp.asarray(_alibi_np(n_heads))
    row = lax.broadcasted_iota(jnp.int32, (span, qc), 0)
    dist = lax.broadcasted_iota(jnp.int32, (span, qc), 1) + window - row
    band = (dist >= 0) & (dist < window)
    term = -(slopes[:, None, None] * dist.astype(F32)[None])
    n_pad = window - jnp.arange(n_var, dtype=jnp.int32)[:, None, None, None] * qc
    ok = band[None, None] & (row[None, None] >= n_pad)
    bias = jnp.where(ok, term[None], NEG_INF)
    return bias.transpose(0, 2, 1, 3).reshape(n_var, span, n_heads * qc)


def band_attention(proj3, sinks, mix_in, *, q_off, kv_off, z_off, qc, window):
    bsz, T, _ = proj3.shape
    G = 2
    n_heads = G * HEADS_PER_GROUP
    span = qc + window
    assert window % qc == 0 and T >= span
    has_sinks = sinks is not None
    mix = mix_in is not None
    bias = _band_bias(qc, window, n_heads)
    n_var = bias.shape[0]
    grp = lambda off: [pl.BlockSpec((1, qc, GROUP_WIDTH), lambda b, c, g=g: (b, c, off // GROUP_WIDTH + g))
                       for g in range(G)]
    in_specs = grp(q_off) + [
        pl.BlockSpec((1, T, 2 * LANES), lambda b, c: (b, 0, kv_off // (2 * LANES))),
        pl.BlockSpec((1, span, n_heads * qc), lambda b, c: (jnp.minimum(c, n_var - 1), 0, 0)),
    ]
    args = [proj3, proj3, proj3, bias]
    if has_sinks:
        in_specs.append(pl.BlockSpec((1, n_heads * qc), lambda b, c: (0, 0)))
        args.append(jnp.repeat(sinks.astype(F32), qc).reshape(1, n_heads * qc))
    in_specs += grp(z_off)
    args += [proj3, proj3]
    if mix:
        full = pl.BlockSpec((1, qc, A_WIDTH), lambda b, c: (b, c, 0))
        in_specs += [full, full, pl.BlockSpec((1, qc, LANES), lambda b, c: (b, c, OFF_A_GATE // LANES))]
        args += [mix_in[0], mix_in[1], proj3]
    return pl.pallas_call(
        functools.partial(_band_kernel, qc=qc, window=window, has_sinks=has_sinks, mix=mix),
        grid=(bsz, T // qc),
        in_specs=in_specs,
        out_specs=pl.BlockSpec((1, qc, G * GROUP_WIDTH), lambda b, c: (b, c, 0)),
        out_shape=jax.ShapeDtypeStruct((bsz, T, G * GROUP_WIDTH), BF16),
        scratch_shapes=[
            pltpu.VMEM((G, LANES, HEADS_PER_GROUP * qc), BF16),
            pltpu.VMEM((span, LANES), BF16),
            pltpu.VMEM((G, LANES, span), BF16),
            pltpu.VMEM((span, HEADS_PER_GROUP * qc), BF16),
        ] + [pltpu.VMEM((span, HEADS_PER_GROUP * qc), F32)] * G,
        compiler_params=_cparams(("parallel", "parallel")),
        name="band_attention_w%d" % window,
    )(*args)


def _log_sigmoid(x):
    return jnp.minimum(x, 0.0) - jnp.log(1.0 + jnp.exp(-jnp.abs(x)))


def _mlstm_kernel(q_ref, k_ref, v_ref, og_ref, z_ref, ic_ref, fc_ref, ir_ref, fr_ref, ib_ref, fb_ref,
                  cwq_ref, cwk_ref, cbq_ref, cbk_ref, g_ref, y_ref,
                  xq_ref, xk_ref, c_ref, n_ref, m_ref, *, L):
    @pl.when(pl.program_id(2) == 0)
    def _():
        xq_ref[0:8, :] = jnp.zeros((8, B_HEAD_DIM), F32)
        xk_ref[0:8, :] = jnp.zeros((8, B_HEAD_DIM), F32)
        c_ref[...] = jnp.zeros(c_ref.shape, F32)
        n_ref[...] = jnp.zeros(n_ref.shape, F32)
        m_ref[...] = jnp.zeros(m_ref.shape, F32)

    def conv_silu(x_ref, hist_ref, w_ref, b_ref):
        hist_ref[8:8 + L, :] = x_ref[0]
        y = b_ref[...]
        for i in range(CONV_WIDTH):
            y = y + w_ref[i:i + 1, :] * hist_ref[5 + i:5 + i + L, :]
        hist_ref[0:8, :] = hist_ref[L:L + 8, :]
        return _silu(y)

    q = conv_silu(q_ref, xq_ref, cwq_ref, cbq_ref)
    k = conv_silu(k_ref, xk_ref, cwk_ref, cbk_ref) * (B_HEAD_DIM ** -0.5)
    v = v_ref[0]
    i_col = ic_ref[0, 0] + ib_ref[0]
    i_row = ir_ref[0, 0] + ib_ref[0]
    lf_col = _log_sigmoid(fc_ref[0, 0] + fb_ref[0])
    lf_row = _log_sigmoid(fr_ref[0, 0] + fb_ref[0])
    r_i = lax.broadcasted_iota(jnp.int32, (L, L), 0)
    c_i = lax.broadcasted_iota(jnp.int32, (L, L), 1)
    causal = c_i <= r_i
    b_col = jnp.sum(jnp.where(causal, lf_row, 0.0), axis=1, keepdims=True)
    b_row = jnp.sum(jnp.where(r_i <= c_i, lf_col, 0.0), axis=0, keepdims=True)
    b_last = jnp.sum(lf_row, axis=1, keepdims=True)
    m_prev = m_ref[...]
    log_d = jnp.where(causal, b_col - b_row + i_row, NEG_INF)
    log_inter = b_col + m_prev
    m_t = jnp.maximum(log_inter, jnp.max(log_d, axis=1, keepdims=True))
    w_intra = jnp.exp(log_d - m_t)
    w_inter = jnp.exp(log_inter - m_t)
    qb = q.astype(BF16)
    vb = v.astype(BF16)
    qk = _dot_nt(qb, k.astype(BF16)) * w_intra
    num = w_inter * _dot(qb, c_ref[...].astype(BF16)) + _dot(qk.astype(BF16), vb)
    den = (w_inter * jnp.sum(q * n_ref[...], axis=1, keepdims=True)
           + jnp.sum(qk, axis=1, keepdims=True))
    h = num / jnp.maximum(jnp.abs(den), jnp.exp(-m_t))
    log_g_row = b_last - b_row + i_row
    m_new = jnp.maximum(b_last + m_prev, jnp.max(log_g_row, axis=1, keepdims=True))
    w_g = jnp.exp(b_last - b_col + i_col - m_new)
    decay = jnp.exp(b_last + m_prev - m_new)
    kw = k * w_g
    c_ref[...] = decay * c_ref[...] + _dot_tn(kw.astype(BF16), vb)
    n_ref[...] = decay * n_ref[...] + jnp.sum(kw, axis=0, keepdims=True)
    m_ref[...] = m_new
    hb = _sigmoid(og_ref[0]) * h
    ms = jnp.mean(hb * hb, axis=-1, keepdims=True)
    hb = hb * lax.rsqrt(ms + EPS) * g_ref[...]
    y_ref[0] = (hb * _silu(z_ref[0])).astype(y_ref.dtype)


def mlstm(proj3, if_col, if_row, i_bias, f_bias, conv_w, conv_b, norm_g, *, L):
    bsz, T, _ = proj3.shape
    H = B_HEADS
    D = B_HEAD_DIM

    def col(off, k_half=False):
        base = off // D + (H if k_half else 0)
        return pl.BlockSpec((1, L, D), lambda b, h, c: (b, c, base + h))

    def par(rows, k_half=False):
        base = H if k_half else 0
        return pl.BlockSpec((rows, D), lambda b, h, c: (0, base + h))

    in_specs = [
        col(OFF_B_QK), col(OFF_B_QK, True), col(OFF_B_V), col(OFF_B_O), col(OFF_B_Z),
        pl.BlockSpec((1, 1, L, 1), lambda b, h, c: (b, h, c, 0)),
        pl.BlockSpec((1, 1, L, 1), lambda b, h, c: (b, H + h, c, 0)),
        pl.BlockSpec((1, 1, 1, L), lambda b, h, c: (b, h, 0, c)),
        pl.BlockSpec((1, 1, 1, L), lambda b, h, c: (b, H + h, 0, c)),
        pl.BlockSpec((1, 1, 1), lambda b, h, c: (h, 0, 0)),
        pl.BlockSpec((1, 1, 1), lambda b, h, c: (h, 0, 0)),
        par(CONV_WIDTH), par(CONV_WIDTH, True), par(1), par(1, True),
        pl.BlockSpec((1, D), lambda b, h, c: (0, h)),
    ]
    return pl.pallas_call(
        functools.partial(_mlstm_kernel, L=L),
        grid=(bsz, H, T // L),
        in_specs=in_specs,
        out_specs=pl.BlockSpec((1, L, D), lambda b, h, c: (b, c, h)),
        out_shape=jax.ShapeDtypeStruct((bsz, T, B_WIDTH), BF16),
        scratch_shapes=[
            pltpu.VMEM((L + 8, D), F32), pltpu.VMEM((L + 8, D), F32),
            pltpu.VMEM((D, D), F32), pltpu.VMEM((1, D), F32), pltpu.VMEM((1, 1), F32),
        ],
        compiler_params=_cparams(("parallel", "parallel", "arbitrary")),
        name="mlstm_chunkwise",
    )(proj3, proj3, proj3, proj3, proj3, if_col, if_col, if_row, if_row, i_bias, f_bias,
      conv_w, conv_w, conv_b, conv_b, norm_g)


def _alibi_np(n_heads):
    s = (2.0 ** (-8.0 * np.arange(1, n_heads + 1, dtype=np.float64) / n_heads)).astype(np.float32)
    for v in (s, SEL_BLOCK * s):
        assert np.array_equal(v.astype(jnp.bfloat16).astype(np.float32), v), "slopes not bf16-exact"
    return s


def _lane_rep(v, groups, width=LANES):
    return jnp.broadcast_to(v.astype(F32).reshape(groups, -1, 1, 1), (groups, v.shape[0] // groups, 1, width))


def _kv_heads(a, bsz, T, G):
    return a.reshape(bsz, T, G, HEAD_DIM).transpose(0, 2, 1, 3)


def _mixers(x2d, bsz, T, p, cfg):
    n, d = x2d.shape
    G = A_KV_HEADS
    w_packed = repack_w_in(p["w_in"], tr=cfg["tr_repack"])
    proj = norm_matmul(x2d, p["norm_g"].reshape(1, d), w_packed, tm=cfg["tm_in"], tn=cfg["tn_in"])
    proj3 = proj.reshape(bsz, T, PROJ_COLS)
    a_kc, a_vc, a_ks, a_vs = [proj3[:, :, OFF_A_KV + i * 128:OFF_A_KV + (i + 1) * 128] for i in range(4)]

    nseg = T // CMP_STRIDE
    seg = jnp.stack([a_kc, a_vc]).reshape(2, bsz, nseg, CMP_STRIDE, G, HEAD_DIM)
    seg = seg.transpose(0, 1, 4, 2, 3, 5).reshape(2, bsz, G, nseg, CMP_STRIDE * HEAD_DIM)
    pe = jnp.stack([p["cmp_pe_k"], p["cmp_pe_v"]])
    w1 = jnp.stack([p["cmp_w1_k"], p["cmp_w1_v"]]).astype(BF16)
    w2 = jnp.stack([p["cmp_w2_k"], p["cmp_w2_v"]]).astype(BF16)
    half = CMP_STRIDE * HEAD_DIM
    kv_cmp, kv_cmp_t = nsa_compress(seg, pe[:, :CMP_STRIDE].reshape(2, 1, half),
                                    pe[:, CMP_STRIDE:].reshape(2, 1, half), w1[:, :half], w1[:, half:], w2)

    n_sel = T // SEL_BLOCK
    assert n_sel <= MAX_SEL_BLOCKS
    ci = np.arange(nseg)[None, :] * CMP_STRIDE
    sj = np.arange(MAX_SEL_BLOCKS)[:, None] * SEL_BLOCK
    overlap_t = jnp.asarray(((ci < sj + SEL_BLOCK) & (ci + CMP_BLOCK > sj)).astype(np.float32), dtype=BF16)
    o_cmp, selbt = cmp_select(proj3, kv_cmp[0], kv_cmp_t[1], overlap_t, qc=cfg["qc_cmp"], n_top=min(SEL_TOPK, n_sel))

    qc_s, tk_s = cfg["qc_slc"], cfg["tk_slc"]
    pos = np.arange(T)
    key_feat = np.zeros((T, AUG_DEPTH - LANES - HEAD_DIM), np.float32)
    key_feat[:, 0] = pos % SEL_BLOCK
    key_feat[:, 1] = (pos // SEL_BLOCK) % (tk_s // SEL_BLOCK)
    onehot = (pos[:, None] // SEL_BLOCK == np.arange(LANES)[None, :]).astype(np.float32)
    bcast = lambda a: jnp.broadcast_to(jnp.asarray(a, dtype=BF16), (bsz, G) + a.shape)
    kaug = jnp.concatenate([bcast(onehot), _kv_heads(a_ks, bsz, T, G).astype(BF16), bcast(key_feat)], axis=-1)
    v_t = a_vs.reshape(bsz, T, G, HEAD_DIM).transpose(0, 2, 3, 1).astype(BF16)
    vaugt = jnp.concatenate([v_t, jnp.ones((bsz, G, 1, T), BF16),
                             jnp.zeros((bsz, G, LANES - HEAD_DIM - 1, T), BF16)], axis=2)
    sl_np = _alibi_np(A_HEADS)
    arows = np.zeros((A_HEADS, AUG_DEPTH - LANES - HEAD_DIM, qc_s), np.float32)
    arows[:, 0, :] = sl_np[:, None]
    arows[:, 1, :] = SEL_BLOCK * sl_np[:, None]
    arows = arows.reshape(G, HEADS_PER_GROUP, -1, qc_s).transpose(0, 2, 1, 3).reshape(G, -1, HEADS_PER_GROUP * qc_s)
    o_slc = slc_attention(proj3, selbt, kaug, vaugt, jnp.asarray(arows, dtype=BF16),
                          _lane_rep(jnp.asarray(sl_np), G, qc_s), qc=qc_s, tk=tk_s)

    y_a = band_attention(proj3, None, (o_cmp, o_slc), q_off=OFF_A_Q, kv_off=OFF_A_KV + 4 * LANES, z_off=OFF_A_Z,
                         qc=cfg["qc_win"], window=NSA_WINDOW)

    if_t = proj3[:, :, OFF_B_IF:OFF_B_IF + 2 * B_HEADS].transpose(0, 2, 1)
    y_b = mlstm(proj3, if_t[..., None], if_t[:, :, None, :],
                p["mlstm_i_bias"].reshape(B_HEADS, 1, 1), p["mlstm_f_bias"].reshape(B_HEADS, 1, 1),
                p["mlstm_conv_w"], p["mlstm_conv_b"].reshape(1, -1), p["mlstm_norm_g"].reshape(1, -1),
                L=cfg["mlstm_chunk"])

    y_c = band_attention(proj3, p["swa_sinks"], None, q_off=OFF_C_Q, kv_off=OFF_C_KV, z_off=OFF_C_Z,
                         qc=cfg["qc_win"], window=SWA_WINDOW)
    return y_a.reshape(n, A_WIDTH), y_b.reshape(n, B_WIDTH), y_c.reshape(n, C_WIDTH), (o_cmp, o_slc, selbt)


def _layer(x2d, bsz, T, p, final_g, cfg):
    n, d = x2d.shape
    y_a, y_b, y_c, _ = _mixers(x2d, bsz, T, p, cfg)
    w_out = p["w_out"].astype(BF16)
    g = final_g if final_g is not None else p["norm_g"]
    return out_proj(y_a, y_b, y_c, x2d,
                    w_out[:A_WIDTH], w_out[A_WIDTH:A_WIDTH + B_WIDTH], w_out[A_WIDTH + B_WIDTH:],
                    g.reshape(1, d), tm=cfg["tm_out"], final_norm=final_g is not None)


def _config(T):
    return dict(tr_repack=256, tm_in=min(1024, T), tn_in=768, tm_out=min(512, T),
                qc_cmp=128, qc_slc=256, tk_slc=512, qc_win=128, mlstm_chunk=min(256, T))


def kernel(x, norm_g, w_in, w_out, cmp_pe_k, cmp_w1_k, cmp_w2_k, cmp_pe_v, cmp_w1_v, cmp_w2_v, mlstm_conv_w, mlstm_conv_b, mlstm_i_bias, mlstm_f_bias, mlstm_norm_g, swa_sinks, final_norm_g):
    bsz, T, d = x.shape
    depth = w_in.shape[0]
    cfg = _config(T)
    x2d = x.reshape(bsz * T, d)
    for l in range(depth):
        p = dict(norm_g=norm_g[l], w_in=w_in[l], w_out=w_out[l],
                 cmp_pe_k=cmp_pe_k[l], cmp_w1_k=cmp_w1_k[l], cmp_w2_k=cmp_w2_k[l],
                 cmp_pe_v=cmp_pe_v[l], cmp_w1_v=cmp_w1_v[l], cmp_w2_v=cmp_w2_v[l],
                 mlstm_conv_w=mlstm_conv_w[l], mlstm_conv_b=mlstm_conv_b[l],
                 mlstm_i_bias=mlstm_i_bias[l], mlstm_f_bias=mlstm_f_bias[l],
                 mlstm_norm_g=mlstm_norm_g[l], swa_sinks=swa_sinks[l])
        x2d = _layer(x2d, bsz, T, p, final_norm_g if l == depth - 1 else None, cfg)
    return x2d.reshape(bsz, T, d)
```

```python
import functools

import numpy as np
import jax
import jax.numpy as jnp
from jax import lax
from jax.experimental import pallas as pl
from jax.experimental.pallas import tpu as pltpu

F32 = jnp.float32
BF16 = jnp.bfloat16

HEAD_DIM = 64
A_HEADS = 8
A_KV_HEADS = 2
HEADS_PER_GROUP = A_HEADS // A_KV_HEADS
GROUP_WIDTH = HEADS_PER_GROUP * HEAD_DIM
A_WIDTH = A_HEADS * HEAD_DIM
C_HEADS = 8
C_KV_HEADS = 2
C_WIDTH = C_HEADS * HEAD_DIM
B_HEADS = 4
B_HEAD_DIM = 256
B_WIDTH = B_HEADS * B_HEAD_DIM
CMP_BLOCK = 32
CMP_STRIDE = 16
SEL_BLOCK = 64
SEL_TOPK = 16
SEL_LOCAL = 2
NSA_WINDOW = 512
SWA_WINDOW = 128
CONV_WIDTH = 4
FORCE_BONUS = 1.0e4
NEG_INF = -1.0e30
EPS = 1.0e-6
SCALE = HEAD_DIM ** -0.5
LANES = 128
MAX_SEL_BLOCKS = LANES
AUG_DEPTH = 256

IN_SPLITS = (A_WIDTH, 128, 128, 128, 128, 128, 128, A_HEADS * 3, A_WIDTH,
             2 * B_WIDTH, B_WIDTH, B_HEADS, B_HEADS, B_WIDTH, B_WIDTH,
             C_WIDTH, 128, 128, C_WIDTH)
IN_COLS = sum(IN_SPLITS)
_SRC = np.concatenate([[0], np.cumsum(IN_SPLITS)]).astype(int)
SRC_GATE, SRC_A_Z, SRC_B_IF, SRC_B_O = int(_SRC[7]), int(_SRC[8]), int(_SRC[11]), int(_SRC[13])
N_GATE = A_HEADS * 3
N_IF = 2 * B_HEADS
OFF_A_Q = 0
OFF_A_KV = 512
OFF_A_Z = 1280
OFF_B_QK = 1792
OFF_B_V = 3840
OFF_B_O = 4864
OFF_B_Z = 5888
OFF_C_Q = 6912
OFF_C_KV = 7424
OFF_C_Z = 7680
OFF_A_GATE = 8192
OFF_B_IF = 8320
PROJ_COLS = 8448
W_RUNS = ((0, OFF_A_Q, SRC_GATE), (SRC_A_Z, OFF_A_Z, SRC_B_IF - SRC_A_Z), (SRC_B_O, OFF_B_O, IN_COLS - SRC_B_O))
assert OFF_A_Z + (SRC_B_IF - SRC_A_Z) == OFF_B_O and OFF_B_O + (IN_COLS - SRC_B_O) == OFF_A_GATE

VMEM_LIMIT = 48 * 1024 * 1024


def _cparams(sem):
    return pltpu.CompilerParams(dimension_semantics=sem, vmem_limit_bytes=VMEM_LIMIT)


def _dot(a, b):
    return jnp.dot(a, b, preferred_element_type=F32)


def _dot_nt(a, b):
    return lax.dot_general(a, b, (((1,), (1,)), ((), ())), preferred_element_type=F32)


def _dot_tn(a, b):
    return lax.dot_general(a, b, (((0,), (0,)), ((), ())), preferred_element_type=F32)


def _sigmoid(x):
    return 1.0 / (1.0 + jnp.exp(-x))


def _silu(x):
    return x * _sigmoid(x)


def _heads_to_rows(ot, qc):
    return jnp.concatenate([ot[:, hh * qc:(hh + 1) * qc] for hh in range(HEADS_PER_GROUP)], axis=0).T


def _repack_kernel(w_ref, o_ref):
    for src, dst, width in W_RUNS:
        o_ref[:, dst:dst + width] = w_ref[0, :, src:src + width].astype(BF16)
    o_ref[:, OFF_A_GATE:PROJ_COLS] = jnp.zeros((o_ref.shape[0], PROJ_COLS - OFF_A_GATE), BF16)
    o_ref[:, OFF_A_GATE:OFF_A_GATE + N_GATE] = w_ref[0, :, SRC_GATE:SRC_GATE + N_GATE].astype(BF16)
    o_ref[:, OFF_B_IF:OFF_B_IF + N_IF] = w_ref[0, :, SRC_B_IF:SRC_B_IF + N_IF].astype(BF16)


def repack_w_in(w_in, layer, *, tr):
    d = w_in.shape[1]
    return pl.pallas_call(
        _repack_kernel,
        grid=(d // tr,),
        in_specs=[pl.BlockSpec((1, tr, IN_COLS), lambda i: (layer, i, 0))],
        out_specs=pl.BlockSpec((tr, PROJ_COLS), lambda i: (i, 0)),
        out_shape=jax.ShapeDtypeStruct((d, PROJ_COLS), BF16),
        compiler_params=_cparams(("parallel",)),
        name="repack_w_in",
    )(w_in)


def _norm_matmul_kernel(x_ref, g_ref, w_ref, o_ref, h_ref):
    @pl.when(pl.program_id(1) == 0)
    def _():
        x = x_ref[...]
        ms = jnp.mean(x * x, axis=-1, keepdims=True)
        h_ref[...] = (x * lax.rsqrt(ms + EPS) * g_ref[...]).astype(BF16)

    o_ref[...] = _dot(h_ref[...], w_ref[...])


def norm_matmul(x, g, w, *, tm, tn):
    n, d = x.shape
    cols = w.shape[1]
    return pl.pallas_call(
        _norm_matmul_kernel,
        grid=(n // tm, cols // tn),
        in_specs=[
            pl.BlockSpec((tm, d), lambda i, j: (i, 0)),
            pl.BlockSpec((1, d), lambda i, j: (0, 0)),
            pl.BlockSpec((d, tn), lambda i, j: (0, j)),
        ],
        out_specs=pl.BlockSpec((tm, tn), lambda i, j: (i, j)),
        out_shape=jax.ShapeDtypeStruct((n, cols), F32),
        scratch_shapes=[pltpu.VMEM((tm, d), BF16)],
        compiler_params=_cparams(("parallel", "arbitrary")),
        name="norm_in_proj",
    )(x, g, w)


def _out_proj_kernel(ya_ref, yb_ref, yc_ref, x_ref, wa_ref, wb_ref, wc_ref, g_ref, o_ref, *, final_norm):
    acc = _dot(ya_ref[...], wa_ref[...]) + _dot(yb_ref[...], wb_ref[...]) + _dot(yc_ref[...], wc_ref[...])
    y = x_ref[...] + acc
    if final_norm:
        ms = jnp.mean(y * y, axis=-1, keepdims=True)
        y = y * lax.rsqrt(ms + EPS) * g_ref[...]
    o_ref[...] = y


def out_proj(ya, yb, yc, x, wa, wb, wc, g, *, tm, final_norm):
    n, d = x.shape
    row = lambda w: pl.BlockSpec((tm, w), lambda i: (i, 0))
    full = lambda a: pl.BlockSpec(a.shape, lambda i: (0, 0))
    return pl.pallas_call(
        functools.partial(_out_proj_kernel, final_norm=final_norm),
        grid=(n // tm,),
        in_specs=[row(ya.shape[1]), row(yb.shape[1]), row(yc.shape[1]), row(d),
                  full(wa), full(wb), full(wc), full(g)],
        out_specs=row(d),
        out_shape=jax.ShapeDtypeStruct((n, d), F32),
        compiler_params=_cparams(("parallel",)),
        name="out_proj",
    )(ya, yb, yc, x, wa, wb, wc, g)


def _compress_kernel(k_ref, v_ref, pe_ref, w1_ref, w2_ref, kc_ref, vct_ref, shift_ref):
    nseg = k_ref.shape[1] // CMP_STRIDE
    for s, src_ref in enumerate((k_ref, v_ref)):
        u1 = jnp.zeros((nseg, LANES), F32)
        u2 = jnp.zeros((nseg, LANES), F32)
        for l in range(CMP_STRIDE):
            x = src_ref[0, pl.ds(l, nseg, stride=CMP_STRIDE), :]
            u1 = u1 + _dot((x + pe_ref[s, l]).astype(BF16), w1_ref[s, l])
            u2 = u2 + _dot((x + pe_ref[s, CMP_STRIDE + l]).astype(BF16), w1_ref[s, CMP_STRIDE + l])
        shift_ref[0:nseg, :] = u2
        shift_ref[nseg:nseg + 8, :] = jnp.zeros((8, LANES), F32)
        pre = u1 + shift_ref[1:nseg + 1, :]
        out = _dot(_silu(pre).astype(BF16), w2_ref[s])
        if s == 0:
            kc_ref[0] = out.astype(BF16)
        else:
            vct_ref[0] = out.T.astype(BF16)


def nsa_compress(proj3, pe, w1, w2):
    bsz, T, _ = proj3.shape
    nseg = T // CMP_STRIDE
    full = lambda a: pl.BlockSpec(a.shape, lambda b: (0,) * a.ndim)
    return pl.pallas_call(
        _compress_kernel,
        grid=(bsz,),
        in_specs=[pl.BlockSpec((1, T, LANES), lambda b: (b, 0, OFF_A_KV // LANES)),
                  pl.BlockSpec((1, T, LANES), lambda b: (b, 0, OFF_A_KV // LANES + 1)),
                  full(pe), full(w1), full(w2)],
        out_specs=[pl.BlockSpec((1, nseg, LANES), lambda b: (b, 0, 0)),
                   pl.BlockSpec((1, LANES, nseg), lambda b: (b, 0, 0))],
        out_shape=[jax.ShapeDtypeStruct((bsz, nseg, LANES), BF16),
                   jax.ShapeDtypeStruct((bsz, LANES, nseg), BF16)],
        scratch_shapes=[pltpu.VMEM((nseg + 8, LANES), F32)],
        compiler_params=_cparams(("parallel",)),
        name="nsa_compress",
    )(proj3, proj3, pe, w1, w2)


def _cmp_select_kernel(q_ref, kc_ref, vct_ref, ovt_ref, ksvs_ref, o_ref, selbt_ref, kaug_ref, vaugt_ref,
                       w_ref, s_ref, p_ref, *, qc, n_top, tk_slc):
    ncmp = kc_ref.shape[1]
    t0 = pl.program_id(1) * qc
    t_row = t0 + lax.broadcasted_iota(jnp.int32, (1, qc), 1)
    cmp_i = lax.broadcasted_iota(jnp.int32, (ncmp, 1), 0)
    vis = (cmp_i * CMP_STRIDE + (CMP_BLOCK - 1)) <= t_row
    any_vis = (t_row >= CMP_BLOCK - 1).astype(F32)
    cur = jnp.right_shift(t_row, 6)
    blk = lax.broadcasted_iota(jnp.int32, (MAX_SEL_BLOCKS, qc), 0)
    valid = blk <= cur
    forced = (blk == 0) | (valid & (blk > cur - SEL_LOCAL))
    qt = (q_ref[0] * SCALE).T.astype(BF16)
    zero = jnp.zeros((HEAD_DIM, qc), BF16)
    for h in range(A_HEADS):
        g, hh = divmod(h, HEADS_PER_GROUP)
        cols = slice(hh * qc, (hh + 1) * qc)
        w_ref[g, g * HEAD_DIM:(g + 1) * HEAD_DIM, cols] = qt[h * HEAD_DIM:(h + 1) * HEAD_DIM, :]
        w_ref[g, (1 - g) * HEAD_DIM:(2 - g) * HEAD_DIM, cols] = zero
    for g in range(A_KV_HEADS):
        s_ref[g] = _dot(kc_ref[0], w_ref[g])
    ovt = ovt_ref[...]
    for g in range(A_KV_HEADS):
        psum = jnp.zeros((ncmp, qc), F32)
        for hh in range(HEADS_PER_GROUP):
            cols = slice(hh * qc, (hh + 1) * qc)
            s = jnp.where(vis, s_ref[g, :, cols], NEG_INF)
            e = jnp.exp(s - jnp.max(s, axis=0, keepdims=True))
            p = e * (any_vis / jnp.sum(e, axis=0, keepdims=True))
            p_ref[g, :, cols] = p.astype(BF16)
            psum = psum + p
        ot = _dot(vct_ref[0], p_ref[g])
        o_ref[0, :, g * GROUP_WIDTH:(g + 1) * GROUP_WIDTH] = _heads_to_rows(ot[g * HEAD_DIM:(g + 1) * HEAD_DIM, :], qc)
        p1 = psum.astype(BF16)
        r1 = psum - p1.astype(F32)
        p2 = r1.astype(BF16)
        p3 = (r1 - p2.astype(F32)).astype(BF16)
        imp = _dot(ovt, p1) + _dot(ovt, p2) + _dot(ovt, p3)
        val = jnp.where(forced, -2.0, jnp.where(valid, imp, -1.0))
        sel = forced
        for _ in range(n_top - (SEL_LOCAL + 1)):
            mx = jnp.max(val, axis=0, keepdims=True)
            first = jnp.min(jnp.where(val == mx, blk, MAX_SEL_BLOCKS), axis=0, keepdims=True)
            hit = blk == first
            sel = sel | hit
            val = jnp.where(hit, -2.0, val)
        chosen = sel & (blk < cur)
        selbt_ref[0, g] = jnp.where(chosen, 0.0, NEG_INF).astype(BF16)

    kv = ksvs_ref[0]
    pos = t0 + lax.broadcasted_iota(jnp.int32, (qc, 1), 0)
    lane = lax.broadcasted_iota(jnp.int32, (1, LANES), 1)
    onehot = jnp.where(jnp.right_shift(pos, 6) == lane, 1.0, 0.0).astype(BF16)
    flane = lax.broadcasted_iota(jnp.int32, (1, AUG_DEPTH - LANES - HEAD_DIM), 1)
    in_block = jnp.bitwise_and(pos, SEL_BLOCK - 1).astype(F32)
    in_tile = jnp.bitwise_and(jnp.right_shift(pos, 6), tk_slc // SEL_BLOCK - 1).astype(F32)
    feat = jnp.where(flane == 0, in_block, jnp.where(flane == 1, in_tile, 0.0)).astype(BF16)
    vt = kv[:, LANES:2 * LANES].T.astype(BF16)
    ones_row = (lax.broadcasted_iota(jnp.int32, (LANES - HEAD_DIM, qc), 0) == 0).astype(BF16)
    for g in range(A_KV_HEADS):
        kaug_ref[0, g, :, 0:LANES] = onehot
        kaug_ref[0, g, :, LANES:LANES + HEAD_DIM] = kv[:, g * HEAD_DIM:(g + 1) * HEAD_DIM].astype(BF16)
        kaug_ref[0, g, :, LANES + HEAD_DIM:AUG_DEPTH] = feat
        vaugt_ref[0, g, 0:HEAD_DIM, :] = vt[g * HEAD_DIM:(g + 1) * HEAD_DIM, :]
        vaugt_ref[0, g, HEAD_DIM:LANES, :] = ones_row


def cmp_select(proj3, kcmp, vcmp_t, overlap_t, *, qc, n_top, tk_slc):
    bsz, T, _ = proj3.shape
    ncmp = kcmp.shape[1]
    G = A_KV_HEADS
    assert n_top > SEL_LOCAL + 1 and (tk_slc // SEL_BLOCK) & (tk_slc // SEL_BLOCK - 1) == 0
    return pl.pallas_call(
        functools.partial(_cmp_select_kernel, qc=qc, n_top=n_top, tk_slc=tk_slc),
        grid=(bsz, T // qc),
        in_specs=[
            pl.BlockSpec((1, qc, A_WIDTH), lambda b, c: (b, c, OFF_A_Q // A_WIDTH)),
            pl.BlockSpec((1, ncmp, LANES), lambda b, c: (b, 0, 0)),
            pl.BlockSpec((1, LANES, ncmp), lambda b, c: (b, 0, 0)),
            pl.BlockSpec((MAX_SEL_BLOCKS, ncmp), lambda b, c: (0, 0)),
            pl.BlockSpec((1, qc, 2 * LANES), lambda b, c: (b, c, (OFF_A_KV + 2 * LANES) // (2 * LANES))),
        ],
        out_specs=[
            pl.BlockSpec((1, qc, A_WIDTH), lambda b, c: (b, c, 0)),
            pl.BlockSpec((1, G, MAX_SEL_BLOCKS, qc), lambda b, c: (b, 0, 0, c)),
            pl.BlockSpec((1, G, qc, AUG_DEPTH), lambda b, c: (b, 0, c, 0)),
            pl.BlockSpec((1, G, LANES, qc), lambda b, c: (b, 0, 0, c)),
        ],
        out_shape=[jax.ShapeDtypeStruct((bsz, T, A_WIDTH), F32),
                   jax.ShapeDtypeStruct((bsz, G, MAX_SEL_BLOCKS, T), BF16),
                   jax.ShapeDtypeStruct((bsz, G, T, AUG_DEPTH), BF16),
                   jax.ShapeDtypeStruct((bsz, G, LANES, T), BF16)],
        scratch_shapes=[
            pltpu.VMEM((G, LANES, HEADS_PER_GROUP * qc), BF16),
            pltpu.VMEM((G, ncmp, HEADS_PER_GROUP * qc), F32),
            pltpu.VMEM((G, ncmp, HEADS_PER_GROUP * qc), BF16),
        ],
        compiler_params=_cparams(("parallel", "parallel")),
        name="nsa_cmp_select",
    )(proj3, kcmp, vcmp_t, overlap_t, proj3)


def _slc_kernel(q_ref, selbt_ref, kaug_ref, vaugt_ref, arow_ref, slope_ref, dbias_ref, o_ref,
                qaug_ref, m_ref, acc_ref, *s_refs, qc, tk):
    t0 = pl.multiple_of(pl.program_id(2) * qc, qc)
    selbt = selbt_ref[0, 0]
    qt = (q_ref[0] * SCALE).T.astype(BF16)
    for hh in range(HEADS_PER_GROUP):
        cols = slice(hh * qc, (hh + 1) * qc)
        qaug_ref[0:LANES, cols] = selbt
        qaug_ref[LANES:LANES + HEAD_DIM, cols] = qt[hh * HEAD_DIM:(hh + 1) * HEAD_DIM, :]
    qaug_ref[LANES + HEAD_DIM:AUG_DEPTH, :] = arow_ref[0]
    m_ref[...] = jnp.full(m_ref.shape, NEG_INF, F32)
    acc_ref[...] = jnp.zeros(acc_ref.shape, F32)

    bufs = (s_refs[:HEADS_PER_GROUP], s_refs[HEADS_PER_GROUP:])

    def scores(kt, hh, dst):
        s0 = pl.multiple_of(kt * tk, tk)
        dst[hh][...] = _dot(kaug_ref[0, 0, pl.ds(s0, tk), :], qaug_ref[:, hh * qc:(hh + 1) * qc])

    def diag_scores(hh, dst):
        ka = kaug_ref[0, 0, pl.ds(t0, qc), LANES:AUG_DEPTH]
        st = _dot(ka, qaug_ref[LANES:AUG_DEPTH, hh * qc:(hh + 1) * qc])
        dst[hh][0:qc, :] = st + dbias_ref[...]

    def softmax_pv(hh, src, rows, vt, off):
        c = slope_ref[0, hh] * off
        m_old = m_ref[hh]
        m_new = jnp.maximum(m_old, jnp.max(src[hh][0:rows, :], axis=0, keepdims=True) + c)
        alpha = jnp.exp(m_old - m_new)
        p = jnp.exp(src[hh][0:rows, :] - (m_new - c)).astype(BF16)
        acc_ref[hh] = alpha * acc_ref[hh] + _dot(vt, p)
        m_ref[hh] = m_new

    def step(kt, src, next_scores):
        s0 = pl.multiple_of(kt * tk, tk)
        vt = vaugt_ref[0, 0, :, pl.ds(s0, tk)]
        off = (s0 - t0).astype(F32)
        for hh in range(HEADS_PER_GROUP):
            next_scores(hh)
            softmax_pv(hh, src, tk, vt, off)

    def main_step(kt, src, dst):
        step(kt, src, lambda hh: scores(kt + 1, hh, dst))

    n_full = lax.div(t0, tk)
    odd = lax.rem(n_full, 2)

    @pl.when(odd == 0)
    def _():
        for hh in range(HEADS_PER_GROUP):
            scores(0, hh, bufs[0])

    @pl.when(odd == 1)
    def _():
        for hh in range(HEADS_PER_GROUP):
            scores(0, hh, bufs[1])
        main_step(0, bufs[1], bufs[0])

    def body(i, carry):
        kt = odd + 2 * i
        main_step(kt, bufs[0], bufs[1])
        main_step(kt + 1, bufs[1], bufs[0])
        return carry

    lax.fori_loop(0, lax.div(n_full, 2), body, 0)
    step(n_full, bufs[0], lambda hh: diag_scores(hh, bufs[1]))
    base = pl.multiple_of(n_full * tk, tk)
    vt_d = vaugt_ref[0, 0, :, pl.ds(t0, qc)]
    for hh in range(HEADS_PER_GROUP):
        softmax_pv(hh, bufs[1], qc, vt_d, (base - t0).astype(F32))
    ot = jnp.concatenate([acc_ref[hh, 0:HEAD_DIM, :] / acc_ref[hh, HEAD_DIM:HEAD_DIM + 1, :]
                          for hh in range(HEADS_PER_GROUP)], axis=0)
    o_ref[0] = ot.T


def slc_attention(proj3, selbt, kaug, vaugt, arows, slopes, *, qc, tk):
    bsz, T, _ = proj3.shape
    G = A_KV_HEADS
    assert tk % qc == 0 and qc % SEL_BLOCK == 0
    ki = np.arange(qc)[:, None]
    qi = np.arange(qc)[None, :]
    dbias = jnp.asarray(np.where((ki // SEL_BLOCK == qi // SEL_BLOCK) & (ki <= qi), 0.0, NEG_INF), dtype=F32)
    return pl.pallas_call(
        functools.partial(_slc_kernel, qc=qc, tk=tk),
        grid=(bsz, G, T // qc),
        in_specs=[
            pl.BlockSpec((1, qc, GROUP_WIDTH), lambda b, g, c: (b, c, OFF_A_Q // GROUP_WIDTH + g)),
            pl.BlockSpec((1, 1, LANES, qc), lambda b, g, c: (b, g, 0, c)),
            pl.BlockSpec((1, 1, T, AUG_DEPTH), lambda b, g, c: (b, g, 0, 0)),
            pl.BlockSpec((1, 1, LANES, T), lambda b, g, c: (b, g, 0, 0)),
            pl.BlockSpec((1, AUG_DEPTH - LANES - HEAD_DIM, HEADS_PER_GROUP * qc), lambda b, g, c: (g, 0, 0)),
            pl.BlockSpec((1, HEADS_PER_GROUP, 1, qc), lambda b, g, c: (g, 0, 0, 0)),
            pl.BlockSpec((qc, qc), lambda b, g, c: (0, 0)),
        ],
        out_specs=pl.BlockSpec((1, qc, GROUP_WIDTH), lambda b, g, c: (b, c, g)),
        out_shape=jax.ShapeDtypeStruct((bsz, T, A_WIDTH), F32),
        scratch_shapes=[
            pltpu.VMEM((AUG_DEPTH, HEADS_PER_GROUP * qc), BF16),
            pltpu.VMEM((HEADS_PER_GROUP, 1, qc), F32),
            pltpu.VMEM((HEADS_PER_GROUP, LANES, qc), F32),
        ] + [pltpu.VMEM((tk, qc), F32)] * (2 * HEADS_PER_GROUP),
        compiler_params=_cparams(("parallel", "parallel", "arbitrary")),
        name="nsa_slc_attention",
    )(proj3, selbt, kaug, vaugt, arows, slopes, dbias)


def _band_kernel(*refs, qc, window, has_sinks, mix):
    q_refs, kv_ref, bias_ref = refs[0:2], refs[2], refs[3]
    pos = 4
    sink_ref = None
    if has_sinks:
        sink_ref = refs[pos]
        pos += 1
    z_refs = refs[pos:pos + 2]
    pos += 2
    if mix:
        ocmp_ref, oslc_ref, gate_ref = refs[pos:pos + 3]
        pos += 3
    o_ref, w_ref, kwin_ref, vt_ref, p_ref = refs[pos:pos + 5]
    s_refs = refs[pos + 5:pos + 7]
    G = len(q_refs)
    span = qc + window
    lanes = HEADS_PER_GROUP * qc
    n_pad_chunks = window // qc
    c = pl.program_id(1)
    ones_row = (lax.broadcasted_iota(jnp.int32, (LANES - HEAD_DIM, span), 0) == 0).astype(BF16)

    def assemble(kv):
        kwin_ref[...] = kv[:, 0:LANES].astype(BF16)
        vt = kv[:, LANES:2 * LANES].T.astype(BF16)
        for g in range(G):
            vt_ref[g, 0:HEAD_DIM, :] = vt[g * HEAD_DIM:(g + 1) * HEAD_DIM, :]
            vt_ref[g, HEAD_DIM:LANES, :] = ones_row

    @pl.when(c >= n_pad_chunks)
    def _():
        start = pl.multiple_of(c * qc - window, qc)
        assemble(kv_ref[0, pl.ds(start, span), :])

    for j in range(n_pad_chunks):
        @pl.when(c == j)
        def _(j=j):
            n_pad = window - j * qc
            assemble(jnp.concatenate([jnp.zeros((n_pad, 2 * LANES), F32), kv_ref[0, 0:span - n_pad, :]], axis=0))

    zero = jnp.zeros((HEAD_DIM, qc), BF16)
    for g in range(G):
        qt = (q_refs[g][0] * SCALE).T.astype(BF16)
        for hh in range(HEADS_PER_GROUP):
            cols = slice(hh * qc, (hh + 1) * qc)
            w_ref[g, g * HEAD_DIM:(g + 1) * HEAD_DIM, cols] = qt[hh * HEAD_DIM:(hh + 1) * HEAD_DIM, :]
            w_ref[g, (1 - g) * HEAD_DIM:(2 - g) * HEAD_DIM, cols] = zero
    pieces = [slice(r, r + LANES) for r in range(0, span, LANES)]
    for g in range(G):
        for rows in pieces:
            s_refs[g][rows, :] = _dot(kwin_ref[rows, :], w_ref[g])
    if mix:
        gate = _sigmoid(gate_ref[0])
    for g in range(G):
        mx8 = None
        for rows in pieces:
            t = s_refs[g][rows, :] + bias_ref[0, rows, g * lanes:(g + 1) * lanes]
            s_refs[g][rows, :] = t
            m8 = jnp.max(t.reshape(LANES // 8, 8, lanes), axis=0)
            mx8 = m8 if mx8 is None else jnp.maximum(mx8, m8)
        mx = jnp.max(mx8, axis=0, keepdims=True)
        if has_sinks:
            sk = sink_ref[:, g * lanes:(g + 1) * lanes]
            mx = jnp.maximum(mx, sk)
        for rows in pieces:
            p_ref[rows, :] = jnp.exp(s_refs[g][rows, :] - mx).astype(BF16)
        ot = _dot(vt_ref[g], p_ref[...])
        den = ot[HEAD_DIM:HEAD_DIM + 1, :]
        if has_sinks:
            den = den + jnp.exp(sk - mx)
        o = _heads_to_rows(ot[0:HEAD_DIM, :] / den, qc)
        z = z_refs[g][0]
        for hh in range(HEADS_PER_GROUP):
            h = g * HEADS_PER_GROUP + hh
            loc = slice(hh * HEAD_DIM, (hh + 1) * HEAD_DIM)
            glob = slice(h * HEAD_DIM, (h + 1) * HEAD_DIM)
            oh = o[:, loc]
            if mix:
                oh = (gate[:, 3 * h:3 * h + 1] * ocmp_ref[0, :, glob]
                      + gate[:, 3 * h + 1:3 * h + 2] * oslc_ref[0, :, glob]
                      + gate[:, 3 * h + 2:3 * h + 3] * oh)
            o_ref[0, :, glob] = (oh * _silu(z[:, loc])).astype(o_ref.dtype)


def _band_bias(qc, window, n_heads):
    span = qc + window
    n_var = window // qc + 1
    slopes = jnp.asarray(_alibi_np(n_heads))
    row = lax.broadcasted_iota(jnp.int32, (span, qc), 0)
    dist = lax.broadcasted_iota(jnp.int32, (span, qc), 1) + window - row
    band = (dist >= 0) & (dist < window)
    term = -(slopes[:, None, None] * dist.astype(F32)[None])
    n_pad = window - jnp.arange(n_var, dtype=jnp.int32)[:, None, None, None] * qc
    ok = band[None, None] & (row[None, None] >= n_pad)
    bias = jnp.where(ok, term[None], NEG_INF)
    return bias.transpose(0, 2, 1, 3).reshape(n_var, span, n_heads * qc)


def band_attention(proj3, sinks, mix_in, *, q_off, kv_off, z_off, qc, window):
    bsz, T, _ = proj3.shape
    G = 2
    n_heads = G * HEADS_PER_GROUP
    span = qc + window
    assert window % qc == 0 and T >= span
    has_sinks = sinks is not None
    mix = mix_in is not None
    bias = _band_bias(qc, window, n_heads)
    n_var = bias.shape[0]
    grp = lambda off: [pl.BlockSpec((1, qc, GROUP_WIDTH), lambda b, c, g=g: (b, c, off // GROUP_WIDTH + g))
                       for g in range(G)]
    in_specs = grp(q_off) + [
        pl.BlockSpec((1, T, 2 * LANES), lambda b, c: (b, 0, kv_off // (2 * LANES))),
        pl.BlockSpec((1, span, n_heads * qc), lambda b, c: (jnp.minimum(c, n_var - 1), 0, 0)),
    ]
    args = [proj3, proj3, proj3, bias]
    if has_sinks:
        in_specs.append(pl.BlockSpec((1, n_heads * qc), lambda b, c: (0, 0)))
        args.append(jnp.repeat(sinks.astype(F32), qc).reshape(1, n_heads * qc))
    in_specs += grp(z_off)
    args += [proj3, proj3]
    if mix:
        full = pl.BlockSpec((1, qc, A_WIDTH), lambda b, c: (b, c, 0))
        in_specs += [full, full, pl.BlockSpec((1, qc, LANES), lambda b, c: (b, c, OFF_A_GATE // LANES))]
        args += [mix_in[0], mix_in[1], proj3]
    return pl.pallas_call(
        functools.partial(_band_kernel, qc=qc, window=window, has_sinks=has_sinks, mix=mix),
        grid=(bsz, T // qc),
        in_specs=in_specs,
        out_specs=pl.BlockSpec((1, qc, G * GROUP_WIDTH), lambda b, c: (b, c, 0)),
        out_shape=jax.ShapeDtypeStruct((bsz, T, G * GROUP_WIDTH), BF16),
        scratch_shapes=[
            pltpu.VMEM((G, LANES, HEADS_PER_GROUP * qc), BF16),
            pltpu.VMEM((span, LANES), BF16),
            pltpu.VMEM((G, LANES, span), BF16),
            pltpu.VMEM((span, HEADS_PER_GROUP * qc), BF16),
        ] + [pltpu.VMEM((span, HEADS_PER_GROUP * qc), F32)] * G,
        compiler_params=_cparams(("parallel", "parallel")),
        name="band_attention_w%d" % window,
    )(*args)


def _log_sigmoid(x):
    return jnp.minimum(x, 0.0) - jnp.log(1.0 + jnp.exp(-jnp.abs(x)))


def _mlstm_kernel(q_ref, k_ref, v_ref, og_ref, z_ref, if_ref, ifb_ref,
                  cwq_ref, cwk_ref, cbq_ref, cbk_ref, g_ref, y_ref,
                  xq_ref, xk_ref, c_ref, n_ref, m_ref, ift_ref, *, L):
    @pl.when(pl.program_id(2) == 0)
    def _():
        xq_ref[0:8, :] = jnp.zeros((8, B_HEAD_DIM), F32)
        xk_ref[0:8, :] = jnp.zeros((8, B_HEAD_DIM), F32)
        c_ref[...] = jnp.zeros(c_ref.shape, F32)
        n_ref[...] = jnp.zeros(n_ref.shape, F32)
        m_ref[...] = jnp.zeros(m_ref.shape, F32)

    def conv_silu(x_ref, hist_ref, w_ref, b_ref):
        hist_ref[8:8 + L, :] = x_ref[0]
        y = b_ref[...]
        for i in range(CONV_WIDTH):
            y = y + w_ref[i:i + 1, :] * hist_ref[5 + i:5 + i + L, :]
        hist_ref[0:8, :] = hist_ref[L:L + 8, :]
        return _silu(y)

    q = conv_silu(q_ref, xq_ref, cwq_ref, cbq_ref)
    k = conv_silu(k_ref, xk_ref, cwk_ref, cbk_ref) * (B_HEAD_DIM ** -0.5)
    v = v_ref[0]
    head = pl.program_id(1)
    gates = if_ref[0] + ifb_ref[...]
    lane = lax.broadcasted_iota(jnp.int32, (1, LANES), 1)
    i_col = jnp.sum(jnp.where(lane == head, gates, 0.0), axis=1, keepdims=True)
    f_col = jnp.sum(jnp.where(lane == B_HEADS + head, gates, 0.0), axis=1, keepdims=True)
    ift_ref[...] = gates.T
    i_row = ift_ref[pl.ds(head, 1), :]
    f_row = ift_ref[pl.ds(B_HEADS + head, 1), :]
    lf_col = _log_sigmoid(f_col)
    lf_row = _log_sigmoid(f_row)
    r_i = lax.broadcasted_iota(jnp.int32, (L, L), 0)
    c_i = lax.broadcasted_iota(jnp.int32, (L, L), 1)
    causal = c_i <= r_i
    b_col = jnp.sum(jnp.where(causal, lf_row, 0.0), axis=1, keepdims=True)
    b_row = jnp.sum(jnp.where(r_i <= c_i, lf_col, 0.0), axis=0, keepdims=True)
    b_last = jnp.sum(lf_row, axis=1, keepdims=True)
    m_prev = m_ref[...]
    log_d = jnp.where(causal, b_col - b_row + i_row, NEG_INF)
    log_inter = b_col + m_prev
    m_t = jnp.maximum(log_inter, jnp.max(log_d, axis=1, keepdims=True))
    w_intra = jnp.exp(log_d - m_t)
    w_inter = jnp.exp(log_inter - m_t)
    qb = q.astype(BF16)
    vb = v.astype(BF16)
    qk = _dot_nt(qb, k.astype(BF16)) * w_intra
    num = w_inter * _dot(qb, c_ref[...].astype(BF16)) + _dot(qk.astype(BF16), vb)
    den = (w_inter * jnp.sum(q * n_ref[...], axis=1, keepdims=True)
           + jnp.sum(qk, axis=1, keepdims=True))
    h = num / jnp.maximum(jnp.abs(den), jnp.exp(-m_t))
    log_g_row = b_last - b_row + i_row
    m_new = jnp.maximum(b_last + m_prev, jnp.max(log_g_row, axis=1, keepdims=True))
    w_g = jnp.exp(b_last - b_col + i_col - m_new)
    decay = jnp.exp(b_last + m_prev - m_new)
    kw = k * w_g
    c_ref[...] = decay * c_ref[...] + _dot_tn(kw.astype(BF16), vb)
    n_ref[...] = decay * n_ref[...] + jnp.sum(kw, axis=0, keepdims=True)
    m_ref[...] = m_new
    hb = _sigmoid(og_ref[0]) * h
    ms = jnp.mean(hb * hb, axis=-1, keepdims=True)
    hb = hb * lax.rsqrt(ms + EPS) * g_ref[...]
    y_ref[0] = (hb * _silu(z_ref[0])).astype(y_ref.dtype)


def mlstm(proj3, if_bias, conv_w, conv_b, norm_g, *, L):
    bsz, T, _ = proj3.shape
    H = B_HEADS
    D = B_HEAD_DIM

    def col(off, k_half=False):
        base = off // D + (H if k_half else 0)
        return pl.BlockSpec((1, L, D), lambda b, h, c: (b, c, base + h))

    def par(rows, k_half=False):
        base = H if k_half else 0
        return pl.BlockSpec((rows, D), lambda b, h, c: (0, base + h))

    in_specs = [
        col(OFF_B_QK), col(OFF_B_QK, True), col(OFF_B_V), col(OFF_B_O), col(OFF_B_Z),
        pl.BlockSpec((1, L, LANES), lambda b, h, c: (b, c, OFF_B_IF // LANES)),
        pl.BlockSpec((1, LANES), lambda b, h, c: (0, 0)),
        par(CONV_WIDTH), par(CONV_WIDTH, True), par(1), par(1, True),
        pl.BlockSpec((1, D), lambda b, h, c: (0, h)),
    ]
    return pl.pallas_call(
        functools.partial(_mlstm_kernel, L=L),
        grid=(bsz, H, T // L),
        in_specs=in_specs,
        out_specs=pl.BlockSpec((1, L, D), lambda b, h, c: (b, c, h)),
        out_shape=jax.ShapeDtypeStruct((bsz, T, B_WIDTH), BF16),
        scratch_shapes=[
            pltpu.VMEM((L + 8, D), F32), pltpu.VMEM((L + 8, D), F32),
            pltpu.VMEM((D, D), F32), pltpu.VMEM((1, D), F32), pltpu.VMEM((1, 1), F32),
            pltpu.VMEM((LANES, L), F32),
        ],
        compiler_params=_cparams(("parallel", "parallel", "arbitrary")),
        name="mlstm_chunkwise",
    )(proj3, proj3, proj3, proj3, proj3, proj3, if_bias, conv_w, conv_w, conv_b, conv_b, norm_g)


def _alibi_np(n_heads):
    s = (2.0 ** (-8.0 * np.arange(1, n_heads + 1, dtype=np.float64) / n_heads)).astype(np.float32)
    for v in (s, SEL_BLOCK * s):
        assert np.array_equal(v.astype(jnp.bfloat16).astype(np.float32), v), "slopes not bf16-exact"
    return s


def _lane_rep(v, groups, width=LANES):
    return jnp.broadcast_to(v.astype(F32).reshape(groups, -1, 1, 1), (groups, v.shape[0] // groups, 1, width))


def _block_diag2(w):
    z = jnp.zeros_like(w)
    return jnp.concatenate([jnp.concatenate([w, z], axis=-1), jnp.concatenate([z, w], axis=-1)], axis=-2)


def _mixers(x2d, bsz, T, p, layer, cfg):
    n, d = x2d.shape
    G = A_KV_HEADS
    w_packed = repack_w_in(p["w_in"], layer, tr=cfg["tr_repack"])
    proj = norm_matmul(x2d, p["norm_g"].reshape(1, d), w_packed, tm=cfg["tm_in"], tn=cfg["tn_in"])
    proj3 = proj.reshape(bsz, T, PROJ_COLS)

    nseg = T // CMP_STRIDE
    pe = jnp.stack([p["cmp_pe_k"], p["cmp_pe_v"]])
    pe = jnp.concatenate([pe, pe], axis=-1).reshape(2, CMP_BLOCK, 1, LANES)
    w1 = jnp.stack([p["cmp_w1_k"], p["cmp_w1_v"]]).reshape(2, CMP_BLOCK, HEAD_DIM, HEAD_DIM)
    w2 = jnp.stack([p["cmp_w2_k"], p["cmp_w2_v"]])
    k_cmp, v_cmp_t = nsa_compress(proj3, pe, _block_diag2(w1).astype(BF16), _block_diag2(w2).astype(BF16))

    n_sel = T // SEL_BLOCK
    assert n_sel <= MAX_SEL_BLOCKS
    qc_s, tk_s = cfg["qc_slc"], cfg["tk_slc"]
    ci = np.arange(nseg)[None, :] * CMP_STRIDE
    sj = np.arange(MAX_SEL_BLOCKS)[:, None] * SEL_BLOCK
    overlap_t = jnp.asarray(((ci < sj + SEL_BLOCK) & (ci + CMP_BLOCK > sj)).astype(np.float32), dtype=BF16)
    o_cmp, selbt, kaug, vaugt = cmp_select(proj3, k_cmp, v_cmp_t, overlap_t, qc=cfg["qc_cmp"],
                                           n_top=min(SEL_TOPK, n_sel), tk_slc=tk_s)

    sl_np = _alibi_np(A_HEADS)
    arows = np.zeros((A_HEADS, AUG_DEPTH - LANES - HEAD_DIM, qc_s), np.float32)
    arows[:, 0, :] = sl_np[:, None]
    arows[:, 1, :] = SEL_BLOCK * sl_np[:, None]
    arows = arows.reshape(G, HEADS_PER_GROUP, -1, qc_s).transpose(0, 2, 1, 3).reshape(G, -1, HEADS_PER_GROUP * qc_s)
    o_slc = slc_attention(proj3, selbt, kaug, vaugt, jnp.asarray(arows, dtype=BF16),
                          _lane_rep(jnp.asarray(sl_np), G, qc_s), qc=qc_s, tk=tk_s)

    y_a = band_attention(proj3, None, (o_cmp, o_slc), q_off=OFF_A_Q, kv_off=OFF_A_KV + 4 * LANES, z_off=OFF_A_Z,
                         qc=cfg["qc_win"], window=NSA_WINDOW)

    if_bias = jnp.concatenate([p["mlstm_i_bias"], p["mlstm_f_bias"], jnp.zeros((LANES - N_IF,), F32)]).reshape(1, LANES)
    y_b = mlstm(proj3, if_bias, p["mlstm_conv_w"], p["mlstm_conv_b"].reshape(1, -1),
                p["mlstm_norm_g"].reshape(1, -1), L=cfg["mlstm_chunk"])

    y_c = band_attention(proj3, p["swa_sinks"], None, q_off=OFF_C_Q, kv_off=OFF_C_KV, z_off=OFF_C_Z,
                         qc=cfg["qc_win"], window=SWA_WINDOW)
    return y_a.reshape(n, A_WIDTH), y_b.reshape(n, B_WIDTH), y_c.reshape(n, C_WIDTH), (o_cmp, o_slc, selbt)


def _layer(x2d, bsz, T, p, layer, final_g, cfg):
    n, d = x2d.shape
    y_a, y_b, y_c, _ = _mixers(x2d, bsz, T, p, layer, cfg)
    w_out = p["w_out"].astype(BF16)
    g = final_g if final_g is not None else p["norm_g"]
    return out_proj(y_a, y_b, y_c, x2d,
                    w_out[:A_WIDTH], w_out[A_WIDTH:A_WIDTH + B_WIDTH], w_out[A_WIDTH + B_WIDTH:],
                    g.reshape(1, d), tm=cfg["tm_out"], final_norm=final_g is not None)


def _config(T):
    return dict(tr_repack=256, tm_in=min(1024, T), tn_in=768, tm_out=min(512, T),
                qc_cmp=128, qc_slc=256, tk_slc=512, qc_win=128, mlstm_chunk=min(256, T))


def kernel(x, norm_g, w_in, w_out, cmp_pe_k, cmp_w1_k, cmp_w2_k, cmp_pe_v, cmp_w1_v, cmp_w2_v, mlstm_conv_w, mlstm_conv_b, mlstm_i_bias, mlstm_f_bias, mlstm_norm_g, swa_sinks, final_norm_g):
    bsz, T, d = x.shape
    depth = w_in.shape[0]
    cfg = _config(T)
    x2d = x.reshape(bsz * T, d)
    for l in range(depth):
        p = dict(norm_g=norm_g[l], w_in=w_in, w_out=w_out[l],
                 cmp_pe_k=cmp_pe_k[l], cmp_w1_k=cmp_w1_k[l], cmp_w2_k=cmp_w2_k[l],
                 cmp_pe_v=cmp_pe_v[l], cmp_w1_v=cmp_w1_v[l], cmp_w2_v=cmp_w2_v[l],
                 mlstm_conv_w=mlstm_conv_w[l], mlstm_conv_b=mlstm_conv_b[l],
                 mlstm_i_bias=mlstm_i_bias[l], mlstm_f_bias=mlstm_f_bias[l],
                 mlstm_norm_g=mlstm_norm_g[l], swa_sinks=swa_sinks[l])
        x2d = _layer(x2d, bsz, T, p, l, final_norm_g if l == depth - 1 else None, cfg)
    return x2d.reshape(bsz, T, d)
```

```python
import functools

import numpy as np
import jax
import jax.numpy as jnp
from jax import lax
from jax.experimental import pallas as pl
from jax.experimental.pallas import tpu as pltpu

F32 = jnp.float32
BF16 = jnp.bfloat16

HEAD_DIM = 64
A_HEADS = 8
A_KV_HEADS = 2
HEADS_PER_GROUP = A_HEADS // A_KV_HEADS
GROUP_WIDTH = HEADS_PER_GROUP * HEAD_DIM
A_WIDTH = A_HEADS * HEAD_DIM
C_HEADS = 8
C_KV_HEADS = 2
C_WIDTH = C_HEADS * HEAD_DIM
B_HEADS = 4
B_HEAD_DIM = 256
B_WIDTH = B_HEADS * B_HEAD_DIM
CMP_BLOCK = 32
CMP_STRIDE = 16
SEL_BLOCK = 64
SEL_TOPK = 16
SEL_LOCAL = 2
NSA_WINDOW = 512
SWA_WINDOW = 128
CONV_WIDTH = 4
FORCE_BONUS = 1.0e4
NEG_INF = -1.0e30
EPS = 1.0e-6
SCALE = HEAD_DIM ** -0.5
LANES = 128
MAX_SEL_BLOCKS = LANES
AUG_DEPTH = 256
VAUG_ROWS = 80

IN_SPLITS = (A_WIDTH, 128, 128, 128, 128, 128, 128, A_HEADS * 3, A_WIDTH,
             2 * B_WIDTH, B_WIDTH, B_HEADS, B_HEADS, B_WIDTH, B_WIDTH,
             C_WIDTH, 128, 128, C_WIDTH)
IN_COLS = sum(IN_SPLITS)
_SRC = np.concatenate([[0], np.cumsum(IN_SPLITS)]).astype(int)
SRC_GATE, SRC_A_Z, SRC_B_IF, SRC_B_O = int(_SRC[7]), int(_SRC[8]), int(_SRC[11]), int(_SRC[13])
N_GATE = A_HEADS * 3
N_IF = 2 * B_HEADS
OFF_A_Q = 0
OFF_A_KV = 512
OFF_A_Z = 1280
OFF_B_QK = 1792
OFF_B_V = 3840
OFF_B_O = 4864
OFF_B_Z = 5888
OFF_C_Q = 6912
OFF_C_KV = 7424
OFF_C_Z = 7680
OFF_A_GATE = 8192
OFF_B_IF = 8320
PROJ_COLS = 8448
W_RUNS = ((0, OFF_A_Q, SRC_GATE), (SRC_A_Z, OFF_A_Z, SRC_B_IF - SRC_A_Z), (SRC_B_O, OFF_B_O, IN_COLS - SRC_B_O))
assert OFF_A_Z + (SRC_B_IF - SRC_A_Z) == OFF_B_O and OFF_B_O + (IN_COLS - SRC_B_O) == OFF_A_GATE

VMEM_LIMIT = 48 * 1024 * 1024


def _cparams(sem):
    return pltpu.CompilerParams(dimension_semantics=sem, vmem_limit_bytes=VMEM_LIMIT)


def _dot(a, b):
    return jnp.dot(a, b, preferred_element_type=F32)


def _dot_nt(a, b):
    return lax.dot_general(a, b, (((1,), (1,)), ((), ())), preferred_element_type=F32)


def _dot_tn(a, b):
    return lax.dot_general(a, b, (((0,), (0,)), ((), ())), preferred_element_type=F32)


def _sigmoid(x):
    return 1.0 / (1.0 + jnp.exp(-x))


def _silu(x):
    return x * _sigmoid(x)


def _heads_to_rows(ot, qc):
    return jnp.concatenate([ot[:, hh * qc:(hh + 1) * qc] for hh in range(HEADS_PER_GROUP)], axis=0).T


def _repack_kernel(wt_ref, o_ref):
    for src, dst, width in W_RUNS:
        o_ref[dst:dst + width, :] = wt_ref[0, src:src + width, :].astype(BF16)
    o_ref[OFF_A_GATE:PROJ_COLS, :] = jnp.zeros((PROJ_COLS - OFF_A_GATE, o_ref.shape[1]), BF16)
    o_ref[OFF_A_GATE:OFF_A_GATE + N_GATE, :] = wt_ref[0, SRC_GATE:SRC_GATE + N_GATE, :].astype(BF16)
    o_ref[OFF_B_IF:OFF_B_IF + N_IF, :] = wt_ref[0, SRC_B_IF:SRC_B_IF + N_IF, :].astype(BF16)


def repack_w_in(w_in_t, layer, *, tc):
    d = w_in_t.shape[2]
    return pl.pallas_call(
        _repack_kernel,
        grid=(d // tc,),
        in_specs=[pl.BlockSpec((1, IN_COLS, tc), lambda i: (layer, 0, i))],
        out_specs=pl.BlockSpec((PROJ_COLS, tc), lambda i: (0, i)),
        out_shape=jax.ShapeDtypeStruct((PROJ_COLS, d), BF16),
        compiler_params=_cparams(("parallel",)),
        name="repack_w_in",
    )(w_in_t)


def _norm_matmul_kernel(x_ref, g_ref, wt_ref, o_ref, h_ref):
    @pl.when(pl.program_id(1) == 0)
    def _():
        x = x_ref[...]
        ms = jnp.mean(x * x, axis=-1, keepdims=True)
        h_ref[...] = (x * lax.rsqrt(ms + EPS) * g_ref[...]).astype(BF16)

    o_ref[...] = _dot_nt(h_ref[...], wt_ref[...])


def norm_matmul(x, g, wt, *, tm, tn):
    n, d = x.shape
    cols = wt.shape[0]
    return pl.pallas_call(
        _norm_matmul_kernel,
        grid=(n // tm, cols // tn),
        in_specs=[
            pl.BlockSpec((tm, d), lambda i, j: (i, 0)),
            pl.BlockSpec((1, d), lambda i, j: (0, 0)),
            pl.BlockSpec((tn, d), lambda i, j: (j, 0)),
        ],
        out_specs=pl.BlockSpec((tm, tn), lambda i, j: (i, j)),
        out_shape=jax.ShapeDtypeStruct((n, cols), F32),
        scratch_shapes=[pltpu.VMEM((tm, d), BF16)],
        compiler_params=_cparams(("parallel", "arbitrary")),
        name="norm_in_proj",
    )(x, g, wt)


def _out_proj_kernel(ya_ref, yb_ref, yc_ref, x_ref, wa_ref, wb_ref, wc_ref, g_ref, o_ref, *, final_norm):
    acc = _dot(ya_ref[...], wa_ref[...]) + _dot(yb_ref[...], wb_ref[...]) + _dot(yc_ref[...], wc_ref[...])
    y = x_ref[...] + acc
    if final_norm:
        ms = jnp.mean(y * y, axis=-1, keepdims=True)
        y = y * lax.rsqrt(ms + EPS) * g_ref[...]
    o_ref[...] = y


def out_proj(ya, yb, yc, x, wa, wb, wc, g, *, tm, final_norm):
    n, d = x.shape
    row = lambda w: pl.BlockSpec((tm, w), lambda i: (i, 0))
    full = lambda a: pl.BlockSpec(a.shape, lambda i: (0, 0))
    return pl.pallas_call(
        functools.partial(_out_proj_kernel, final_norm=final_norm),
        grid=(n // tm,),
        in_specs=[row(ya.shape[1]), row(yb.shape[1]), row(yc.shape[1]), row(d),
                  full(wa), full(wb), full(wc), full(g)],
        out_specs=row(d),
        out_shape=jax.ShapeDtypeStruct((n, d), F32),
        compiler_params=_cparams(("parallel",)),
        name="out_proj",
    )(ya, yb, yc, x, wa, wb, wc, g)


def _compress_kernel(k_ref, v_ref, pe_ref, w1_ref, w2_ref, kc_ref, vct_ref, shift_ref):
    nseg = k_ref.shape[1] // CMP_STRIDE
    for s, src_ref in enumerate((k_ref, v_ref)):
        u1 = jnp.zeros((nseg, LANES), F32)
        u2 = jnp.zeros((nseg, LANES), F32)
        for l in range(CMP_STRIDE):
            x = src_ref[0, pl.ds(l, nseg, stride=CMP_STRIDE), :]
            u1 = u1 + _dot((x + pe_ref[s, l]).astype(BF16), w1_ref[s, l])
            u2 = u2 + _dot((x + pe_ref[s, CMP_STRIDE + l]).astype(BF16), w1_ref[s, CMP_STRIDE + l])
        shift_ref[0:nseg, :] = u2
        shift_ref[nseg:nseg + 8, :] = jnp.zeros((8, LANES), F32)
        pre = u1 + shift_ref[1:nseg + 1, :]
        out = _dot(_silu(pre).astype(BF16), w2_ref[s])
        if s == 0:
            kc_ref[0] = out.astype(BF16)
        else:
            vct_ref[0] = out.T.astype(BF16)


def nsa_compress(proj3, pe, w1, w2):
    bsz, T, _ = proj3.shape
    nseg = T // CMP_STRIDE
    full = lambda a: pl.BlockSpec(a.shape, lambda b: (0,) * a.ndim)
    return pl.pallas_call(
        _compress_kernel,
        grid=(bsz,),
        in_specs=[pl.BlockSpec((1, T, LANES), lambda b: (b, 0, OFF_A_KV // LANES)),
                  pl.BlockSpec((1, T, LANES), lambda b: (b, 0, OFF_A_KV // LANES + 1)),
                  full(pe), full(w1), full(w2)],
        out_specs=[pl.BlockSpec((1, nseg, LANES), lambda b: (b, 0, 0)),
                   pl.BlockSpec((1, LANES, nseg), lambda b: (b, 0, 0))],
        out_shape=[jax.ShapeDtypeStruct((bsz, nseg, LANES), BF16),
                   jax.ShapeDtypeStruct((bsz, LANES, nseg), BF16)],
        scratch_shapes=[pltpu.VMEM((nseg + 8, LANES), F32)],
        compiler_params=_cparams(("parallel",)),
        name="nsa_compress",
    )(proj3, proj3, pe, w1, w2)


def _cmp_select_kernel(q_ref, kc_ref, vct_ref, ovt_ref, ksvs_ref, o_ref, selbt_ref, kaug_ref, vaugt_ref,
                       w_ref, s_ref, p_ref, *, qc, n_top, tk_slc):
    ncmp = kc_ref.shape[1]
    t0 = pl.program_id(1) * qc
    t_row = t0 + lax.broadcasted_iota(jnp.int32, (1, qc), 1)
    cmp_i = lax.broadcasted_iota(jnp.int32, (ncmp, 1), 0)
    vis = (cmp_i * CMP_STRIDE + (CMP_BLOCK - 1)) <= t_row
    any_vis = (t_row >= CMP_BLOCK - 1).astype(F32)
    cur = jnp.right_shift(t_row, 6)
    blk = lax.broadcasted_iota(jnp.int32, (MAX_SEL_BLOCKS, qc), 0)
    valid = blk <= cur
    forced = (blk == 0) | (valid & (blk > cur - SEL_LOCAL))
    qt = (q_ref[0] * SCALE).T.astype(BF16)
    zero = jnp.zeros((HEAD_DIM, qc), BF16)
    for h in range(A_HEADS):
        g, hh = divmod(h, HEADS_PER_GROUP)
        cols = slice(hh * qc, (hh + 1) * qc)
        w_ref[g, g * HEAD_DIM:(g + 1) * HEAD_DIM, cols] = qt[h * HEAD_DIM:(h + 1) * HEAD_DIM, :]
        w_ref[g, (1 - g) * HEAD_DIM:(2 - g) * HEAD_DIM, cols] = zero
    for g in range(A_KV_HEADS):
        s_ref[g] = _dot(kc_ref[0], w_ref[g])
    ovt = ovt_ref[...]
    for g in range(A_KV_HEADS):
        psum = jnp.zeros((ncmp, qc), F32)
        for hh in range(HEADS_PER_GROUP):
            cols = slice(hh * qc, (hh + 1) * qc)
            s = jnp.where(vis, s_ref[g, :, cols], NEG_INF)
            e = jnp.exp(s - jnp.max(s, axis=0, keepdims=True))
            p = e * (any_vis / jnp.sum(e, axis=0, keepdims=True))
            p_ref[g, :, cols] = p.astype(BF16)
            psum = psum + p
        ot = _dot(vct_ref[0], p_ref[g])
        o_ref[0, :, g * GROUP_WIDTH:(g + 1) * GROUP_WIDTH] = _heads_to_rows(ot[g * HEAD_DIM:(g + 1) * HEAD_DIM, :], qc)
        p1 = psum.astype(BF16)
        r1 = psum - p1.astype(F32)
        p2 = r1.astype(BF16)
        p3 = (r1 - p2.astype(F32)).astype(BF16)
        imp = _dot(ovt, p1) + _dot(ovt, p2) + _dot(ovt, p3)
        val = jnp.where(forced, -2.0, jnp.where(valid, imp, -1.0))
        for _ in range(n_top - (SEL_LOCAL + 1)):
            mx = jnp.max(val, axis=0, keepdims=True)
            first = jnp.min(jnp.where(val == mx, blk, MAX_SEL_BLOCKS), axis=0, keepdims=True)
            val = jnp.where(blk == first, -2.0, val)
        chosen = (val == -2.0) & (blk < cur)
        selbt_ref[0, g] = jnp.where(chosen, 0.0, NEG_INF).astype(BF16)

    kv = ksvs_ref[0]
    pos = t0 + lax.broadcasted_iota(jnp.int32, (qc, 1), 0)
    lane = lax.broadcasted_iota(jnp.int32, (1, LANES), 1)
    onehot = jnp.where(jnp.right_shift(pos, 6) == lane, 1.0, 0.0).astype(BF16)
    flane = lax.broadcasted_iota(jnp.int32, (1, AUG_DEPTH - LANES - HEAD_DIM), 1)
    in_block = jnp.bitwise_and(pos, SEL_BLOCK - 1).astype(F32)
    in_tile = jnp.bitwise_and(jnp.right_shift(pos, 6), tk_slc // SEL_BLOCK - 1).astype(F32)
    feat = jnp.where(flane == 0, in_block, jnp.where(flane == 1, in_tile, 0.0)).astype(BF16)
    vt = kv[:, LANES:2 * LANES].T.astype(BF16)
    ones_row = (lax.broadcasted_iota(jnp.int32, (VAUG_ROWS - HEAD_DIM, qc), 0) == 0).astype(BF16)
    for g in range(A_KV_HEADS):
        kaug_ref[0, g, :, 0:LANES] = onehot
        kaug_ref[0, g, :, LANES:LANES + HEAD_DIM] = kv[:, g * HEAD_DIM:(g + 1) * HEAD_DIM].astype(BF16)
        kaug_ref[0, g, :, LANES + HEAD_DIM:AUG_DEPTH] = feat
        vaugt_ref[0, g, 0:HEAD_DIM, :] = vt[g * HEAD_DIM:(g + 1) * HEAD_DIM, :]
        vaugt_ref[0, g, HEAD_DIM:VAUG_ROWS, :] = ones_row


def cmp_select(proj3, kcmp, vcmp_t, overlap_t, *, qc, n_top, tk_slc):
    bsz, T, _ = proj3.shape
    ncmp = kcmp.shape[1]
    G = A_KV_HEADS
    assert n_top > SEL_LOCAL + 1 and (tk_slc // SEL_BLOCK) & (tk_slc // SEL_BLOCK - 1) == 0
    return pl.pallas_call(
        functools.partial(_cmp_select_kernel, qc=qc, n_top=n_top, tk_slc=tk_slc),
        grid=(bsz, T // qc),
        in_specs=[
            pl.BlockSpec((1, qc, A_WIDTH), lambda b, c: (b, c, OFF_A_Q // A_WIDTH)),
            pl.BlockSpec((1, ncmp, LANES), lambda b, c: (b, 0, 0)),
            pl.BlockSpec((1, LANES, ncmp), lambda b, c: (b, 0, 0)),
            pl.BlockSpec((MAX_SEL_BLOCKS, ncmp), lambda b, c: (0, 0)),
            pl.BlockSpec((1, qc, 2 * LANES), lambda b, c: (b, c, (OFF_A_KV + 2 * LANES) // (2 * LANES))),
        ],
        out_specs=[
            pl.BlockSpec((1, qc, A_WIDTH), lambda b, c: (b, c, 0)),
            pl.BlockSpec((1, G, MAX_SEL_BLOCKS, qc), lambda b, c: (b, 0, 0, c)),
            pl.BlockSpec((1, G, qc, AUG_DEPTH), lambda b, c: (b, 0, c, 0)),
            pl.BlockSpec((1, G, VAUG_ROWS, qc), lambda b, c: (b, 0, 0, c)),
        ],
        out_shape=[jax.ShapeDtypeStruct((bsz, T, A_WIDTH), F32),
                   jax.ShapeDtypeStruct((bsz, G, MAX_SEL_BLOCKS, T), BF16),
                   jax.ShapeDtypeStruct((bsz, G, T, AUG_DEPTH), BF16),
                   jax.ShapeDtypeStruct((bsz, G, VAUG_ROWS, T), BF16)],
        scratch_shapes=[
            pltpu.VMEM((G, LANES, HEADS_PER_GROUP * qc), BF16),
            pltpu.VMEM((G, ncmp, HEADS_PER_GROUP * qc), F32),
            pltpu.VMEM((G, ncmp, HEADS_PER_GROUP * qc), BF16),
        ],
        compiler_params=_cparams(("parallel", "parallel")),
        name="nsa_cmp_select",
    )(proj3, kcmp, vcmp_t, overlap_t, proj3)


def _slc_kernel(q_ref, selbt_ref, kaug_ref, vaugt_ref, arow_ref, slope_ref, dbias_ref, o_ref,
                qaug_ref, m_ref, acc_ref, *s_refs, qc, tk):
    t0 = pl.multiple_of(pl.program_id(2) * qc, qc)
    selbt = selbt_ref[0, 0]
    qt = (q_ref[0] * SCALE).T.astype(BF16)
    for hh in range(HEADS_PER_GROUP):
        cols = slice(hh * qc, (hh + 1) * qc)
        qaug_ref[0:LANES, cols] = selbt
        qaug_ref[LANES:LANES + HEAD_DIM, cols] = qt[hh * HEAD_DIM:(hh + 1) * HEAD_DIM, :]
    qaug_ref[LANES + HEAD_DIM:AUG_DEPTH, :] = arow_ref[0]
    m_ref[...] = jnp.full(m_ref.shape, NEG_INF, F32)
    acc_ref[...] = jnp.zeros(acc_ref.shape, F32)

    bufs = (s_refs[:HEADS_PER_GROUP], s_refs[HEADS_PER_GROUP:])

    def scores(kt, hh, dst):
        s0 = pl.multiple_of(kt * tk, tk)
        dst[hh][...] = _dot(kaug_ref[0, 0, pl.ds(s0, tk), :], qaug_ref[:, hh * qc:(hh + 1) * qc])

    def diag_scores(hh, dst):
        ka = kaug_ref[0, 0, pl.ds(t0, qc), LANES:AUG_DEPTH]
        st = _dot(ka, qaug_ref[LANES:AUG_DEPTH, hh * qc:(hh + 1) * qc])
        dst[hh][0:qc, :] = st + dbias_ref[...]

    def softmax_pv(hh, src, rows, vt, off):
        c = slope_ref[0, hh] * off
        m_old = m_ref[hh]
        m_new = jnp.maximum(m_old, jnp.max(src[hh][0:rows, :], axis=0, keepdims=True) + c)
        alpha = jnp.exp(m_old - m_new)
        p = jnp.exp(src[hh][0:rows, :] - (m_new - c)).astype(BF16)
        acc_ref[hh] = alpha * acc_ref[hh] + _dot(vt, p)
        m_ref[hh] = m_new

    def step(kt, src, next_scores):
        s0 = pl.multiple_of(kt * tk, tk)
        vt = vaugt_ref[0, 0, :, pl.ds(s0, tk)]
        off = (s0 - t0).astype(F32)
        for hh in range(HEADS_PER_GROUP):
            next_scores(hh)
            softmax_pv(hh, src, tk, vt, off)

    def main_step(kt, src, dst):
        step(kt, src, lambda hh: scores(kt + 1, hh, dst))

    n_full = lax.div(t0, tk)
    odd = lax.rem(n_full, 2)

    @pl.when(odd == 0)
    def _():
        for hh in range(HEADS_PER_GROUP):
            scores(0, hh, bufs[0])

    @pl.when(odd == 1)
    def _():
        for hh in range(HEADS_PER_GROUP):
            scores(0, hh, bufs[1])
        main_step(0, bufs[1], bufs[0])

    def body(i, carry):
        kt = odd + 2 * i
        main_step(kt, bufs[0], bufs[1])
        main_step(kt + 1, bufs[1], bufs[0])
        return carry

    lax.fori_loop(0, lax.div(n_full, 2), body, 0)
    step(n_full, bufs[0], lambda hh: diag_scores(hh, bufs[1]))
    base = pl.multiple_of(n_full * tk, tk)
    vt_d = vaugt_ref[0, 0, :, pl.ds(t0, qc)]
    for hh in range(HEADS_PER_GROUP):
        softmax_pv(hh, bufs[1], qc, vt_d, (base - t0).astype(F32))
    ot = jnp.concatenate([acc_ref[hh, 0:HEAD_DIM, :] / acc_ref[hh, HEAD_DIM:HEAD_DIM + 1, :]
                          for hh in range(HEADS_PER_GROUP)], axis=0)
    o_ref[0] = ot.T


def slc_attention(proj3, selbt, kaug, vaugt, arows, slopes, *, qc, tk):
    bsz, T, _ = proj3.shape
    G = A_KV_HEADS
    assert tk % qc == 0 and qc % SEL_BLOCK == 0
    ki = np.arange(qc)[:, None]
    qi = np.arange(qc)[None, :]
    dbias = jnp.asarray(np.where((ki // SEL_BLOCK == qi // SEL_BLOCK) & (ki <= qi), 0.0, NEG_INF), dtype=F32)
    return pl.pallas_call(
        functools.partial(_slc_kernel, qc=qc, tk=tk),
        grid=(bsz, G, T // qc),
        in_specs=[
            pl.BlockSpec((1, qc, GROUP_WIDTH), lambda b, g, c: (b, c, OFF_A_Q // GROUP_WIDTH + g)),
            pl.BlockSpec((1, 1, LANES, qc), lambda b, g, c: (b, g, 0, c)),
            pl.BlockSpec((1, 1, T, AUG_DEPTH), lambda b, g, c: (b, g, 0, 0)),
            pl.BlockSpec((1, 1, VAUG_ROWS, T), lambda b, g, c: (b, g, 0, 0)),
            pl.BlockSpec((1, AUG_DEPTH - LANES - HEAD_DIM, HEADS_PER_GROUP * qc), lambda b, g, c: (g, 0, 0)),
            pl.BlockSpec((1, HEADS_PER_GROUP, 1, qc), lambda b, g, c: (g, 0, 0, 0)),
            pl.BlockSpec((qc, qc), lambda b, g, c: (0, 0)),
        ],
        out_specs=pl.BlockSpec((1, qc, GROUP_WIDTH), lambda b, g, c: (b, c, g)),
        out_shape=jax.ShapeDtypeStruct((bsz, T, A_WIDTH), F32),
        scratch_shapes=[
            pltpu.VMEM((AUG_DEPTH, HEADS_PER_GROUP * qc), BF16),
            pltpu.VMEM((HEADS_PER_GROUP, 1, qc), F32),
            pltpu.VMEM((HEADS_PER_GROUP, VAUG_ROWS, qc), F32),
        ] + [pltpu.VMEM((tk, qc), F32)] * (2 * HEADS_PER_GROUP),
        compiler_params=_cparams(("parallel", "parallel", "arbitrary")),
        name="nsa_slc_attention",
    )(proj3, selbt, kaug, vaugt, arows, slopes, dbias)


def _band_kernel(*refs, qc, window, has_sinks, mix):
    q_refs, kv_ref, bias_ref = refs[0:2], refs[2], refs[3]
    pos = 4
    sink_ref = None
    if has_sinks:
        sink_ref = refs[pos]
        pos += 1
    z_refs = refs[pos:pos + 2]
    pos += 2
    if mix:
        ocmp_ref, oslc_ref, gate_ref = refs[pos:pos + 3]
        pos += 3
    o_ref, w_ref, kwin_ref, vt_ref, p_ref = refs[pos:pos + 5]
    s_refs = refs[pos + 5:pos + 7]
    G = len(q_refs)
    span = qc + window
    lanes = HEADS_PER_GROUP * qc
    n_pad_chunks = window // qc
    c = pl.program_id(1)
    ones_row = (lax.broadcasted_iota(jnp.int32, (VAUG_ROWS - HEAD_DIM, span), 0) == 0).astype(BF16)

    def assemble(kv):
        kwin_ref[...] = kv[:, 0:LANES].astype(BF16)
        vt = kv[:, LANES:2 * LANES].T.astype(BF16)
        for g in range(G):
            vt_ref[g, 0:HEAD_DIM, :] = vt[g * HEAD_DIM:(g + 1) * HEAD_DIM, :]
            vt_ref[g, HEAD_DIM:VAUG_ROWS, :] = ones_row

    @pl.when(c >= n_pad_chunks)
    def _():
        start = pl.multiple_of(c * qc - window, qc)
        assemble(kv_ref[0, pl.ds(start, span), :])

    for j in range(n_pad_chunks):
        @pl.when(c == j)
        def _(j=j):
            n_pad = window - j * qc
            assemble(jnp.concatenate([jnp.zeros((n_pad, 2 * LANES), F32), kv_ref[0, 0:span - n_pad, :]], axis=0))

    zero = jnp.zeros((HEAD_DIM, qc), BF16)
    for g in range(G):
        qt = (q_refs[g][0] * SCALE).T.astype(BF16)
        for hh in range(HEADS_PER_GROUP):
            cols = slice(hh * qc, (hh + 1) * qc)
            w_ref[g, g * HEAD_DIM:(g + 1) * HEAD_DIM, cols] = qt[hh * HEAD_DIM:(hh + 1) * HEAD_DIM, :]
            w_ref[g, (1 - g) * HEAD_DIM:(2 - g) * HEAD_DIM, cols] = zero
    pieces = [slice(r, r + LANES) for r in range(0, span, LANES)]
    for g in range(G):
        for rows in pieces:
            s_refs[g][rows, :] = _dot(kwin_ref[rows, :], w_ref[g])
    if mix:
        gate = _sigmoid(gate_ref[0])
    for g in range(G):
        mx8 = None
        for rows in pieces:
            t = s_refs[g][rows, :] + bias_ref[0, rows, g * lanes:(g + 1) * lanes]
            s_refs[g][rows, :] = t
            m8 = jnp.max(t.reshape(LANES // 8, 8, lanes), axis=0)
            mx8 = m8 if mx8 is None else jnp.maximum(mx8, m8)
        mx = jnp.max(mx8, axis=0, keepdims=True)
        if has_sinks:
            sk = sink_ref[:, g * lanes:(g + 1) * lanes]
            mx = jnp.maximum(mx, sk)
        for rows in pieces:
            p_ref[rows, :] = jnp.exp(s_refs[g][rows, :] - mx).astype(BF16)
        ot = _dot(vt_ref[g], p_ref[...])
        den = ot[HEAD_DIM:HEAD_DIM + 1, :]
        if has_sinks:
            den = den + jnp.exp(sk - mx)
        o = _heads_to_rows(ot[0:HEAD_DIM, :] / den, qc)
        z = z_refs[g][0]
        for hh in range(HEADS_PER_GROUP):
            h = g * HEADS_PER_GROUP + hh
            loc = slice(hh * HEAD_DIM, (hh + 1) * HEAD_DIM)
            glob = slice(h * HEAD_DIM, (h + 1) * HEAD_DIM)
            oh = o[:, loc]
            if mix:
                oh = (gate[:, 3 * h:3 * h + 1] * ocmp_ref[0, :, glob]
                      + gate[:, 3 * h + 1:3 * h + 2] * oslc_ref[0, :, glob]
                      + gate[:, 3 * h + 2:3 * h + 3] * oh)
            o_ref[0, :, glob] = (oh * _silu(z[:, loc])).astype(o_ref.dtype)


def _band_bias(qc, window, n_heads):
    span = qc + window
    n_var = window // qc + 1
    slopes = jnp.asarray(_alibi_np(n_heads))
    row = lax.broadcasted_iota(jnp.int32, (span, qc), 0)
    dist = lax.broadcasted_iota(jnp.int32, (span, qc), 1) + window - row
    band = (dist >= 0) & (dist < window)
    term = -(slopes[:, None, None] * dist.astype(F32)[None])
    n_pad = window - jnp.arange(n_var, dtype=jnp.int32)[:, None, None, None] * qc
    ok = band[None, None] & (row[None, None] >= n_pad)
    bias = jnp.where(ok, term[None], NEG_INF)
    return bias.transpose(0, 2, 1, 3).reshape(n_var, span, n_heads * qc)


def band_attention(proj3, sinks, mix_in, *, q_off, kv_off, z_off, qc, window):
    bsz, T, _ = proj3.shape
    G = 2
    n_heads = G * HEADS_PER_GROUP
    span = qc + window
    assert window % qc == 0 and T >= span
    has_sinks = sinks is not None
    mix = mix_in is not None
    bias = _band_bias(qc, window, n_heads)
    n_var = bias.shape[0]
    grp = lambda off: [pl.BlockSpec((1, qc, GROUP_WIDTH), lambda b, c, g=g: (b, c, off // GROUP_WIDTH + g))
                       for g in range(G)]
    in_specs = grp(q_off) + [
        pl.BlockSpec((1, T, 2 * LANES), lambda b, c: (b, 0, kv_off // (2 * LANES))),
        pl.BlockSpec((1, span, n_heads * qc), lambda b, c: (jnp.minimum(c, n_var - 1), 0, 0)),
    ]
    args = [proj3, proj3, proj3, bias]
    if has_sinks:
        in_specs.append(pl.BlockSpec((1, n_heads * qc), lambda b, c: (0, 0)))
        args.append(jnp.repeat(sinks.astype(F32), qc).reshape(1, n_heads * qc))
    in_specs += grp(z_off)
    args += [proj3, proj3]
    if mix:
        full = pl.BlockSpec((1, qc, A_WIDTH), lambda b, c: (b, c, 0))
        in_specs += [full, full, pl.BlockSpec((1, qc, LANES), lambda b, c: (b, c, OFF_A_GATE // LANES))]
        args += [mix_in[0], mix_in[1], proj3]
    return pl.pallas_call(
        functools.partial(_band_kernel, qc=qc, window=window, has_sinks=has_sinks, mix=mix),
        grid=(bsz, T // qc),
        in_specs=in_specs,
        out_specs=pl.BlockSpec((1, qc, G * GROUP_WIDTH), lambda b, c: (b, c, 0)),
        out_shape=jax.ShapeDtypeStruct((bsz, T, G * GROUP_WIDTH), BF16),
        scratch_shapes=[
            pltpu.VMEM((G, LANES, HEADS_PER_GROUP * qc), BF16),
            pltpu.VMEM((span, LANES), BF16),
            pltpu.VMEM((G, VAUG_ROWS, span), BF16),
            pltpu.VMEM((span, HEADS_PER_GROUP * qc), BF16),
        ] + [pltpu.VMEM((span, HEADS_PER_GROUP * qc), F32)] * G,
        compiler_params=_cparams(("parallel", "parallel")),
        name="band_attention_w%d" % window,
    )(*args)


def _log_sigmoid(x):
    return jnp.minimum(x, 0.0) - jnp.log(1.0 + jnp.exp(-jnp.abs(x)))


def _mlstm_kernel(q_ref, k_ref, v_ref, og_ref, z_ref, if_ref, ifb_ref,
                  cwq_ref, cwk_ref, cbq_ref, cbk_ref, g_ref, y_ref,
                  xq_ref, xk_ref, c_ref, n_ref, m_ref, ift_ref, *, L):
    @pl.when(pl.program_id(2) == 0)
    def _():
        xq_ref[0:8, :] = jnp.zeros((8, B_HEAD_DIM), F32)
        xk_ref[0:8, :] = jnp.zeros((8, B_HEAD_DIM), F32)
        c_ref[...] = jnp.zeros(c_ref.shape, F32)
        n_ref[...] = jnp.zeros(n_ref.shape, F32)
        m_ref[...] = jnp.zeros(m_ref.shape, F32)

    def conv_silu(x_ref, hist_ref, w_ref, b_ref):
        hist_ref[8:8 + L, :] = x_ref[0]
        xx = hist_ref[...]
        y = b_ref[...] + w_ref[CONV_WIDTH - 1:CONV_WIDTH, :] * xx[8:8 + L, :]
        for back in range(1, CONV_WIDTH):
            tap = CONV_WIDTH - 1 - back
            y = y + w_ref[tap:tap + 1, :] * pltpu.roll(xx, back, axis=0)[8:8 + L, :]
        hist_ref[0:8, :] = xx[L:L + 8, :]
        return _silu(y)

    q = conv_silu(q_ref, xq_ref, cwq_ref, cbq_ref)
    k = conv_silu(k_ref, xk_ref, cwk_ref, cbk_ref) * (B_HEAD_DIM ** -0.5)
    v = v_ref[0]
    head = pl.program_id(1)
    gates = if_ref[0] + ifb_ref[...]
    lane = lax.broadcasted_iota(jnp.int32, (1, LANES), 1)
    i_col = jnp.sum(jnp.where(lane == head, gates, 0.0), axis=1, keepdims=True)
    f_col = jnp.sum(jnp.where(lane == B_HEADS + head, gates, 0.0), axis=1, keepdims=True)
    ift_ref[...] = gates.T
    i_row = ift_ref[pl.ds(head, 1), :]
    f_row = ift_ref[pl.ds(B_HEADS + head, 1), :]
    lf_col = _log_sigmoid(f_col)
    lf_row = _log_sigmoid(f_row)
    r_i = lax.broadcasted_iota(jnp.int32, (L, L), 0)
    c_i = lax.broadcasted_iota(jnp.int32, (L, L), 1)
    causal = c_i <= r_i
    b_col = jnp.sum(jnp.where(causal, lf_row, 0.0), axis=1, keepdims=True)
    b_row = jnp.sum(jnp.where(r_i <= c_i, lf_col, 0.0), axis=0, keepdims=True)
    b_last = jnp.sum(lf_row, axis=1, keepdims=True)
    m_prev = m_ref[...]
    log_d = jnp.where(causal, b_col - b_row + i_row, NEG_INF)
    log_inter = b_col + m_prev
    m_t = jnp.maximum(log_inter, jnp.max(log_d, axis=1, keepdims=True))
    w_intra = jnp.exp(log_d - m_t)
    w_inter = jnp.exp(log_inter - m_t)
    qb = q.astype(BF16)
    vb = v.astype(BF16)
    qk = _dot_nt(qb, k.astype(BF16)) * w_intra
    num = w_inter * _dot(qb, c_ref[...].astype(BF16)) + _dot(qk.astype(BF16), vb)
    den = (w_inter * jnp.sum(q * n_ref[...], axis=1, keepdims=True)
           + jnp.sum(qk, axis=1, keepdims=True))
    h = num / jnp.maximum(jnp.abs(den), jnp.exp(-m_t))
    log_g_row = b_last - b_row + i_row
    m_new = jnp.maximum(b_last + m_prev, jnp.max(log_g_row, axis=1, keepdims=True))
    w_g = jnp.exp(b_last - b_col + i_col - m_new)
    decay = jnp.exp(b_last + m_prev - m_new)
    kw = k * w_g
    c_ref[...] = decay * c_ref[...] + _dot_tn(kw.astype(BF16), vb)
    n_ref[...] = decay * n_ref[...] + jnp.sum(kw, axis=0, keepdims=True)
    m_ref[...] = m_new
    hb = _sigmoid(og_ref[0]) * h
    ms = jnp.mean(hb * hb, axis=-1, keepdims=True)
    hb = hb * lax.rsqrt(ms + EPS) * g_ref[...]
    y_ref[0] = (hb * _silu(z_ref[0])).astype(y_ref.dtype)


def mlstm(proj3, if_bias, conv_w, conv_b, norm_g, *, L):
    bsz, T, _ = proj3.shape
    H = B_HEADS
    D = B_HEAD_DIM

    def col(off, k_half=False):
        base = off // D + (H if k_half else 0)
        return pl.BlockSpec((1, L, D), lambda b, h, c: (b, c, base + h))

    def par(rows, k_half=False):
        base = H if k_half else 0
        return pl.BlockSpec((rows, D), lambda b, h, c: (0, base + h))

    in_specs = [
        col(OFF_B_QK), col(OFF_B_QK, True), col(OFF_B_V), col(OFF_B_O), col(OFF_B_Z),
        pl.BlockSpec((1, L, LANES), lambda b, h, c: (b, c, OFF_B_IF // LANES)),
        pl.BlockSpec((1, LANES), lambda b, h, c: (0, 0)),
        par(CONV_WIDTH), par(CONV_WIDTH, True), par(1), par(1, True),
        pl.BlockSpec((1, D), lambda b, h, c: (0, h)),
    ]
    return pl.pallas_call(
        functools.partial(_mlstm_kernel, L=L),
        grid=(bsz, H, T // L),
        in_specs=in_specs,
        out_specs=pl.BlockSpec((1, L, D), lambda b, h, c: (b, c, h)),
        out_shape=jax.ShapeDtypeStruct((bsz, T, B_WIDTH), BF16),
        scratch_shapes=[
            pltpu.VMEM((L + 8, D), F32), pltpu.VMEM((L + 8, D), F32),
            pltpu.VMEM((D, D), F32), pltpu.VMEM((1, D), F32), pltpu.VMEM((1, 1), F32),
            pltpu.VMEM((LANES, L), F32),
        ],
        compiler_params=_cparams(("parallel", "parallel", "arbitrary")),
        name="mlstm_chunkwise",
    )(proj3, proj3, proj3, proj3, proj3, proj3, if_bias, conv_w, conv_w, conv_b, conv_b, norm_g)


def _alibi_np(n_heads):
    s = (2.0 ** (-8.0 * np.arange(1, n_heads + 1, dtype=np.float64) / n_heads)).astype(np.float32)
    for v in (s, SEL_BLOCK * s):
        assert np.array_equal(v.astype(jnp.bfloat16).astype(np.float32), v), "slopes not bf16-exact"
    return s


def _lane_rep(v, groups, width=LANES):
    return jnp.broadcast_to(v.astype(F32).reshape(groups, -1, 1, 1), (groups, v.shape[0] // groups, 1, width))


def _block_diag2(w):
    z = jnp.zeros_like(w)
    return jnp.concatenate([jnp.concatenate([w, z], axis=-1), jnp.concatenate([z, w], axis=-1)], axis=-2)


def _mixers(x2d, bsz, T, p, layer, cfg):
    n, d = x2d.shape
    G = A_KV_HEADS
    w_packed = repack_w_in(jnp.swapaxes(p["w_in"], 1, 2), layer, tc=cfg["tc_repack"])
    proj = norm_matmul(x2d, p["norm_g"].reshape(1, d), w_packed, tm=cfg["tm_in"], tn=cfg["tn_in"])
    proj3 = proj.reshape(bsz, T, PROJ_COLS)

    nseg = T // CMP_STRIDE
    pe = jnp.stack([p["cmp_pe_k"], p["cmp_pe_v"]])
    pe = jnp.concatenate([pe, pe], axis=-1).reshape(2, CMP_BLOCK, 1, LANES)
    w1 = jnp.stack([p["cmp_w1_k"], p["cmp_w1_v"]]).reshape(2, CMP_BLOCK, HEAD_DIM, HEAD_DIM)
    w2 = jnp.stack([p["cmp_w2_k"], p["cmp_w2_v"]])
    k_cmp, v_cmp_t = nsa_compress(proj3, pe, _block_diag2(w1).astype(BF16), _block_diag2(w2).astype(BF16))

    n_sel = T // SEL_BLOCK
    assert n_sel <= MAX_SEL_BLOCKS
    qc_s, tk_s = cfg["qc_slc"], cfg["tk_slc"]
    ci = np.arange(nseg)[None, :] * CMP_STRIDE
    sj = np.arange(MAX_SEL_BLOCKS)[:, None] * SEL_BLOCK
    overlap_t = jnp.asarray(((ci < sj + SEL_BLOCK) & (ci + CMP_BLOCK > sj)).astype(np.float32), dtype=BF16)
    o_cmp, selbt, kaug, vaugt = cmp_select(proj3, k_cmp, v_cmp_t, overlap_t, qc=cfg["qc_cmp"],
                                           n_top=min(SEL_TOPK, n_sel), tk_slc=tk_s)

    sl_np = _alibi_np(A_HEADS)
    arows = np.zeros((A_HEADS, AUG_DEPTH - LANES - HEAD_DIM, qc_s), np.float32)
    arows[:, 0, :] = sl_np[:, None]
    arows[:, 1, :] = SEL_BLOCK * sl_np[:, None]
    arows = arows.reshape(G, HEADS_PER_GROUP, -1, qc_s).transpose(0, 2, 1, 3).reshape(G, -1, HEADS_PER_GROUP * qc_s)
    o_slc = slc_attention(proj3, selbt, kaug, vaugt, jnp.asarray(arows, dtype=BF16),
                          _lane_rep(jnp.asarray(sl_np), G, qc_s), qc=qc_s, tk=tk_s)

    y_a = band_attention(proj3, None, (o_cmp, o_slc), q_off=OFF_A_Q, kv_off=OFF_A_KV + 4 * LANES, z_off=OFF_A_Z,
                         qc=cfg["qc_win"], window=NSA_WINDOW)

    if_bias = jnp.concatenate([p["mlstm_i_bias"], p["mlstm_f_bias"], jnp.zeros((LANES - N_IF,), F32)]).reshape(1, LANES)
    y_b = mlstm(proj3, if_bias, p["mlstm_conv_w"], p["mlstm_conv_b"].reshape(1, -1),
                p["mlstm_norm_g"].reshape(1, -1), L=cfg["mlstm_chunk"])

    y_c = band_attention(proj3, p["swa_sinks"], None, q_off=OFF_C_Q, kv_off=OFF_C_KV, z_off=OFF_C_Z,
                         qc=cfg["qc_win"], window=SWA_WINDOW)
    return y_a.reshape(n, A_WIDTH), y_b.reshape(n, B_WIDTH), y_c.reshape(n, C_WIDTH), (o_cmp, o_slc, selbt)


def _layer(x2d, bsz, T, p, layer, final_g, cfg):
    n, d = x2d.shape
    y_a, y_b, y_c, _ = _mixers(x2d, bsz, T, p, layer, cfg)
    w_out = p["w_out"].astype(BF16)
    g = final_g if final_g is not None else p["norm_g"]
    return out_proj(y_a, y_b, y_c, x2d,
                    w_out[:A_WIDTH], w_out[A_WIDTH:A_WIDTH + B_WIDTH], w_out[A_WIDTH + B_WIDTH:],
                    g.reshape(1, d), tm=cfg["tm_out"], final_norm=final_g is not None)


def _config(T):
    return dict(tc_repack=256, tm_in=min(1024, T), tn_in=768, tm_out=min(512, T),
                qc_cmp=128, qc_slc=256, tk_slc=512, qc_win=128, mlstm_chunk=min(256, T))


def kernel(x, norm_g, w_in, w_out, cmp_pe_k, cmp_w1_k, cmp_w2_k, cmp_pe_v, cmp_w1_v, cmp_w2_v, mlstm_conv_w, mlstm_conv_b, mlstm_i_bias, mlstm_f_bias, mlstm_norm_g, swa_sinks, final_norm_g):
    bsz, T, d = x.shape
    depth = w_in.shape[0]
    cfg = _config(T)
    x2d = x.reshape(bsz * T, d)
    for l in range(depth):
        p = dict(norm_g=norm_g[l], w_in=w_in, w_out=w_out[l],
                 cmp_pe_k=cmp_pe_k[l], cmp_w1_k=cmp_w1_k[l], cmp_w2_k=cmp_w2_k[l],
                 cmp_pe_v=cmp_pe_v[l], cmp_w1_v=cmp_w1_v[l], cmp_w2_v=cmp_w2_v[l],
                 mlstm_conv_w=mlstm_conv_w[l], mlstm_conv_b=mlstm_conv_b[l],
                 mlstm_i_bias=mlstm_i_bias[l], mlstm_f_bias=mlstm_f_bias[l],
                 mlstm_norm_g=mlstm_norm_g[l], swa_sinks=swa_sinks[l])
        x2d = _layer(x2d, bsz, T, p, l, final_norm_g if l == depth - 1 else None, cfg)
    return x2d.reshape(bsz, T, d)
```

```python
import functools

import numpy as np
import jax
import jax.numpy as jnp
from jax import lax
from jax.experimental import pallas as pl
from jax.experimental.pallas import tpu as pltpu

F32 = jnp.float32
BF16 = jnp.bfloat16

HEAD_DIM = 64
A_HEADS = 8
A_KV_HEADS = 2
HEADS_PER_GROUP = A_HEADS // A_KV_HEADS
GROUP_WIDTH = HEADS_PER_GROUP * HEAD_DIM
A_WIDTH = A_HEADS * HEAD_DIM
C_HEADS = 8
C_KV_HEADS = 2
C_WIDTH = C_HEADS * HEAD_DIM
B_HEADS = 4
B_HEAD_DIM = 256
B_WIDTH = B_HEADS * B_HEAD_DIM
CMP_BLOCK = 32
CMP_STRIDE = 16
SEL_BLOCK = 64
SEL_TOPK = 16
SEL_LOCAL = 2
NSA_WINDOW = 512
SWA_WINDOW = 128
CONV_WIDTH = 4
FORCE_BONUS = 1.0e4
NEG_INF = -1.0e30
EPS = 1.0e-6
SCALE = HEAD_DIM ** -0.5
LOG2E = 1.4426950408889634
Q_SCALE = SCALE * LOG2E
LANES = 128
MAX_SEL_BLOCKS = LANES
AUG_DEPTH = 256
VAUG_ROWS = 80

IN_SPLITS = (A_WIDTH, 128, 128, 128, 128, 128, 128, A_HEADS * 3, A_WIDTH,
             2 * B_WIDTH, B_WIDTH, B_HEADS, B_HEADS, B_WIDTH, B_WIDTH,
             C_WIDTH, 128, 128, C_WIDTH)
IN_COLS = sum(IN_SPLITS)
_SRC = np.concatenate([[0], np.cumsum(IN_SPLITS)]).astype(int)
SRC_GATE, SRC_A_Z, SRC_B_IF, SRC_B_O = int(_SRC[7]), int(_SRC[8]), int(_SRC[11]), int(_SRC[13])
N_GATE = A_HEADS * 3
N_IF = 2 * B_HEADS
OFF_A_Q = 0
OFF_A_KV = 512
OFF_A_Z = 1280
OFF_B_QK = 1792
OFF_B_V = 3840
OFF_B_O = 4864
OFF_B_Z = 5888
OFF_C_Q = 6912
OFF_C_KV = 7424
OFF_C_Z = 7680
OFF_A_GATE = 8192
OFF_B_IF = 8320
PROJ_COLS = 8448
W_RUNS = ((0, OFF_A_Q, SRC_GATE), (SRC_A_Z, OFF_A_Z, SRC_B_IF - SRC_A_Z), (SRC_B_O, OFF_B_O, IN_COLS - SRC_B_O))
assert OFF_A_Z + (SRC_B_IF - SRC_A_Z) == OFF_B_O and OFF_B_O + (IN_COLS - SRC_B_O) == OFF_A_GATE

VMEM_LIMIT = 48 * 1024 * 1024


def _cparams(sem):
    return pltpu.CompilerParams(dimension_semantics=sem, vmem_limit_bytes=VMEM_LIMIT)


def _dot(a, b):
    return jnp.dot(a, b, preferred_element_type=F32)


def _dot_nt(a, b):
    return lax.dot_general(a, b, (((1,), (1,)), ((), ())), preferred_element_type=F32)


def _dot_tn(a, b):
    return lax.dot_general(a, b, (((0,), (0,)), ((), ())), preferred_element_type=F32)


def _sigmoid(x):
    return 1.0 / (1.0 + jnp.exp(-x))


def _silu(x):
    return x * _sigmoid(x)


def _heads_to_rows(ot, qc):
    return jnp.concatenate([ot[:, hh * qc:(hh + 1) * qc] for hh in range(HEADS_PER_GROUP)], axis=0).T


def _repack_kernel(wt_ref, o_ref):
    for src, dst, width in W_RUNS:
        o_ref[dst:dst + width, :] = wt_ref[0, src:src + width, :].astype(BF16)
    o_ref[OFF_A_GATE:PROJ_COLS, :] = jnp.zeros((PROJ_COLS - OFF_A_GATE, o_ref.shape[1]), BF16)
    o_ref[OFF_A_GATE:OFF_A_GATE + N_GATE, :] = wt_ref[0, SRC_GATE:SRC_GATE + N_GATE, :].astype(BF16)
    o_ref[OFF_B_IF:OFF_B_IF + N_IF, :] = wt_ref[0, SRC_B_IF:SRC_B_IF + N_IF, :].astype(BF16)


def repack_w_in(w_in_t, layer, *, tc):
    d = w_in_t.shape[2]
    return pl.pallas_call(
        _repack_kernel,
        grid=(d // tc,),
        in_specs=[pl.BlockSpec((1, IN_COLS, tc), lambda i: (layer, 0, i))],
        out_specs=pl.BlockSpec((PROJ_COLS, tc), lambda i: (0, i)),
        out_shape=jax.ShapeDtypeStruct((PROJ_COLS, d), BF16),
        compiler_params=_cparams(("parallel",)),
        name="repack_w_in",
    )(w_in_t)


def _norm_matmul_kernel(x_ref, g_ref, wt_ref, o_ref, h_ref):
    @pl.when(pl.program_id(1) == 0)
    def _():
        x = x_ref[...]
        ms = jnp.mean(x * x, axis=-1, keepdims=True)
        h_ref[...] = (x * lax.rsqrt(ms + EPS) * g_ref[...]).astype(BF16)

    o_ref[...] = _dot_nt(h_ref[...], wt_ref[...])


def norm_matmul(x, g, wt, *, tm, tn):
    n, d = x.shape
    cols = wt.shape[0]
    return pl.pallas_call(
        _norm_matmul_kernel,
        grid=(n // tm, cols // tn),
        in_specs=[
            pl.BlockSpec((tm, d), lambda i, j: (i, 0)),
            pl.BlockSpec((1, d), lambda i, j: (0, 0)),
            pl.BlockSpec((tn, d), lambda i, j: (j, 0)),
        ],
        out_specs=pl.BlockSpec((tm, tn), lambda i, j: (i, j)),
        out_shape=jax.ShapeDtypeStruct((n, cols), F32),
        scratch_shapes=[pltpu.VMEM((tm, d), BF16)],
        compiler_params=_cparams(("parallel", "arbitrary")),
        name="norm_in_proj",
    )(x, g, wt)


def _out_proj_kernel(ya_ref, yb_ref, yc_ref, x_ref, wa_ref, wb_ref, wc_ref, g_ref, o_ref, *, final_norm):
    acc = _dot(ya_ref[...], wa_ref[...]) + _dot(yb_ref[...], wb_ref[...]) + _dot(yc_ref[...], wc_ref[...])
    y = x_ref[...] + acc
    if final_norm:
        ms = jnp.mean(y * y, axis=-1, keepdims=True)
        y = y * lax.rsqrt(ms + EPS) * g_ref[...]
    o_ref[...] = y


def out_proj(ya, yb, yc, x, wa, wb, wc, g, *, tm, final_norm):
    n, d = x.shape
    row = lambda w: pl.BlockSpec((tm, w), lambda i: (i, 0))
    full = lambda a: pl.BlockSpec(a.shape, lambda i: (0, 0))
    return pl.pallas_call(
        functools.partial(_out_proj_kernel, final_norm=final_norm),
        grid=(n // tm,),
        in_specs=[row(ya.shape[1]), row(yb.shape[1]), row(yc.shape[1]), row(d),
                  full(wa), full(wb), full(wc), full(g)],
        out_specs=row(d),
        out_shape=jax.ShapeDtypeStruct((n, d), F32),
        compiler_params=_cparams(("parallel",)),
        name="out_proj",
    )(ya, yb, yc, x, wa, wb, wc, g)


def _compress_kernel(k_ref, v_ref, pe_ref, w1_ref, w2_ref, kc_ref, vct_ref, shift_ref):
    nseg = k_ref.shape[1] // CMP_STRIDE
    for s, src_ref in enumerate((k_ref, v_ref)):
        u1 = jnp.zeros((nseg, LANES), F32)
        u2 = jnp.zeros((nseg, LANES), F32)
        for l in range(CMP_STRIDE):
            x = src_ref[0, pl.ds(l, nseg, stride=CMP_STRIDE), :]
            u1 = u1 + _dot((x + pe_ref[s, l]).astype(BF16), w1_ref[s, l])
            u2 = u2 + _dot((x + pe_ref[s, CMP_STRIDE + l]).astype(BF16), w1_ref[s, CMP_STRIDE + l])
        shift_ref[0:nseg, :] = u2
        shift_ref[nseg:nseg + 8, :] = jnp.zeros((8, LANES), F32)
        pre = u1 + shift_ref[1:nseg + 1, :]
        out = _dot(_silu(pre).astype(BF16), w2_ref[s])
        if s == 0:
            kc_ref[0] = out.astype(BF16)
        else:
            vct_ref[0] = out.T.astype(BF16)


def nsa_compress(proj3, pe, w1, w2):
    bsz, T, _ = proj3.shape
    nseg = T // CMP_STRIDE
    full = lambda a: pl.BlockSpec(a.shape, lambda b: (0,) * a.ndim)
    return pl.pallas_call(
        _compress_kernel,
        grid=(bsz,),
        in_specs=[pl.BlockSpec((1, T, LANES), lambda b: (b, 0, OFF_A_KV // LANES)),
                  pl.BlockSpec((1, T, LANES), lambda b: (b, 0, OFF_A_KV // LANES + 1)),
                  full(pe), full(w1), full(w2)],
        out_specs=[pl.BlockSpec((1, nseg, LANES), lambda b: (b, 0, 0)),
                   pl.BlockSpec((1, LANES, nseg), lambda b: (b, 0, 0))],
        out_shape=[jax.ShapeDtypeStruct((bsz, nseg, LANES), BF16),
                   jax.ShapeDtypeStruct((bsz, LANES, nseg), BF16)],
        scratch_shapes=[pltpu.VMEM((nseg + 8, LANES), F32)],
        compiler_params=_cparams(("parallel",)),
        name="nsa_compress",
    )(proj3, proj3, pe, w1, w2)


def _cmp_select_kernel(q_ref, kc_ref, vct_ref, ovt_ref, ksvs_ref, o_ref, selbt_ref, kaug_ref, vaugt_ref,
                       w_ref, s_ref, p_ref, *, qc, n_top, tk_slc):
    ncmp = kc_ref.shape[1]
    t0 = pl.program_id(1) * qc
    t_row = t0 + lax.broadcasted_iota(jnp.int32, (1, qc), 1)
    cmp_i = lax.broadcasted_iota(jnp.int32, (ncmp, 1), 0)
    vis = (cmp_i * CMP_STRIDE + (CMP_BLOCK - 1)) <= t_row
    any_vis = (t_row >= CMP_BLOCK - 1).astype(F32)
    cur = jnp.right_shift(t_row, 6)
    blk = lax.broadcasted_iota(jnp.int32, (MAX_SEL_BLOCKS, qc), 0)
    valid = blk <= cur
    forced = (blk == 0) | (valid & (blk > cur - SEL_LOCAL))
    qt = (q_ref[0] * Q_SCALE).T.astype(BF16)
    zero = jnp.zeros((HEAD_DIM, qc), BF16)
    for h in range(A_HEADS):
        g, hh = divmod(h, HEADS_PER_GROUP)
        cols = slice(hh * qc, (hh + 1) * qc)
        w_ref[g, g * HEAD_DIM:(g + 1) * HEAD_DIM, cols] = qt[h * HEAD_DIM:(h + 1) * HEAD_DIM, :]
        w_ref[g, (1 - g) * HEAD_DIM:(2 - g) * HEAD_DIM, cols] = zero
    for g in range(A_KV_HEADS):
        s_ref[g] = _dot(kc_ref[0], w_ref[g])
    ovt = ovt_ref[...]
    for g in range(A_KV_HEADS):
        psum = jnp.zeros((ncmp, qc), F32)
        for hh in range(HEADS_PER_GROUP):
            cols = slice(hh * qc, (hh + 1) * qc)
            s = jnp.where(vis, s_ref[g, :, cols], NEG_INF)
            e = jnp.exp2(s - jnp.max(s, axis=0, keepdims=True))
            p = e * (any_vis / jnp.sum(e, axis=0, keepdims=True))
            p_ref[g, :, cols] = p.astype(BF16)
            psum = psum + p
        ot = _dot(vct_ref[0], p_ref[g])
        o_ref[0, :, g * GROUP_WIDTH:(g + 1) * GROUP_WIDTH] = _heads_to_rows(ot[g * HEAD_DIM:(g + 1) * HEAD_DIM, :], qc)
        p1 = psum.astype(BF16)
        r1 = psum - p1.astype(F32)
        p2 = r1.astype(BF16)
        p3 = (r1 - p2.astype(F32)).astype(BF16)
        imp = _dot(ovt, p1) + _dot(ovt, p2) + _dot(ovt, p3)
        val = jnp.where(forced, -2.0, jnp.where(valid, imp, -1.0))
        for _ in range(n_top - (SEL_LOCAL + 1)):
            mx = jnp.max(val, axis=0, keepdims=True)
            first = jnp.min(jnp.where(val == mx, blk, MAX_SEL_BLOCKS), axis=0, keepdims=True)
            val = jnp.where(blk == first, -2.0, val)
        chosen = (val == -2.0) & (blk < cur)
        selbt_ref[0, g] = jnp.where(chosen, 0.0, NEG_INF).astype(BF16)

    kv = ksvs_ref[0]
    pos = t0 + lax.broadcasted_iota(jnp.int32, (qc, 1), 0)
    lane = lax.broadcasted_iota(jnp.int32, (1, LANES), 1)
    onehot = jnp.where(jnp.right_shift(pos, 6) == lane, 1.0, 0.0).astype(BF16)
    flane = lax.broadcasted_iota(jnp.int32, (1, AUG_DEPTH - LANES - HEAD_DIM), 1)
    in_block = jnp.bitwise_and(pos, SEL_BLOCK - 1).astype(F32)
    in_tile = jnp.bitwise_and(jnp.right_shift(pos, 6), tk_slc // SEL_BLOCK - 1).astype(F32)
    feat = jnp.where(flane < N_SPLIT, in_block, jnp.where(flane < 2 * N_SPLIT, in_tile, 0.0)).astype(BF16)
    vt = kv[:, LANES:2 * LANES].T.astype(BF16)
    ones_row = (lax.broadcasted_iota(jnp.int32, (VAUG_ROWS - HEAD_DIM, qc), 0) == 0).astype(BF16)
    for g in range(A_KV_HEADS):
        kaug_ref[0, g, :, 0:LANES] = onehot
        kaug_ref[0, g, :, LANES:LANES + HEAD_DIM] = kv[:, g * HEAD_DIM:(g + 1) * HEAD_DIM].astype(BF16)
        kaug_ref[0, g, :, LANES + HEAD_DIM:AUG_DEPTH] = feat
        vaugt_ref[0, g, 0:HEAD_DIM, :] = vt[g * HEAD_DIM:(g + 1) * HEAD_DIM, :]
        vaugt_ref[0, g, HEAD_DIM:VAUG_ROWS, :] = ones_row


def cmp_select(proj3, kcmp, vcmp_t, overlap_t, *, qc, n_top, tk_slc):
    bsz, T, _ = proj3.shape
    ncmp = kcmp.shape[1]
    G = A_KV_HEADS
    assert n_top > SEL_LOCAL + 1 and (tk_slc // SEL_BLOCK) & (tk_slc // SEL_BLOCK - 1) == 0
    return pl.pallas_call(
        functools.partial(_cmp_select_kernel, qc=qc, n_top=n_top, tk_slc=tk_slc),
        grid=(bsz, T // qc),
        in_specs=[
            pl.BlockSpec((1, qc, A_WIDTH), lambda b, c: (b, c, OFF_A_Q // A_WIDTH)),
            pl.BlockSpec((1, ncmp, LANES), lambda b, c: (b, 0, 0)),
            pl.BlockSpec((1, LANES, ncmp), lambda b, c: (b, 0, 0)),
            pl.BlockSpec((MAX_SEL_BLOCKS, ncmp), lambda b, c: (0, 0)),
            pl.BlockSpec((1, qc, 2 * LANES), lambda b, c: (b, c, (OFF_A_KV + 2 * LANES) // (2 * LANES))),
        ],
        out_specs=[
            pl.BlockSpec((1, qc, A_WIDTH), lambda b, c: (b, c, 0)),
            pl.BlockSpec((1, G, MAX_SEL_BLOCKS, qc), lambda b, c: (b, 0, 0, c)),
            pl.BlockSpec((1, G, qc, AUG_DEPTH), lambda b, c: (b, 0, c, 0)),
            pl.BlockSpec((1, G, VAUG_ROWS, qc), lambda b, c: (b, 0, 0, c)),
        ],
        out_shape=[jax.ShapeDtypeStruct((bsz, T, A_WIDTH), F32),
                   jax.ShapeDtypeStruct((bsz, G, MAX_SEL_BLOCKS, T), BF16),
                   jax.ShapeDtypeStruct((bsz, G, T, AUG_DEPTH), BF16),
                   jax.ShapeDtypeStruct((bsz, G, VAUG_ROWS, T), BF16)],
        scratch_shapes=[
            pltpu.VMEM((G, LANES, HEADS_PER_GROUP * qc), BF16),
            pltpu.VMEM((G, ncmp, HEADS_PER_GROUP * qc), F32),
            pltpu.VMEM((G, ncmp, HEADS_PER_GROUP * qc), BF16),
        ],
        compiler_params=_cparams(("parallel", "parallel")),
        name="nsa_cmp_select",
    )(proj3, kcmp, vcmp_t, overlap_t, proj3)


def _slc_kernel(q_ref, selbt_ref, kaug_ref, vaugt_ref, arow_ref, slope_ref, dbias_ref, o_ref,
                qaug_ref, m_ref, acc_ref, *s_refs, qc, tk):
    t0 = pl.multiple_of(pl.program_id(2) * qc, qc)
    selbt = selbt_ref[0, 0]
    qt = (q_ref[0] * Q_SCALE).T.astype(BF16)
    for hh in range(HEADS_PER_GROUP):
        cols = slice(hh * qc, (hh + 1) * qc)
        qaug_ref[0:LANES, cols] = selbt
        qaug_ref[LANES:LANES + HEAD_DIM, cols] = qt[hh * HEAD_DIM:(hh + 1) * HEAD_DIM, :]
    qaug_ref[LANES + HEAD_DIM:AUG_DEPTH, :] = arow_ref[0]
    m_ref[...] = jnp.full(m_ref.shape, NEG_INF, F32)
    acc_ref[...] = jnp.zeros(acc_ref.shape, F32)

    bufs = (s_refs[:HEADS_PER_GROUP], s_refs[HEADS_PER_GROUP:])

    def scores(kt, hh, dst):
        s0 = pl.multiple_of(kt * tk, tk)
        dst[hh][...] = _dot(kaug_ref[0, 0, pl.ds(s0, tk), :], qaug_ref[:, hh * qc:(hh + 1) * qc])

    def diag_scores(hh, dst):
        ka = kaug_ref[0, 0, pl.ds(t0, qc), LANES:AUG_DEPTH]
        st = _dot(ka, qaug_ref[LANES:AUG_DEPTH, hh * qc:(hh + 1) * qc])
        dst[hh][0:qc, :] = st + dbias_ref[...]

    def softmax_pv(hh, src, rows, vt, off):
        c = slope_ref[0, hh] * off
        m_old = m_ref[hh]
        m_new = jnp.maximum(m_old, jnp.max(src[hh][0:rows, :], axis=0, keepdims=True) + c)
        alpha = jnp.exp2(m_old - m_new)
        p = jnp.exp2(src[hh][0:rows, :] - (m_new - c)).astype(BF16)
        acc_ref[hh] = alpha * acc_ref[hh] + _dot(vt, p)
        m_ref[hh] = m_new

    def step(kt, src, next_scores):
        s0 = pl.multiple_of(kt * tk, tk)
        vt = vaugt_ref[0, 0, :, pl.ds(s0, tk)]
        off = (s0 - t0).astype(F32)
        for hh in range(HEADS_PER_GROUP):
            next_scores(hh)
            softmax_pv(hh, src, tk, vt, off)

    def main_step(kt, src, dst):
        step(kt, src, lambda hh: scores(kt + 1, hh, dst))

    n_full = lax.div(t0, tk)
    odd = lax.rem(n_full, 2)

    @pl.when(odd == 0)
    def _():
        for hh in range(HEADS_PER_GROUP):
            scores(0, hh, bufs[0])

    @pl.when(odd == 1)
    def _():
        for hh in range(HEADS_PER_GROUP):
            scores(0, hh, bufs[1])
        main_step(0, bufs[1], bufs[0])

    def body(i, carry):
        kt = odd + 2 * i
        main_step(kt, bufs[0], bufs[1])
        main_step(kt + 1, bufs[1], bufs[0])
        return carry

    lax.fori_loop(0, lax.div(n_full, 2), body, 0)
    step(n_full, bufs[0], lambda hh: diag_scores(hh, bufs[1]))
    base = pl.multiple_of(n_full * tk, tk)
    vt_d = vaugt_ref[0, 0, :, pl.ds(t0, qc)]
    for hh in range(HEADS_PER_GROUP):
        softmax_pv(hh, bufs[1], qc, vt_d, (base - t0).astype(F32))
    ot = jnp.concatenate([acc_ref[hh, 0:HEAD_DIM, :] / acc_ref[hh, HEAD_DIM:HEAD_DIM + 1, :]
                          for hh in range(HEADS_PER_GROUP)], axis=0)
    o_ref[0] = ot.T


def slc_attention(proj3, selbt, kaug, vaugt, arows, slopes, *, qc, tk):
    bsz, T, _ = proj3.shape
    G = A_KV_HEADS
    assert tk % qc == 0 and qc % SEL_BLOCK == 0
    ki = np.arange(qc)[:, None]
    qi = np.arange(qc)[None, :]
    dbias = jnp.asarray(np.where((ki // SEL_BLOCK == qi // SEL_BLOCK) & (ki <= qi), 0.0, NEG_INF), dtype=F32)
    return pl.pallas_call(
        functools.partial(_slc_kernel, qc=qc, tk=tk),
        grid=(bsz, G, T // qc),
        in_specs=[
            pl.BlockSpec((1, qc, GROUP_WIDTH), lambda b, g, c: (b, c, OFF_A_Q // GROUP_WIDTH + g)),
            pl.BlockSpec((1, 1, LANES, qc), lambda b, g, c: (b, g, 0, c)),
            pl.BlockSpec((1, 1, T, AUG_DEPTH), lambda b, g, c: (b, g, 0, 0)),
            pl.BlockSpec((1, 1, VAUG_ROWS, T), lambda b, g, c: (b, g, 0, 0)),
            pl.BlockSpec((1, AUG_DEPTH - LANES - HEAD_DIM, HEADS_PER_GROUP * qc), lambda b, g, c: (g, 0, 0)),
            pl.BlockSpec((1, HEADS_PER_GROUP, 1, qc), lambda b, g, c: (g, 0, 0, 0)),
            pl.BlockSpec((qc, qc), lambda b, g, c: (0, 0)),
        ],
        out_specs=pl.BlockSpec((1, qc, GROUP_WIDTH), lambda b, g, c: (b, c, g)),
        out_shape=jax.ShapeDtypeStruct((bsz, T, A_WIDTH), F32),
        scratch_shapes=[
            pltpu.VMEM((AUG_DEPTH, HEADS_PER_GROUP * qc), BF16),
            pltpu.VMEM((HEADS_PER_GROUP, 1, qc), F32),
            pltpu.VMEM((HEADS_PER_GROUP, VAUG_ROWS, qc), F32),
        ] + [pltpu.VMEM((tk, qc), F32)] * (2 * HEADS_PER_GROUP),
        compiler_params=_cparams(("parallel", "parallel", "arbitrary")),
        name="nsa_slc_attention",
    )(proj3, selbt, kaug, vaugt, arows, slopes, dbias)


def _band_kernel(*refs, qc, window, has_sinks, mix):
    q_refs, kv_ref, bias_ref = refs[0:2], refs[2], refs[3]
    pos = 4
    sink_ref = None
    if has_sinks:
        sink_ref = refs[pos]
        pos += 1
    z_refs = refs[pos:pos + 2]
    pos += 2
    if mix:
        ocmp_ref, oslc_ref, gate_ref = refs[pos:pos + 3]
        pos += 3
    o_ref, w_ref, kwin_ref, vt_ref, p_ref = refs[pos:pos + 5]
    s_refs = refs[pos + 5:pos + 7]
    G = len(q_refs)
    span = qc + window
    lanes = HEADS_PER_GROUP * qc
    n_pad_chunks = window // qc
    c = pl.program_id(1)
    ones_row = (lax.broadcasted_iota(jnp.int32, (VAUG_ROWS - HEAD_DIM, span), 0) == 0).astype(BF16)

    def assemble(kv):
        kwin_ref[...] = kv[:, 0:LANES].astype(BF16)
        vt = kv[:, LANES:2 * LANES].T.astype(BF16)
        for g in range(G):
            vt_ref[g, 0:HEAD_DIM, :] = vt[g * HEAD_DIM:(g + 1) * HEAD_DIM, :]
            vt_ref[g, HEAD_DIM:VAUG_ROWS, :] = ones_row

    @pl.when(c >= n_pad_chunks)
    def _():
        start = pl.multiple_of(c * qc - window, qc)
        assemble(kv_ref[0, pl.ds(start, span), :])

    for j in range(n_pad_chunks):
        @pl.when(c == j)
        def _(j=j):
            n_pad = window - j * qc
            assemble(jnp.concatenate([jnp.zeros((n_pad, 2 * LANES), F32), kv_ref[0, 0:span - n_pad, :]], axis=0))

    zero = jnp.zeros((HEAD_DIM, qc), BF16)
    for g in range(G):
        qt = (q_refs[g][0] * Q_SCALE).T.astype(BF16)
        for hh in range(HEADS_PER_GROUP):
            cols = slice(hh * qc, (hh + 1) * qc)
            w_ref[g, g * HEAD_DIM:(g + 1) * HEAD_DIM, cols] = qt[hh * HEAD_DIM:(hh + 1) * HEAD_DIM, :]
            w_ref[g, (1 - g) * HEAD_DIM:(2 - g) * HEAD_DIM, cols] = zero
    pieces = [slice(r, r + LANES) for r in range(0, span, LANES)]
    for g in range(G):
        for rows in pieces:
            s_refs[g][rows, :] = _dot(kwin_ref[rows, :], w_ref[g])
    if mix:
        gate = _sigmoid(gate_ref[0])
    for g in range(G):
        mx8 = None
        for rows in pieces:
            t = s_refs[g][rows, :] + bias_ref[0, rows, g * lanes:(g + 1) * lanes]
            s_refs[g][rows, :] = t
            m8 = jnp.max(t.reshape(LANES // 8, 8, lanes), axis=0)
            mx8 = m8 if mx8 is None else jnp.maximum(mx8, m8)
        mx = jnp.max(mx8, axis=0, keepdims=True)
        if has_sinks:
            sk = sink_ref[:, g * lanes:(g + 1) * lanes]
            mx = jnp.maximum(mx, sk)
        for rows in pieces:
            p_ref[rows, :] = jnp.exp2(s_refs[g][rows, :] - mx).astype(BF16)
        ot = _dot(vt_ref[g], p_ref[...])
        den = ot[HEAD_DIM:HEAD_DIM + 1, :]
        if has_sinks:
            den = den + jnp.exp2(sk - mx)
        o = _heads_to_rows(ot[0:HEAD_DIM, :] / den, qc)
        z = z_refs[g][0]
        for hh in range(HEADS_PER_GROUP):
            h = g * HEADS_PER_GROUP + hh
            loc = slice(hh * HEAD_DIM, (hh + 1) * HEAD_DIM)
            glob = slice(h * HEAD_DIM, (h + 1) * HEAD_DIM)
            oh = o[:, loc]
            if mix:
                oh = (gate[:, 3 * h:3 * h + 1] * ocmp_ref[0, :, glob]
                      + gate[:, 3 * h + 1:3 * h + 2] * oslc_ref[0, :, glob]
                      + gate[:, 3 * h + 2:3 * h + 3] * oh)
            o_ref[0, :, glob] = (oh * _silu(z[:, loc])).astype(o_ref.dtype)


def _band_bias(qc, window, n_heads):
    span = qc + window
    n_var = window // qc + 1
    slopes = jnp.asarray(_alibi_np(n_heads)) * LOG2E
    row = lax.broadcasted_iota(jnp.int32, (span, qc), 0)
    dist = lax.broadcasted_iota(jnp.int32, (span, qc), 1) + window - row
    band = (dist >= 0) & (dist < window)
    term = -(slopes[:, None, None] * dist.astype(F32)[None])
    n_pad = window - jnp.arange(n_var, dtype=jnp.int32)[:, None, None, None] * qc
    ok = band[None, None] & (row[None, None] >= n_pad)
    bias = jnp.where(ok, term[None], NEG_INF)
    return bias.transpose(0, 2, 1, 3).reshape(n_var, span, n_heads * qc)


def band_attention(proj3, sinks, mix_in, *, q_off, kv_off, z_off, qc, window):
    bsz, T, _ = proj3.shape
    G = 2
    n_heads = G * HEADS_PER_GROUP
    span = qc + window
    assert window % qc == 0 and T >= span
    has_sinks = sinks is not None
    mix = mix_in is not None
    bias = _band_bias(qc, window, n_heads)
    n_var = bias.shape[0]
    grp = lambda off: [pl.BlockSpec((1, qc, GROUP_WIDTH), lambda b, c, g=g: (b, c, off // GROUP_WIDTH + g))
                       for g in range(G)]
    in_specs = grp(q_off) + [
        pl.BlockSpec((1, T, 2 * LANES), lambda b, c: (b, 0, kv_off // (2 * LANES))),
        pl.BlockSpec((1, span, n_heads * qc), lambda b, c: (jnp.minimum(c, n_var - 1), 0, 0)),
    ]
    args = [proj3, proj3, proj3, bias]
    if has_sinks:
        in_specs.append(pl.BlockSpec((1, n_heads * qc), lambda b, c: (0, 0)))
        args.append(jnp.repeat(sinks.astype(F32) * LOG2E, qc).reshape(1, n_heads * qc))
    in_specs += grp(z_off)
    args += [proj3, proj3]
    if mix:
        full = pl.BlockSpec((1, qc, A_WIDTH), lambda b, c: (b, c, 0))
        in_specs += [full, full, pl.BlockSpec((1, qc, LANES), lambda b, c: (b, c, OFF_A_GATE // LANES))]
        args += [mix_in[0], mix_in[1], proj3]
    return pl.pallas_call(
        functools.partial(_band_kernel, qc=qc, window=window, has_sinks=has_sinks, mix=mix),
        grid=(bsz, T // qc),
        in_specs=in_specs,
        out_specs=pl.BlockSpec((1, qc, G * GROUP_WIDTH), lambda b, c: (b, c, 0)),
        out_shape=jax.ShapeDtypeStruct((bsz, T, G * GROUP_WIDTH), BF16),
        scratch_shapes=[
            pltpu.VMEM((G, LANES, HEADS_PER_GROUP * qc), BF16),
            pltpu.VMEM((span, LANES), BF16),
            pltpu.VMEM((G, VAUG_ROWS, span), BF16),
            pltpu.VMEM((span, HEADS_PER_GROUP * qc), BF16),
        ] + [pltpu.VMEM((span, HEADS_PER_GROUP * qc), F32)] * G,
        compiler_params=_cparams(("parallel", "parallel")),
        name="band_attention_w%d" % window,
    )(*args)


def _log_sigmoid(x):
    return jnp.minimum(x, 0.0) - jnp.log(1.0 + jnp.exp(-jnp.abs(x)))


def _mlstm_kernel(*refs, L, hps):
    per_head_in = [refs[5 * j:5 * j + 5] for j in range(hps)]
    if_ref, ifb_ref = refs[5 * hps:5 * hps + 2]
    base = 5 * hps + 2
    per_head_par = [refs[base + 5 * j:base + 5 * j + 5] for j in range(hps)]
    y_ref, xq_ref, xk_ref, c_ref, n_ref, m_ref, ift_ref = refs[base + 5 * hps:]

    @pl.when(pl.program_id(2) == 0)
    def _():
        xq_ref[:, 0:8, :] = jnp.zeros((hps, 8, B_HEAD_DIM), F32)
        xk_ref[:, 0:8, :] = jnp.zeros((hps, 8, B_HEAD_DIM), F32)
        c_ref[...] = jnp.zeros(c_ref.shape, F32)
        n_ref[...] = jnp.zeros(n_ref.shape, F32)
        m_ref[...] = jnp.zeros(m_ref.shape, F32)

    def conv_silu(x_ref, hist_ref, j, w_ref, b_ref):
        hist_ref[j, 8:8 + L, :] = x_ref[0]
        xx = hist_ref[j]
        y = b_ref[...] + w_ref[CONV_WIDTH - 1:CONV_WIDTH, :] * xx[8:8 + L, :]
        for back in range(1, CONV_WIDTH):
            tap = CONV_WIDTH - 1 - back
            y = y + w_ref[tap:tap + 1, :] * pltpu.roll(xx, back, axis=0)[8:8 + L, :]
        hist_ref[j, 0:8, :] = xx[L:L + 8, :]
        return _silu(y)

    gates = if_ref[0] + ifb_ref[...]
    ift_ref[...] = gates.T
    lane = lax.broadcasted_iota(jnp.int32, (1, LANES), 1)
    r_i = lax.broadcasted_iota(jnp.int32, (L, L), 0)
    c_i = lax.broadcasted_iota(jnp.int32, (L, L), 1)
    causal = c_i <= r_i

    for j in range(hps):
        q_ref, k_ref, v_ref, og_ref, z_ref = per_head_in[j]
        cwq_ref, cwk_ref, cbq_ref, cbk_ref, g_ref = per_head_par[j]
        head = pl.program_id(1) * hps + j
        q = conv_silu(q_ref, xq_ref, j, cwq_ref, cbq_ref)
        k = conv_silu(k_ref, xk_ref, j, cwk_ref, cbk_ref) * (B_HEAD_DIM ** -0.5)
        v = v_ref[0]
        i_col = jnp.sum(jnp.where(lane == head, gates, 0.0), axis=1, keepdims=True)
        f_col = jnp.sum(jnp.where(lane == B_HEADS + head, gates, 0.0), axis=1, keepdims=True)
        i_row = ift_ref[pl.ds(head, 1), :]
        f_row = ift_ref[pl.ds(B_HEADS + head, 1), :]
        lf_col = _log_sigmoid(f_col)
        lf_row = _log_sigmoid(f_row)
        b_col = jnp.sum(jnp.where(causal, lf_row, 0.0), axis=1, keepdims=True)
        b_row = jnp.sum(jnp.where(r_i <= c_i, lf_col, 0.0), axis=0, keepdims=True)
        b_last = jnp.sum(lf_row, axis=1, keepdims=True)
        m_prev = m_ref[j]
        log_d = jnp.where(causal, b_col - b_row + i_row, NEG_INF)
        log_inter = b_col + m_prev
        m_t = jnp.maximum(log_inter, jnp.max(log_d, axis=1, keepdims=True))
        w_intra = jnp.exp(log_d - m_t)
        w_inter = jnp.exp(log_inter - m_t)
        qb = q.astype(BF16)
        vb = v.astype(BF16)
        qk = _dot_nt(qb, k.astype(BF16)) * w_intra
        num = w_inter * _dot(qb, c_ref[j].astype(BF16)) + _dot(qk.astype(BF16), vb)
        den = (w_inter * jnp.sum(q * n_ref[j], axis=1, keepdims=True)
               + jnp.sum(qk, axis=1, keepdims=True))
        h = num / jnp.maximum(jnp.abs(den), jnp.exp(-m_t))
        log_g_row = b_last - b_row + i_row
        m_new = jnp.maximum(b_last + m_prev, jnp.max(log_g_row, axis=1, keepdims=True))
        w_g = jnp.exp(b_last - b_col + i_col - m_new)
        decay = jnp.exp(b_last + m_prev - m_new)
        kw = k * w_g
        c_ref[j] = decay * c_ref[j] + _dot_tn(kw.astype(BF16), vb)
        n_ref[j] = decay * n_ref[j] + jnp.sum(kw, axis=0, keepdims=True)
        m_ref[j] = m_new
        hb = _sigmoid(og_ref[0]) * h
        ms = jnp.mean(hb * hb, axis=-1, keepdims=True)
        hb = hb * lax.rsqrt(ms + EPS) * g_ref[...]
        y_ref[0, :, j * B_HEAD_DIM:(j + 1) * B_HEAD_DIM] = (hb * _silu(z_ref[0])).astype(y_ref.dtype)


def mlstm(proj3, if_bias, conv_w, conv_b, norm_g, *, L, hps):
    bsz, T, _ = proj3.shape
    H = B_HEADS
    D = B_HEAD_DIM
    assert H % hps == 0

    def col(off, j, k_half=False):
        base = off // D + (H if k_half else 0) + j
        return pl.BlockSpec((1, L, D), lambda b, h, c: (b, c, base + h * hps))

    def par(rows, j, k_half=False):
        base = (H if k_half else 0) + j
        return pl.BlockSpec((rows, D), lambda b, h, c: (0, base + h * hps))

    in_specs, args = [], []
    for j in range(hps):
        in_specs += [col(OFF_B_QK, j), col(OFF_B_QK, j, True), col(OFF_B_V, j), col(OFF_B_O, j), col(OFF_B_Z, j)]
        args += [proj3] * 5
    in_specs += [pl.BlockSpec((1, L, LANES), lambda b, h, c: (b, c, OFF_B_IF // LANES)),
                 pl.BlockSpec((1, LANES), lambda b, h, c: (0, 0))]
    args += [proj3, if_bias]
    for j in range(hps):
        in_specs += [par(CONV_WIDTH, j), par(CONV_WIDTH, j, True), par(1, j), par(1, j, True),
                     pl.BlockSpec((1, D), lambda b, h, c, j=j: (0, h * hps + j))]
        args += [conv_w, conv_w, conv_b, conv_b, norm_g]
    return pl.pallas_call(
        functools.partial(_mlstm_kernel, L=L, hps=hps),
        grid=(bsz, H // hps, T // L),
        in_specs=in_specs,
        out_specs=pl.BlockSpec((1, L, hps * D), lambda b, h, c: (b, c, h)),
        out_shape=jax.ShapeDtypeStruct((bsz, T, B_WIDTH), BF16),
        scratch_shapes=[
            pltpu.VMEM((hps, L + 8, D), F32), pltpu.VMEM((hps, L + 8, D), F32),
            pltpu.VMEM((hps, D, D), F32), pltpu.VMEM((hps, 1, D), F32), pltpu.VMEM((hps, 1, 1), F32),
            pltpu.VMEM((LANES, L), F32),
        ],
        compiler_params=_cparams(("parallel", "parallel", "arbitrary")),
        name="mlstm_chunkwise",
    )(*args)


def _alibi_np(n_heads):
    return (2.0 ** (-8.0 * np.arange(1, n_heads + 1, dtype=np.float64) / n_heads)).astype(np.float32)


N_SPLIT = 3


def _bf16_pieces(c):
    pieces, rest = [], np.asarray(c, np.float64)
    for _ in range(N_SPLIT):
        piece = rest.astype(np.float32).astype(jnp.bfloat16).astype(np.float64)
        pieces.append(piece)
        rest = rest - piece
    return np.stack(pieces, axis=1).astype(np.float32)


def _lane_rep(v, groups, width=LANES):
    return jnp.broadcast_to(v.astype(F32).reshape(groups, -1, 1, 1), (groups, v.shape[0] // groups, 1, width))


def _block_diag2(w):
    z = jnp.zeros_like(w)
    return jnp.concatenate([jnp.concatenate([w, z], axis=-1), jnp.concatenate([z, w], axis=-1)], axis=-2)


def _mixers(x2d, bsz, T, p, layer, cfg):
    n, d = x2d.shape
    G = A_KV_HEADS
    w_packed = repack_w_in(jnp.swapaxes(p["w_in"], 1, 2), layer, tc=cfg["tc_repack"])
    proj = norm_matmul(x2d, p["norm_g"].reshape(1, d), w_packed, tm=cfg["tm_in"], tn=cfg["tn_in"])
    proj3 = proj.reshape(bsz, T, PROJ_COLS)

    nseg = T // CMP_STRIDE
    pe = jnp.stack([p["cmp_pe_k"], p["cmp_pe_v"]])
    pe = jnp.concatenate([pe, pe], axis=-1).reshape(2, CMP_BLOCK, 1, LANES)
    w1 = jnp.stack([p["cmp_w1_k"], p["cmp_w1_v"]]).reshape(2, CMP_BLOCK, HEAD_DIM, HEAD_DIM)
    w2 = jnp.stack([p["cmp_w2_k"], p["cmp_w2_v"]])
    k_cmp, v_cmp_t = nsa_compress(proj3, pe, _block_diag2(w1).astype(BF16), _block_diag2(w2).astype(BF16))

    n_sel = T // SEL_BLOCK
    assert n_sel <= MAX_SEL_BLOCKS
    qc_s, tk_s = cfg["qc_slc"], cfg["tk_slc"]
    ci = np.arange(nseg)[None, :] * CMP_STRIDE
    sj = np.arange(MAX_SEL_BLOCKS)[:, None] * SEL_BLOCK
    overlap_t = jnp.asarray(((ci < sj + SEL_BLOCK) & (ci + CMP_BLOCK > sj)).astype(np.float32), dtype=BF16)
    o_cmp, selbt, kaug, vaugt = cmp_select(proj3, k_cmp, v_cmp_t, overlap_t, qc=cfg["qc_cmp"],
                                           n_top=min(SEL_TOPK, n_sel), tk_slc=tk_s)

    sl2 = _alibi_np(A_HEADS).astype(np.float64) * LOG2E
    arows = np.zeros((A_HEADS, AUG_DEPTH - LANES - HEAD_DIM, qc_s), np.float32)
    arows[:, 0:N_SPLIT, :] = _bf16_pieces(sl2)[:, :, None]
    arows[:, N_SPLIT:2 * N_SPLIT, :] = _bf16_pieces(SEL_BLOCK * sl2)[:, :, None]
    arows = arows.reshape(G, HEADS_PER_GROUP, -1, qc_s).transpose(0, 2, 1, 3).reshape(G, -1, HEADS_PER_GROUP * qc_s)
    o_slc = slc_attention(proj3, selbt, kaug, vaugt, jnp.asarray(arows, dtype=BF16),
                          _lane_rep(jnp.asarray(sl2.astype(np.float32)), G, qc_s), qc=qc_s, tk=tk_s)

    y_a = band_attention(proj3, None, (o_cmp, o_slc), q_off=OFF_A_Q, kv_off=OFF_A_KV + 4 * LANES, z_off=OFF_A_Z,
                         qc=cfg["qc_win"], window=NSA_WINDOW)

    if_bias = jnp.concatenate([p["mlstm_i_bias"], p["mlstm_f_bias"], jnp.zeros((LANES - N_IF,), F32)]).reshape(1, LANES)
    y_b = mlstm(proj3, if_bias, p["mlstm_conv_w"], p["mlstm_conv_b"].reshape(1, -1),
                p["mlstm_norm_g"].reshape(1, -1), L=cfg["mlstm_chunk"], hps=cfg["mlstm_heads_per_step"])

    y_c = band_attention(proj3, p["swa_sinks"], None, q_off=OFF_C_Q, kv_off=OFF_C_KV, z_off=OFF_C_Z,
                         qc=cfg["qc_win"], window=SWA_WINDOW)
    return y_a.reshape(n, A_WIDTH), y_b.reshape(n, B_WIDTH), y_c.reshape(n, C_WIDTH), (o_cmp, o_slc, selbt)


def _layer(x2d, bsz, T, p, layer, final_g, cfg):
    n, d = x2d.shape
    y_a, y_b, y_c, _ = _mixers(x2d, bsz, T, p, layer, cfg)
    w_out = p["w_out"].astype(BF16)
    g = final_g if final_g is not None else p["norm_g"]
    return out_proj(y_a, y_b, y_c, x2d,
                    w_out[:A_WIDTH], w_out[A_WIDTH:A_WIDTH + B_WIDTH], w_out[A_WIDTH + B_WIDTH:],
                    g.reshape(1, d), tm=cfg["tm_out"], final_norm=final_g is not None)


def _config(T):
    return dict(tc_repack=256, tm_in=min(1024, T), tn_in=768, tm_out=min(512, T),
                qc_cmp=128, qc_slc=256, tk_slc=512, qc_win=128, mlstm_chunk=min(256, T), mlstm_heads_per_step=4)


def kernel(x, norm_g, w_in, w_out, cmp_pe_k, cmp_w1_k, cmp_w2_k, cmp_pe_v, cmp_w1_v, cmp_w2_v, mlstm_conv_w, mlstm_conv_b, mlstm_i_bias, mlstm_f_bias, mlstm_norm_g, swa_sinks, final_norm_g):
    bsz, T, d = x.shape
    depth = w_in.shape[0]
    cfg = _config(T)
    x2d = x.reshape(bsz * T, d)
    for l in range(depth):
        p = dict(norm_g=norm_g[l], w_in=w_in, w_out=w_out[l],
                 cmp_pe_k=cmp_pe_k[l], cmp_w1_k=cmp_w1_k[l], cmp_w2_k=cmp_w2_k[l],
                 cmp_pe_v=cmp_pe_v[l], cmp_w1_v=cmp_w1_v[l], cmp_w2_v=cmp_w2_v[l],
                 mlstm_conv_w=mlstm_conv_w[l], mlstm_conv_b=mlstm_conv_b[l],
                 mlstm_i_bias=mlstm_i_bias[l], mlstm_f_bias=mlstm_f_bias[l],
                 mlstm_norm_g=mlstm_norm_g[l], swa_sinks=swa_sinks[l])
        x2d = _layer(x2d, bsz, T, p, l, final_norm_g if l == depth - 1 else None, cfg)
    return x2d.reshape(bsz, T, d)
```

```python
import functools

import numpy as np
import jax
import jax.numpy as jnp
from jax import lax
from jax.experimental import pallas as pl
from jax.experimental.pallas import tpu as pltpu

F32 = jnp.float32
BF16 = jnp.bfloat16

HEAD_DIM = 64
A_HEADS = 8
A_KV_HEADS = 2
HEADS_PER_GROUP = A_HEADS // A_KV_HEADS
GROUP_WIDTH = HEADS_PER_GROUP * HEAD_DIM
A_WIDTH = A_HEADS * HEAD_DIM
C_HEADS = 8
C_KV_HEADS = 2
C_WIDTH = C_HEADS * HEAD_DIM
B_HEADS = 4
B_HEAD_DIM = 256
B_WIDTH = B_HEADS * B_HEAD_DIM
CMP_BLOCK = 32
CMP_STRIDE = 16
SEL_BLOCK = 64
SEL_TOPK = 16
SEL_LOCAL = 2
NSA_WINDOW = 512
SWA_WINDOW = 128
CONV_WIDTH = 4
FORCE_BONUS = 1.0e4
NEG_INF = -1.0e30
EPS = 1.0e-6
SCALE = HEAD_DIM ** -0.5
LOG2E = 1.4426950408889634
Q_SCALE = SCALE * LOG2E
LANES = 128
MAX_SEL_BLOCKS = LANES
AUG_DEPTH = 256
VAUG_ROWS = 80

IN_SPLITS = (A_WIDTH, 128, 128, 128, 128, 128, 128, A_HEADS * 3, A_WIDTH,
             2 * B_WIDTH, B_WIDTH, B_HEADS, B_HEADS, B_WIDTH, B_WIDTH,
             C_WIDTH, 128, 128, C_WIDTH)
IN_COLS = sum(IN_SPLITS)
_SRC = np.concatenate([[0], np.cumsum(IN_SPLITS)]).astype(int)
SRC_GATE, SRC_A_Z, SRC_B_IF, SRC_B_O = int(_SRC[7]), int(_SRC[8]), int(_SRC[11]), int(_SRC[13])
N_GATE = A_HEADS * 3
N_IF = 2 * B_HEADS
OFF_A_Q = 0
OFF_A_KV = 512
OFF_A_Z = 1280
OFF_B_QK = 1792
OFF_B_V = 3840
OFF_B_O = 4864
OFF_B_Z = 5888
OFF_C_Q = 6912
OFF_C_KV = 7424
OFF_C_Z = 7680
OFF_A_GATE = 8192
OFF_B_IF = 8320
PROJ_COLS = 8448
W_RUNS = ((0, OFF_A_Q, SRC_GATE), (SRC_A_Z, OFF_A_Z, SRC_B_IF - SRC_A_Z), (SRC_B_O, OFF_B_O, IN_COLS - SRC_B_O))
assert OFF_A_Z + (SRC_B_IF - SRC_A_Z) == OFF_B_O and OFF_B_O + (IN_COLS - SRC_B_O) == OFF_A_GATE

VMEM_LIMIT = 48 * 1024 * 1024


def _cparams(sem):
    return pltpu.CompilerParams(dimension_semantics=sem, vmem_limit_bytes=VMEM_LIMIT)


def _dot(a, b):
    return jnp.dot(a, b, preferred_element_type=F32)


def _dot_nt(a, b):
    return lax.dot_general(a, b, (((1,), (1,)), ((), ())), preferred_element_type=F32)


def _dot_tn(a, b):
    return lax.dot_general(a, b, (((0,), (0,)), ((), ())), preferred_element_type=F32)


def _sigmoid(x):
    return 1.0 / (1.0 + jnp.exp(-x))


def _silu(x):
    return x * _sigmoid(x)


def _heads_to_rows(ot, qc):
    return jnp.concatenate([ot[:, hh * qc:(hh + 1) * qc] for hh in range(HEADS_PER_GROUP)], axis=0).T


def _repack_kernel(wt_ref, o_ref):
    for src, dst, width in W_RUNS:
        o_ref[dst:dst + width, :] = wt_ref[0, src:src + width, :].astype(BF16)
    o_ref[OFF_A_GATE:PROJ_COLS, :] = jnp.zeros((PROJ_COLS - OFF_A_GATE, o_ref.shape[1]), BF16)
    o_ref[OFF_A_GATE:OFF_A_GATE + N_GATE, :] = wt_ref[0, SRC_GATE:SRC_GATE + N_GATE, :].astype(BF16)
    o_ref[OFF_B_IF:OFF_B_IF + N_IF, :] = wt_ref[0, SRC_B_IF:SRC_B_IF + N_IF, :].astype(BF16)


def repack_w_in(w_in_t, layer, *, tc):
    d = w_in_t.shape[2]
    return pl.pallas_call(
        _repack_kernel,
        grid=(d // tc,),
        in_specs=[pl.BlockSpec((1, IN_COLS, tc), lambda i: (layer, 0, i))],
        out_specs=pl.BlockSpec((PROJ_COLS, tc), lambda i: (0, i)),
        out_shape=jax.ShapeDtypeStruct((PROJ_COLS, d), BF16),
        compiler_params=_cparams(("parallel",)),
        name="repack_w_in",
    )(w_in_t)


def _norm_matmul_kernel(x_ref, g_ref, wt_ref, o_ref, h_ref):
    @pl.when(pl.program_id(1) == 0)
    def _():
        x = x_ref[...]
        ms = jnp.mean(x * x, axis=-1, keepdims=True)
        h_ref[...] = (x * lax.rsqrt(ms + EPS) * g_ref[...]).astype(BF16)

    o_ref[...] = _dot_nt(h_ref[...], wt_ref[...])


def norm_matmul(x, g, wt, *, tm, tn):
    n, d = x.shape
    cols = wt.shape[0]
    return pl.pallas_call(
        _norm_matmul_kernel,
        grid=(n // tm, cols // tn),
        in_specs=[
            pl.BlockSpec((tm, d), lambda i, j: (i, 0)),
            pl.BlockSpec((1, d), lambda i, j: (0, 0)),
            pl.BlockSpec((tn, d), lambda i, j: (j, 0)),
        ],
        out_specs=pl.BlockSpec((tm, tn), lambda i, j: (i, j)),
        out_shape=jax.ShapeDtypeStruct((n, cols), F32),
        scratch_shapes=[pltpu.VMEM((tm, d), BF16)],
        compiler_params=_cparams(("parallel", "arbitrary")),
        name="norm_in_proj",
    )(x, g, wt)


def _out_proj_kernel(ya_ref, yb_ref, yc_ref, x_ref, wa_ref, wb_ref, wc_ref, g_ref, o_ref, *, final_norm):
    acc = _dot(ya_ref[...], wa_ref[...]) + _dot(yb_ref[...], wb_ref[...]) + _dot(yc_ref[...], wc_ref[...])
    y = x_ref[...] + acc
    if final_norm:
        ms = jnp.mean(y * y, axis=-1, keepdims=True)
        y = y * lax.rsqrt(ms + EPS) * g_ref[...]
    o_ref[...] = y


def out_proj(ya, yb, yc, x, wa, wb, wc, g, *, tm, final_norm):
    n, d = x.shape
    row = lambda w: pl.BlockSpec((tm, w), lambda i: (i, 0))
    full = lambda a: pl.BlockSpec(a.shape, lambda i: (0, 0))
    return pl.pallas_call(
        functools.partial(_out_proj_kernel, final_norm=final_norm),
        grid=(n // tm,),
        in_specs=[row(ya.shape[1]), row(yb.shape[1]), row(yc.shape[1]), row(d),
                  full(wa), full(wb), full(wc), full(g)],
        out_specs=row(d),
        out_shape=jax.ShapeDtypeStruct((n, d), F32),
        compiler_params=_cparams(("parallel",)),
        name="out_proj",
    )(ya, yb, yc, x, wa, wb, wc, g)


def _compress_kernel(k_ref, v_ref, pe_ref, w1_ref, w2_ref, kc_ref, vct_ref, shift_ref):
    nseg = k_ref.shape[1] // CMP_STRIDE
    for s, src_ref in enumerate((k_ref, v_ref)):
        u1 = jnp.zeros((nseg, LANES), F32)
        u2 = jnp.zeros((nseg, LANES), F32)
        for l in range(CMP_STRIDE):
            x = src_ref[0, pl.ds(l, nseg, stride=CMP_STRIDE), :]
            u1 = u1 + _dot((x + pe_ref[s, l]).astype(BF16), w1_ref[s, l])
            u2 = u2 + _dot((x + pe_ref[s, CMP_STRIDE + l]).astype(BF16), w1_ref[s, CMP_STRIDE + l])
        shift_ref[0:nseg, :] = u2
        shift_ref[nseg:nseg + 8, :] = jnp.zeros((8, LANES), F32)
        pre = u1 + shift_ref[1:nseg + 1, :]
        out = _dot(_silu(pre).astype(BF16), w2_ref[s])
        if s == 0:
            kc_ref[0] = out.astype(BF16)
        else:
            vct_ref[0] = out.T.astype(BF16)


def nsa_compress(proj3, pe, w1, w2):
    bsz, T, _ = proj3.shape
    nseg = T // CMP_STRIDE
    full = lambda a: pl.BlockSpec(a.shape, lambda b: (0,) * a.ndim)
    return pl.pallas_call(
        _compress_kernel,
        grid=(bsz,),
        in_specs=[pl.BlockSpec((1, T, LANES), lambda b: (b, 0, OFF_A_KV // LANES)),
                  pl.BlockSpec((1, T, LANES), lambda b: (b, 0, OFF_A_KV // LANES + 1)),
                  full(pe), full(w1), full(w2)],
        out_specs=[pl.BlockSpec((1, nseg, LANES), lambda b: (b, 0, 0)),
                   pl.BlockSpec((1, LANES, nseg), lambda b: (b, 0, 0))],
        out_shape=[jax.ShapeDtypeStruct((bsz, nseg, LANES), BF16),
                   jax.ShapeDtypeStruct((bsz, LANES, nseg), BF16)],
        scratch_shapes=[pltpu.VMEM((nseg + 8, LANES), F32)],
        compiler_params=_cparams(("parallel",)),
        name="nsa_compress",
    )(proj3, proj3, pe, w1, w2)


def _cmp_select_kernel(q_ref, kc_ref, vct_ref, ovt_ref, ksvs_ref, o_ref, selbt_ref, kaug_ref, vaugt_ref,
                       w_ref, s_ref, p_ref, *, qc, n_top, tk_slc, n_buckets):
    ncmp = kc_ref.shape[1]
    c = pl.program_id(1)
    t0 = c * qc
    t_row = t0 + lax.broadcasted_iota(jnp.int32, (1, qc), 1)
    any_vis = (t_row >= CMP_BLOCK - 1).astype(F32)
    cur = jnp.right_shift(t_row, 6)
    qt = (q_ref[0] * Q_SCALE).T.astype(BF16)
    zero = jnp.zeros((HEAD_DIM, qc), BF16)
    for h in range(A_HEADS):
        g, hh = divmod(h, HEADS_PER_GROUP)
        cols = slice(hh * qc, (hh + 1) * qc)
        w_ref[g, g * HEAD_DIM:(g + 1) * HEAD_DIM, cols] = qt[h * HEAD_DIM:(h + 1) * HEAD_DIM, :]
        w_ref[g, (1 - g) * HEAD_DIM:(2 - g) * HEAD_DIM, cols] = zero

    def attend_and_select(rc, rb):
        cmp_i = lax.broadcasted_iota(jnp.int32, (rc, 1), 0)
        vis = (cmp_i * CMP_STRIDE + (CMP_BLOCK - 1)) <= t_row
        blk = lax.broadcasted_iota(jnp.int32, (rb, qc), 0)
        valid = blk <= cur
        forced = (blk == 0) | (valid & (blk > cur - SEL_LOCAL))
        for g in range(A_KV_HEADS):
            s_ref[g, 0:rc, :] = _dot(kc_ref[0, 0:rc, :], w_ref[g])
        ovt = ovt_ref[0:rb, 0:rc]
        for g in range(A_KV_HEADS):
            psum = jnp.zeros((rc, qc), F32)
            for hh in range(HEADS_PER_GROUP):
                cols = slice(hh * qc, (hh + 1) * qc)
                s = jnp.where(vis, s_ref[g, 0:rc, cols], NEG_INF)
                e = jnp.exp2(s - jnp.max(s, axis=0, keepdims=True))
                p = e * (any_vis / jnp.sum(e, axis=0, keepdims=True))
                p_ref[g, 0:rc, cols] = p.astype(BF16)
                psum = psum + p
            ot = _dot(vct_ref[0, :, 0:rc], p_ref[g, 0:rc, :])
            o_ref[0, :, g * GROUP_WIDTH:(g + 1) * GROUP_WIDTH] = _heads_to_rows(
                ot[g * HEAD_DIM:(g + 1) * HEAD_DIM, :], qc)
            p1 = psum.astype(BF16)
            r1 = psum - p1.astype(F32)
            p2 = r1.astype(BF16)
            p3 = (r1 - p2.astype(F32)).astype(BF16)
            imp = _dot(ovt, p1) + _dot(ovt, p2) + _dot(ovt, p3)
            val = jnp.where(forced, -2.0, jnp.where(valid, imp, -1.0))
            for _ in range(n_top - (SEL_LOCAL + 1)):
                mx = jnp.max(val, axis=0, keepdims=True)
                first = jnp.min(jnp.where(val == mx, blk, MAX_SEL_BLOCKS), axis=0, keepdims=True)
                val = jnp.where(blk == first, -2.0, val)
            chosen = (val == -2.0) & (blk < cur)
            selbt_ref[0, g, 0:rb, :] = jnp.where(chosen, 0.0, NEG_INF).astype(BF16)
            if rb < MAX_SEL_BLOCKS:
                selbt_ref[0, g, rb:MAX_SEL_BLOCKS, :] = jnp.full((MAX_SEL_BLOCKS - rb, qc), NEG_INF, BF16)

    chunks_per_bucket = (ncmp // n_buckets) * CMP_STRIDE // qc
    for b in range(n_buckets):
        @pl.when((c >= b * chunks_per_bucket) & (c < (b + 1) * chunks_per_bucket))
        def _(b=b):
            attend_and_select((b + 1) * (ncmp // n_buckets), (b + 1) * (MAX_SEL_BLOCKS // n_buckets))

    kv = ksvs_ref[0]
    pos = t0 + lax.broadcasted_iota(jnp.int32, (qc, 1), 0)
    lane = lax.broadcasted_iota(jnp.int32, (1, LANES), 1)
    onehot = jnp.where(jnp.right_shift(pos, 6) == lane, 1.0, 0.0).astype(BF16)
    flane = lax.broadcasted_iota(jnp.int32, (1, AUG_DEPTH - LANES - HEAD_DIM), 1)
    in_block = jnp.bitwise_and(pos, SEL_BLOCK - 1).astype(F32)
    in_tile = jnp.bitwise_and(jnp.right_shift(pos, 6), tk_slc // SEL_BLOCK - 1).astype(F32)
    feat = jnp.where(flane < N_SPLIT, in_block, jnp.where(flane < 2 * N_SPLIT, in_tile, 0.0)).astype(BF16)
    vt = kv[:, LANES:2 * LANES].T.astype(BF16)
    ones_row = (lax.broadcasted_iota(jnp.int32, (VAUG_ROWS - HEAD_DIM, qc), 0) == 0).astype(BF16)
    for g in range(A_KV_HEADS):
        kaug_ref[0, g, :, 0:LANES] = onehot
        kaug_ref[0, g, :, LANES:LANES + HEAD_DIM] = kv[:, g * HEAD_DIM:(g + 1) * HEAD_DIM].astype(BF16)
        kaug_ref[0, g, :, LANES + HEAD_DIM:AUG_DEPTH] = feat
        vaugt_ref[0, g, 0:HEAD_DIM, :] = vt[g * HEAD_DIM:(g + 1) * HEAD_DIM, :]
        vaugt_ref[0, g, HEAD_DIM:VAUG_ROWS, :] = ones_row


def cmp_select(proj3, kcmp, vcmp_t, overlap_t, *, qc, n_top, tk_slc):
    bsz, T, _ = proj3.shape
    ncmp = kcmp.shape[1]
    G = A_KV_HEADS
    assert n_top > SEL_LOCAL + 1 and (tk_slc // SEL_BLOCK) & (tk_slc // SEL_BLOCK - 1) == 0
    n_buckets = max(1, min(MAX_SEL_BLOCKS // (2 * n_top), ncmp // LANES))
    assert ncmp % (n_buckets * LANES) == 0 and ncmp == T // CMP_STRIDE and MAX_SEL_BLOCKS // n_buckets > n_top
    return pl.pallas_call(
        functools.partial(_cmp_select_kernel, qc=qc, n_top=n_top, tk_slc=tk_slc, n_buckets=n_buckets),
        grid=(bsz, T // qc),
        in_specs=[
            pl.BlockSpec((1, qc, A_WIDTH), lambda b, c: (b, c, OFF_A_Q // A_WIDTH)),
            pl.BlockSpec((1, ncmp, LANES), lambda b, c: (b, 0, 0)),
            pl.BlockSpec((1, LANES, ncmp), lambda b, c: (b, 0, 0)),
            pl.BlockSpec((MAX_SEL_BLOCKS, ncmp), lambda b, c: (0, 0)),
            pl.BlockSpec((1, qc, 2 * LANES), lambda b, c: (b, c, (OFF_A_KV + 2 * LANES) // (2 * LANES))),
        ],
        out_specs=[
            pl.BlockSpec((1, qc, A_WIDTH), lambda b, c: (b, c, 0)),
            pl.BlockSpec((1, G, MAX_SEL_BLOCKS, qc), lambda b, c: (b, 0, 0, c)),
            pl.BlockSpec((1, G, qc, AUG_DEPTH), lambda b, c: (b, 0, c, 0)),
            pl.BlockSpec((1, G, VAUG_ROWS, qc), lambda b, c: (b, 0, 0, c)),
        ],
        out_shape=[jax.ShapeDtypeStruct((bsz, T, A_WIDTH), F32),
                   jax.ShapeDtypeStruct((bsz, G, MAX_SEL_BLOCKS, T), BF16),
                   jax.ShapeDtypeStruct((bsz, G, T, AUG_DEPTH), BF16),
                   jax.ShapeDtypeStruct((bsz, G, VAUG_ROWS, T), BF16)],
        scratch_shapes=[
            pltpu.VMEM((G, LANES, HEADS_PER_GROUP * qc), BF16),
            pltpu.VMEM((G, ncmp, HEADS_PER_GROUP * qc), F32),
            pltpu.VMEM((G, ncmp, HEADS_PER_GROUP * qc), BF16),
        ],
        compiler_params=_cparams(("parallel", "parallel")),
        name="nsa_cmp_select",
    )(proj3, kcmp, vcmp_t, overlap_t, proj3)


def _slc_kernel(q_ref, selbt_ref, kaug_ref, vaugt_ref, arow_ref, slope_ref, dbias_ref, o_ref,
                qaug_ref, m_ref, acc_ref, *s_refs, qc, tk):
    t0 = pl.multiple_of(pl.program_id(2) * qc, qc)
    selbt = selbt_ref[0, 0]
    qt = (q_ref[0] * Q_SCALE).T.astype(BF16)
    for hh in range(HEADS_PER_GROUP):
        cols = slice(hh * qc, (hh + 1) * qc)
        qaug_ref[0:LANES, cols] = selbt
        qaug_ref[LANES:LANES + HEAD_DIM, cols] = qt[hh * HEAD_DIM:(hh + 1) * HEAD_DIM, :]
    qaug_ref[LANES + HEAD_DIM:AUG_DEPTH, :] = arow_ref[0]
    m_ref[...] = jnp.full(m_ref.shape, NEG_INF, F32)
    acc_ref[...] = jnp.zeros(acc_ref.shape, F32)

    bufs = (s_refs[:HEADS_PER_GROUP], s_refs[HEADS_PER_GROUP:])

    def scores(kt, hh, dst):
        s0 = pl.multiple_of(kt * tk, tk)
        dst[hh][...] = _dot(kaug_ref[0, 0, pl.ds(s0, tk), :], qaug_ref[:, hh * qc:(hh + 1) * qc])

    def diag_scores(hh, dst):
        ka = kaug_ref[0, 0, pl.ds(t0, qc), LANES:AUG_DEPTH]
        st = _dot(ka, qaug_ref[LANES:AUG_DEPTH, hh * qc:(hh + 1) * qc])
        dst[hh][0:qc, :] = st + dbias_ref[...]

    def softmax_pv(hh, src, rows, vt, off):
        c = slope_ref[0, hh] * off
        m_old = m_ref[hh]
        m_new = jnp.maximum(m_old, jnp.max(src[hh][0:rows, :], axis=0, keepdims=True) + c)
        alpha = jnp.exp2(m_old - m_new)
        p = jnp.exp2(src[hh][0:rows, :] - (m_new - c)).astype(BF16)
        acc_ref[hh] = alpha * acc_ref[hh] + _dot(vt, p)
        m_ref[hh] = m_new

    def step(kt, src, next_scores):
        s0 = pl.multiple_of(kt * tk, tk)
        vt = vaugt_ref[0, 0, :, pl.ds(s0, tk)]
        off = (s0 - t0).astype(F32)
        for hh in range(HEADS_PER_GROUP):
            next_scores(hh)
            softmax_pv(hh, src, tk, vt, off)

    def main_step(kt, src, dst):
        step(kt, src, lambda hh: scores(kt + 1, hh, dst))

    n_full = lax.div(t0, tk)
    odd = lax.rem(n_full, 2)

    @pl.when(odd == 0)
    def _():
        for hh in range(HEADS_PER_GROUP):
            scores(0, hh, bufs[0])

    @pl.when(odd == 1)
    def _():
        for hh in range(HEADS_PER_GROUP):
            scores(0, hh, bufs[1])
        main_step(0, bufs[1], bufs[0])

    def body(i, carry):
        kt = odd + 2 * i
        main_step(kt, bufs[0], bufs[1])
        main_step(kt + 1, bufs[1], bufs[0])
        return carry

    lax.fori_loop(0, lax.div(n_full, 2), body, 0)
    step(n_full, bufs[0], lambda hh: diag_scores(hh, bufs[1]))
    base = pl.multiple_of(n_full * tk, tk)
    vt_d = vaugt_ref[0, 0, :, pl.ds(t0, qc)]
    for hh in range(HEADS_PER_GROUP):
        softmax_pv(hh, bufs[1], qc, vt_d, (base - t0).astype(F32))
    ot = jnp.concatenate([acc_ref[hh, 0:HEAD_DIM, :] / acc_ref[hh, HEAD_DIM:HEAD_DIM + 1, :]
                          for hh in range(HEADS_PER_GROUP)], axis=0)
    o_ref[0] = ot.T


def slc_attention(proj3, selbt, kaug, vaugt, arows, slopes, *, qc, tk):
    bsz, T, _ = proj3.shape
    G = A_KV_HEADS
    assert tk % qc == 0 and qc % SEL_BLOCK == 0
    ki = np.arange(qc)[:, None]
    qi = np.arange(qc)[None, :]
    dbias = jnp.asarray(np.where((ki // SEL_BLOCK == qi // SEL_BLOCK) & (ki <= qi), 0.0, NEG_INF), dtype=F32)
    return pl.pallas_call(
        functools.partial(_slc_kernel, qc=qc, tk=tk),
        grid=(bsz, G, T // qc),
        in_specs=[
            pl.BlockSpec((1, qc, GROUP_WIDTH), lambda b, g, c: (b, c, OFF_A_Q // GROUP_WIDTH + g)),
            pl.BlockSpec((1, 1, LANES, qc), lambda b, g, c: (b, g, 0, c)),
            pl.BlockSpec((1, 1, T, AUG_DEPTH), lambda b, g, c: (b, g, 0, 0)),
            pl.BlockSpec((1, 1, VAUG_ROWS, T), lambda b, g, c: (b, g, 0, 0)),
            pl.BlockSpec((1, AUG_DEPTH - LANES - HEAD_DIM, HEADS_PER_GROUP * qc), lambda b, g, c: (g, 0, 0)),
            pl.BlockSpec((1, HEADS_PER_GROUP, 1, qc), lambda b, g, c: (g, 0, 0, 0)),
            pl.BlockSpec((qc, qc), lambda b, g, c: (0, 0)),
        ],
        out_specs=pl.BlockSpec((1, qc, GROUP_WIDTH), lambda b, g, c: (b, c, g)),
        out_shape=jax.ShapeDtypeStruct((bsz, T, A_WIDTH), F32),
        scratch_shapes=[
            pltpu.VMEM((AUG_DEPTH, HEADS_PER_GROUP * qc), BF16),
            pltpu.VMEM((HEADS_PER_GROUP, 1, qc), F32),
            pltpu.VMEM((HEADS_PER_GROUP, VAUG_ROWS, qc), F32),
        ] + [pltpu.VMEM((tk, qc), F32)] * (2 * HEADS_PER_GROUP),
        compiler_params=_cparams(("parallel", "parallel", "arbitrary")),
        name="nsa_slc_attention",
    )(proj3, selbt, kaug, vaugt, arows, slopes, dbias)


def _band_kernel(*refs, qc, window, has_sinks, mix):
    q_refs, kv_ref, bias_ref = refs[0:2], refs[2], refs[3]
    pos = 4
    sink_ref = None
    if has_sinks:
        sink_ref = refs[pos]
        pos += 1
    z_refs = refs[pos:pos + 2]
    pos += 2
    if mix:
        ocmp_ref, oslc_ref, gate_ref = refs[pos:pos + 3]
        pos += 3
    o_ref, w_ref, kwin_ref, vt_ref, p_ref = refs[pos:pos + 5]
    s_refs = refs[pos + 5:pos + 7]
    G = len(q_refs)
    span = qc + window
    lanes = HEADS_PER_GROUP * qc
    n_pad_chunks = window // qc
    c = pl.program_id(1)
    ones_row = (lax.broadcasted_iota(jnp.int32, (VAUG_ROWS - HEAD_DIM, span), 0) == 0).astype(BF16)

    def assemble(kv):
        kwin_ref[...] = kv[:, 0:LANES].astype(BF16)
        vt = kv[:, LANES:2 * LANES].T.astype(BF16)
        for g in range(G):
            vt_ref[g, 0:HEAD_DIM, :] = vt[g * HEAD_DIM:(g + 1) * HEAD_DIM, :]
            vt_ref[g, HEAD_DIM:VAUG_ROWS, :] = ones_row

    @pl.when(c >= n_pad_chunks)
    def _():
        start = pl.multiple_of(c * qc - window, qc)
        assemble(kv_ref[0, pl.ds(start, span), :])

    for j in range(n_pad_chunks):
        @pl.when(c == j)
        def _(j=j):
            n_pad = window - j * qc
            assemble(jnp.concatenate([jnp.zeros((n_pad, 2 * LANES), F32), kv_ref[0, 0:span - n_pad, :]], axis=0))

    zero = jnp.zeros((HEAD_DIM, qc), BF16)
    for g in range(G):
        qt = (q_refs[g][0] * Q_SCALE).T.astype(BF16)
        for hh in range(HEADS_PER_GROUP):
            cols = slice(hh * qc, (hh + 1) * qc)
            w_ref[g, g * HEAD_DIM:(g + 1) * HEAD_DIM, cols] = qt[hh * HEAD_DIM:(hh + 1) * HEAD_DIM, :]
            w_ref[g, (1 - g) * HEAD_DIM:(2 - g) * HEAD_DIM, cols] = zero
    pieces = [slice(r, r + LANES) for r in range(0, span, LANES)]
    for g in range(G):
        for rows in pieces:
            s_refs[g][rows, :] = _dot(kwin_ref[rows, :], w_ref[g])
    if mix:
        gate = _sigmoid(gate_ref[0])
    for g in range(G):
        mx8 = None
        for rows in pieces:
            t = s_refs[g][rows, :] + bias_ref[0, rows, g * lanes:(g + 1) * lanes]
            s_refs[g][rows, :] = t
            m8 = jnp.max(t.reshape(LANES // 8, 8, lanes), axis=0)
            mx8 = m8 if mx8 is None else jnp.maximum(mx8, m8)
        mx = jnp.max(mx8, axis=0, keepdims=True)
        if has_sinks:
            sk = sink_ref[:, g * lanes:(g + 1) * lanes]
            mx = jnp.maximum(mx, sk)
        for rows in pieces:
            p_ref[rows, :] = jnp.exp2(s_refs[g][rows, :] - mx).astype(BF16)
        ot = _dot(vt_ref[g], p_ref[...])
        den = ot[HEAD_DIM:HEAD_DIM + 1, :]
        if has_sinks:
            den = den + jnp.exp2(sk - mx)
        o = _heads_to_rows(ot[0:HEAD_DIM, :] / den, qc)
        z = z_refs[g][0]
        for hh in range(HEADS_PER_GROUP):
            h = g * HEADS_PER_GROUP + hh
            loc = slice(hh * HEAD_DIM, (hh + 1) * HEAD_DIM)
            glob = slice(h * HEAD_DIM, (h + 1) * HEAD_DIM)
            oh = o[:, loc]
            if mix:
                oh = (gate[:, 3 * h:3 * h + 1] * ocmp_ref[0, :, glob]
                      + gate[:, 3 * h + 1:3 * h + 2] * oslc_ref[0, :, glob]
                      + gate[:, 3 * h + 2:3 * h + 3] * oh)
            o_ref[0, :, glob] = (oh * _silu(z[:, loc])).astype(o_ref.dtype)


def _band_bias(qc, window, n_heads):
    span = qc + window
    n_var = window // qc + 1
    slopes = jnp.asarray(_alibi_np(n_heads)) * LOG2E
    row = lax.broadcasted_iota(jnp.int32, (span, qc), 0)
    dist = lax.broadcasted_iota(jnp.int32, (span, qc), 1) + window - row
    band = (dist >= 0) & (dist < window)
    term = -(slopes[:, None, None] * dist.astype(F32)[None])
    n_pad = window - jnp.arange(n_var, dtype=jnp.int32)[:, None, None, None] * qc
    ok = band[None, None] & (row[None, None] >= n_pad)
    bias = jnp.where(ok, term[None], NEG_INF)
    return bias.transpose(0, 2, 1, 3).reshape(n_var, span, n_heads * qc)


def band_attention(proj3, sinks, mix_in, *, q_off, kv_off, z_off, qc, window):
    bsz, T, _ = proj3.shape
    G = 2
    n_heads = G * HEADS_PER_GROUP
    span = qc + window
    assert window % qc == 0 and T >= span
    has_sinks = sinks is not None
    mix = mix_in is not None
    bias = _band_bias(qc, window, n_heads)
    n_var = bias.shape[0]
    grp = lambda off: [pl.BlockSpec((1, qc, GROUP_WIDTH), lambda b, c, g=g: (b, c, off // GROUP_WIDTH + g))
                       for g in range(G)]
    in_specs = grp(q_off) + [
        pl.BlockSpec((1, T, 2 * LANES), lambda b, c: (b, 0, kv_off // (2 * LANES))),
        pl.BlockSpec((1, span, n_heads * qc), lambda b, c: (jnp.minimum(c, n_var - 1), 0, 0)),
    ]
    args = [proj3, proj3, proj3, bias]
    if has_sinks:
        in_specs.append(pl.BlockSpec((1, n_heads * qc), lambda b, c: (0, 0)))
        args.append(jnp.repeat(sinks.astype(F32) * LOG2E, qc).reshape(1, n_heads * qc))
    in_specs += grp(z_off)
    args += [proj3, proj3]
    if mix:
        full = pl.BlockSpec((1, qc, A_WIDTH), lambda b, c: (b, c, 0))
        in_specs += [full, full, pl.BlockSpec((1, qc, LANES), lambda b, c: (b, c, OFF_A_GATE // LANES))]
        args += [mix_in[0], mix_in[1], proj3]
    return pl.pallas_call(
        functools.partial(_band_kernel, qc=qc, window=window, has_sinks=has_sinks, mix=mix),
        grid=(bsz, T // qc),
        in_specs=in_specs,
        out_specs=pl.BlockSpec((1, qc, G * GROUP_WIDTH), lambda b, c: (b, c, 0)),
        out_shape=jax.ShapeDtypeStruct((bsz, T, G * GROUP_WIDTH), BF16),
        scratch_shapes=[
            pltpu.VMEM((G, LANES, HEADS_PER_GROUP * qc), BF16),
            pltpu.VMEM((span, LANES), BF16),
            pltpu.VMEM((G, VAUG_ROWS, span), BF16),
            pltpu.VMEM((span, HEADS_PER_GROUP * qc), BF16),
        ] + [pltpu.VMEM((span, HEADS_PER_GROUP * qc), F32)] * G,
        compiler_params=_cparams(("parallel", "parallel")),
        name="band_attention_w%d" % window,
    )(*args)


def _log_sigmoid(x):
    return jnp.minimum(x, 0.0) - jnp.log(1.0 + jnp.exp(-jnp.abs(x)))


def _mlstm_kernel(*refs, L, hps):
    per_head_in = [refs[5 * j:5 * j + 5] for j in range(hps)]
    if_ref, ifb_ref = refs[5 * hps:5 * hps + 2]
    base = 5 * hps + 2
    per_head_par = [refs[base + 5 * j:base + 5 * j + 5] for j in range(hps)]
    y_ref, xq_ref, xk_ref, c_ref, n_ref, m_ref, ift_ref = refs[base + 5 * hps:]

    @pl.when(pl.program_id(2) == 0)
    def _():
        xq_ref[:, 0:8, :] = jnp.zeros((hps, 8, B_HEAD_DIM), F32)
        xk_ref[:, 0:8, :] = jnp.zeros((hps, 8, B_HEAD_DIM), F32)
        c_ref[...] = jnp.zeros(c_ref.shape, F32)
        n_ref[...] = jnp.zeros(n_ref.shape, F32)
        m_ref[...] = jnp.zeros(m_ref.shape, F32)

    def conv_silu(x_ref, hist_ref, j, w_ref, b_ref):
        hist_ref[j, 8:8 + L, :] = x_ref[0]
        xx = hist_ref[j]
        y = b_ref[...] + w_ref[CONV_WIDTH - 1:CONV_WIDTH, :] * xx[8:8 + L, :]
        for back in range(1, CONV_WIDTH):
            tap = CONV_WIDTH - 1 - back
            y = y + w_ref[tap:tap + 1, :] * pltpu.roll(xx, back, axis=0)[8:8 + L, :]
        hist_ref[j, 0:8, :] = xx[L:L + 8, :]
        return _silu(y)

    gates = if_ref[0] + ifb_ref[...]
    ift_ref[...] = gates.T
    lane = lax.broadcasted_iota(jnp.int32, (1, LANES), 1)
    r_i = lax.broadcasted_iota(jnp.int32, (L, L), 0)
    c_i = lax.broadcasted_iota(jnp.int32, (L, L), 1)
    causal = c_i <= r_i

    for j in range(hps):
        q_ref, k_ref, v_ref, og_ref, z_ref = per_head_in[j]
        cwq_ref, cwk_ref, cbq_ref, cbk_ref, g_ref = per_head_par[j]
        head = pl.program_id(1) * hps + j
        q = conv_silu(q_ref, xq_ref, j, cwq_ref, cbq_ref)
        k = conv_silu(k_ref, xk_ref, j, cwk_ref, cbk_ref) * (B_HEAD_DIM ** -0.5)
        v = v_ref[0]
        i_col = jnp.sum(jnp.where(lane == head, gates, 0.0), axis=1, keepdims=True)
        f_col = jnp.sum(jnp.where(lane == B_HEADS + head, gates, 0.0), axis=1, keepdims=True)
        i_row = ift_ref[pl.ds(head, 1), :]
        f_row = ift_ref[pl.ds(B_HEADS + head, 1), :]
        lf_col = _log_sigmoid(f_col)
        lf_row = _log_sigmoid(f_row)
        b_col = jnp.sum(jnp.where(causal, lf_row, 0.0), axis=1, keepdims=True)
        b_row = jnp.sum(jnp.where(r_i <= c_i, lf_col, 0.0), axis=0, keepdims=True)
        b_last = jnp.sum(lf_row, axis=1, keepdims=True)
        m_prev = m_ref[j]
        log_d = jnp.where(causal, b_col - b_row + i_row, NEG_INF)
        log_inter = b_col + m_prev
        m_t = jnp.maximum(log_inter, jnp.max(log_d, axis=1, keepdims=True))
        w_intra = jnp.exp(log_d - m_t)
        w_inter = jnp.exp(log_inter - m_t)
        qb = q.astype(BF16)
        vb = v.astype(BF16)
        qk = _dot_nt(qb, k.astype(BF16)) * w_intra
        num = w_inter * _dot(qb, c_ref[j].astype(BF16)) + _dot(qk.astype(BF16), vb)
        den = (w_inter * jnp.sum(q * n_ref[j], axis=1, keepdims=True)
               + jnp.sum(qk, axis=1, keepdims=True))
        h = num / jnp.maximum(jnp.abs(den), jnp.exp(-m_t))
        log_g_row = b_last - b_row + i_row
        m_new = jnp.maximum(b_last + m_prev, jnp.max(log_g_row, axis=1, keepdims=True))
        w_g = jnp.exp(b_last - b_col + i_col - m_new)
        decay = jnp.exp(b_last + m_prev - m_new)
        kw = k * w_g
        c_ref[j] = decay * c_ref[j] + _dot_tn(kw.astype(BF16), vb)
        n_ref[j] = decay * n_ref[j] + jnp.sum(kw, axis=0, keepdims=True)
        m_ref[j] = m_new
        hb = _sigmoid(og_ref[0]) * h
        ms = jnp.mean(hb * hb, axis=-1, keepdims=True)
        hb = hb * lax.rsqrt(ms + EPS) * g_ref[...]
        y_ref[0, :, j * B_HEAD_DIM:(j + 1) * B_HEAD_DIM] = (hb * _silu(z_ref[0])).astype(y_ref.dtype)


def mlstm(proj3, if_bias, conv_w, conv_b, norm_g, *, L, hps):
    bsz, T, _ = proj3.shape
    H = B_HEADS
    D = B_HEAD_DIM
    assert H % hps == 0

    def col(off, j, k_half=False):
        base = off // D + (H if k_half else 0) + j
        return pl.BlockSpec((1, L, D), lambda b, h, c: (b, c, base + h * hps))

    def par(rows, j, k_half=False):
        base = (H if k_half else 0) + j
        return pl.BlockSpec((rows, D), lambda b, h, c: (0, base + h * hps))

    in_specs, args = [], []
    for j in range(hps):
        in_specs += [col(OFF_B_QK, j), col(OFF_B_QK, j, True), col(OFF_B_V, j), col(OFF_B_O, j), col(OFF_B_Z, j)]
        args += [proj3] * 5
    in_specs += [pl.BlockSpec((1, L, LANES), lambda b, h, c: (b, c, OFF_B_IF // LANES)),
                 pl.BlockSpec((1, LANES), lambda b, h, c: (0, 0))]
    args += [proj3, if_bias]
    for j in range(hps):
        in_specs += [par(CONV_WIDTH, j), par(CONV_WIDTH, j, True), par(1, j), par(1, j, True),
                     pl.BlockSpec((1, D), lambda b, h, c, j=j: (0, h * hps + j))]
        args += [conv_w, conv_w, conv_b, conv_b, norm_g]
    return pl.pallas_call(
        functools.partial(_mlstm_kernel, L=L, hps=hps),
        grid=(bsz, H // hps, T // L),
        in_specs=in_specs,
        out_specs=pl.BlockSpec((1, L, hps * D), lambda b, h, c: (b, c, h)),
        out_shape=jax.ShapeDtypeStruct((bsz, T, B_WIDTH), BF16),
        scratch_shapes=[
            pltpu.VMEM((hps, L + 8, D), F32), pltpu.VMEM((hps, L + 8, D), F32),
            pltpu.VMEM((hps, D, D), F32), pltpu.VMEM((hps, 1, D), F32), pltpu.VMEM((hps, 1, 1), F32),
            pltpu.VMEM((LANES, L), F32),
        ],
        compiler_params=_cparams(("parallel", "parallel", "arbitrary")),
        name="mlstm_chunkwise",
    )(*args)


def _alibi_np(n_heads):
    return (2.0 ** (-8.0 * np.arange(1, n_heads + 1, dtype=np.float64) / n_heads)).astype(np.float32)


N_SPLIT = 3


def _bf16_pieces(c):
    pieces, rest = [], np.asarray(c, np.float64)
    for _ in range(N_SPLIT):
        piece = rest.astype(np.float32).astype(jnp.bfloat16).astype(np.float64)
        pieces.append(piece)
        rest = rest - piece
    return np.stack(pieces, axis=1).astype(np.float32)


def _lane_rep(v, groups, width=LANES):
    return jnp.broadcast_to(v.astype(F32).reshape(groups, -1, 1, 1), (groups, v.shape[0] // groups, 1, width))


def _block_diag2(w):
    z = jnp.zeros_like(w)
    return jnp.concatenate([jnp.concatenate([w, z], axis=-1), jnp.concatenate([z, w], axis=-1)], axis=-2)


def _mixers(x2d, bsz, T, p, layer, cfg):
    n, d = x2d.shape
    G = A_KV_HEADS
    w_packed = repack_w_in(jnp.swapaxes(p["w_in"], 1, 2), layer, tc=cfg["tc_repack"])
    proj = norm_matmul(x2d, p["norm_g"].reshape(1, d), w_packed, tm=cfg["tm_in"], tn=cfg["tn_in"])
    proj3 = proj.reshape(bsz, T, PROJ_COLS)

    nseg = T // CMP_STRIDE
    pe = jnp.stack([p["cmp_pe_k"], p["cmp_pe_v"]])
    pe = jnp.concatenate([pe, pe], axis=-1).reshape(2, CMP_BLOCK, 1, LANES)
    w1 = jnp.stack([p["cmp_w1_k"], p["cmp_w1_v"]]).reshape(2, CMP_BLOCK, HEAD_DIM, HEAD_DIM)
    w2 = jnp.stack([p["cmp_w2_k"], p["cmp_w2_v"]])
    k_cmp, v_cmp_t = nsa_compress(proj3, pe, _block_diag2(w1).astype(BF16), _block_diag2(w2).astype(BF16))

    n_sel = T // SEL_BLOCK
    assert n_sel <= MAX_SEL_BLOCKS
    qc_s, tk_s = cfg["qc_slc"], cfg["tk_slc"]
    ci = np.arange(nseg)[None, :] * CMP_STRIDE
    sj = np.arange(MAX_SEL_BLOCKS)[:, None] * SEL_BLOCK
    overlap_t = jnp.asarray(((ci < sj + SEL_BLOCK) & (ci + CMP_BLOCK > sj)).astype(np.float32), dtype=BF16)
    o_cmp, selbt, kaug, vaugt = cmp_select(proj3, k_cmp, v_cmp_t, overlap_t, qc=cfg["qc_cmp"],
                                           n_top=min(SEL_TOPK, n_sel), tk_slc=tk_s)

    sl2 = _alibi_np(A_HEADS).astype(np.float64) * LOG2E
    arows = np.zeros((A_HEADS, AUG_DEPTH - LANES - HEAD_DIM, qc_s), np.float32)
    arows[:, 0:N_SPLIT, :] = _bf16_pieces(sl2)[:, :, None]
    arows[:, N_SPLIT:2 * N_SPLIT, :] = _bf16_pieces(SEL_BLOCK * sl2)[:, :, None]
    arows = arows.reshape(G, HEADS_PER_GROUP, -1, qc_s).transpose(0, 2, 1, 3).reshape(G, -1, HEADS_PER_GROUP * qc_s)
    o_slc = slc_attention(proj3, selbt, kaug, vaugt, jnp.asarray(arows, dtype=BF16),
                          _lane_rep(jnp.asarray(sl2.astype(np.float32)), G, qc_s), qc=qc_s, tk=tk_s)

    y_a = band_attention(proj3, None, (o_cmp, o_slc), q_off=OFF_A_Q, kv_off=OFF_A_KV + 4 * LANES, z_off=OFF_A_Z,
                         qc=cfg["qc_win"], window=NSA_WINDOW)

    if_bias = jnp.concatenate([p["mlstm_i_bias"], p["mlstm_f_bias"], jnp.zeros((LANES - N_IF,), F32)]).reshape(1, LANES)
    y_b = mlstm(proj3, if_bias, p["mlstm_conv_w"], p["mlstm_conv_b"].reshape(1, -1),
                p["mlstm_norm_g"].reshape(1, -1), L=cfg["mlstm_chunk"], hps=cfg["mlstm_heads_per_step"])

    y_c = band_attention(proj3, p["swa_sinks"], None, q_off=OFF_C_Q, kv_off=OFF_C_KV, z_off=OFF_C_Z,
                         qc=cfg["qc_win"], window=SWA_WINDOW)
    return y_a.reshape(n, A_WIDTH), y_b.reshape(n, B_WIDTH), y_c.reshape(n, C_WIDTH), (o_cmp, o_slc, selbt)


def _layer(x2d, bsz, T, p, layer, final_g, cfg):
    n, d = x2d.shape
    y_a, y_b, y_c, _ = _mixers(x2d, bsz, T, p, layer, cfg)
    w_out = p["w_out"].astype(BF16)
    g = final_g if final_g is not None else p["norm_g"]
    return out_proj(y_a, y_b, y_c, x2d,
                    w_out[:A_WIDTH], w_out[A_WIDTH:A_WIDTH + B_WIDTH], w_out[A_WIDTH + B_WIDTH:],
                    g.reshape(1, d), tm=cfg["tm_out"], final_norm=final_g is not None)


def _config(T):
    return dict(tc_repack=256, tm_in=min(1024, T), tn_in=1408, tm_out=min(512, T),
                qc_cmp=128, qc_slc=256, tk_slc=512, qc_win=128, mlstm_chunk=min(256, T), mlstm_heads_per_step=4)


def kernel(x, norm_g, w_in, w_out, cmp_pe_k, cmp_w1_k, cmp_w2_k, cmp_pe_v, cmp_w1_v, cmp_w2_v, mlstm_conv_w, mlstm_conv_b, mlstm_i_bias, mlstm_f_bias, mlstm_norm_g, swa_sinks, final_norm_g):
    bsz, T, d = x.shape
    depth = w_in.shape[0]
    cfg = _config(T)
    x2d = x.reshape(bsz * T, d)
    for l in range(depth):
        p = dict(norm_g=norm_g[l], w_in=w_in, w_out=w_out[l],
                 cmp_pe_k=cmp_pe_k[l], cmp_w1_k=cmp_w1_k[l], cmp_w2_k=cmp_w2_k[l],
                 cmp_pe_v=cmp_pe_v[l], cmp_w1_v=cmp_w1_v[l], cmp_w2_v=cmp_w2_v[l],
                 mlstm_conv_w=mlstm_conv_w[l], mlstm_conv_b=mlstm_conv_b[l],
                 mlstm_i_bias=mlstm_i_bias[l], mlstm_f_bias=mlstm_f_bias[l],
                 mlstm_norm_g=mlstm_norm_g[l], swa_sinks=swa_sinks[l])
        x2d = _layer(x2d, bsz, T, p, l, final_norm_g if l == depth - 1 else None, cfg)
    return x2d.reshape(bsz, T, d)
```

```python
import functools
import math

import numpy as np
import jax
import jax.numpy as jnp
from jax import lax
from jax.experimental import pallas as pl
from jax.experimental.pallas import tpu as pltpu

F32 = jnp.float32
BF16 = jnp.bfloat16

HEAD_DIM = 64
A_HEADS = 8
A_KV_HEADS = 2
HEADS_PER_GROUP = A_HEADS // A_KV_HEADS
GROUP_WIDTH = HEADS_PER_GROUP * HEAD_DIM
A_WIDTH = A_HEADS * HEAD_DIM
C_HEADS = 8
C_WIDTH = C_HEADS * HEAD_DIM
B_HEADS = 4
B_HEAD_DIM = 256
B_WIDTH = B_HEADS * B_HEAD_DIM
CMP_BLOCK = 32
CMP_STRIDE = 16
SEL_BLOCK = 64
SEL_TOPK = 16
SEL_LOCAL = 2
NSA_WINDOW = 512
SWA_WINDOW = 128
CONV_WIDTH = 4
NEG_INF = -1.0e30
EPS = 1.0e-6
SCALE = HEAD_DIM ** -0.5
LOG2E = 1.4426950408889634
Q_SCALE = SCALE * LOG2E
LANES = 128
MAX_SEL_BLOCKS = LANES
AUG_DEPTH = 256
VAUG_ROWS = 80
N_SPLIT = 3

IN_SPLITS = (A_WIDTH, 128, 128, 128, 128, 128, 128, A_HEADS * 3, A_WIDTH,
             2 * B_WIDTH, B_WIDTH, B_HEADS, B_HEADS, B_WIDTH, B_WIDTH,
             C_WIDTH, 128, 128, C_WIDTH)
IN_COLS = sum(IN_SPLITS)
_SRC = np.concatenate([[0], np.cumsum(IN_SPLITS)]).astype(int)
SRC_GATE, SRC_A_Z, SRC_B_IF, SRC_B_O = int(_SRC[7]), int(_SRC[8]), int(_SRC[11]), int(_SRC[13])
N_GATE = A_HEADS * 3
N_IF = 2 * B_HEADS
OFF_A_Q = 0
OFF_A_KV = 512
OFF_A_Z = 1280
OFF_B_QK = 1792
OFF_B_V = 3840
OFF_B_O = 4864
OFF_B_Z = 5888
OFF_C_Q = 6912
OFF_C_KV = 7424
OFF_C_Z = 7680
OFF_A_GATE = 8192
OFF_B_IF = 8320
PROJ_COLS = 8448
W_RUNS = ((0, OFF_A_Q, SRC_GATE), (SRC_A_Z, OFF_A_Z, SRC_B_IF - SRC_A_Z), (SRC_B_O, OFF_B_O, IN_COLS - SRC_B_O))
assert OFF_A_Z + (SRC_B_IF - SRC_A_Z) == OFF_B_O and OFF_B_O + (IN_COLS - SRC_B_O) == OFF_A_GATE

V7X_VMEM_BYTES = 64 * 1024 * 1024
VMEM_LIMIT = V7X_VMEM_BYTES * 3 // 4


def _cparams(sem):
    return pltpu.CompilerParams(dimension_semantics=sem, vmem_limit_bytes=VMEM_LIMIT)


def _dot(a, b):
    return jnp.dot(a, b, preferred_element_type=F32)


def _dot_nt(a, b):
    return lax.dot_general(a, b, (((1,), (1,)), ((), ())), preferred_element_type=F32)


def _dot_tn(a, b):
    return lax.dot_general(a, b, (((0,), (0,)), ((), ())), preferred_element_type=F32)


def _sigmoid(x):
    return 1.0 / (1.0 + jnp.exp(-x))


def _silu(x):
    return x * _sigmoid(x)


def _heads_to_rows(ot, qc):
    return jnp.concatenate([ot[:, hh * qc:(hh + 1) * qc] for hh in range(HEADS_PER_GROUP)], axis=0).T


def _repack_kernel(wt_ref, o_ref):
    for src, dst, width in W_RUNS:
        o_ref[dst:dst + width, :] = wt_ref[0, src:src + width, :].astype(BF16)
    o_ref[OFF_A_GATE:PROJ_COLS, :] = jnp.zeros((PROJ_COLS - OFF_A_GATE, o_ref.shape[1]), BF16)
    o_ref[OFF_A_GATE:OFF_A_GATE + N_GATE, :] = wt_ref[0, SRC_GATE:SRC_GATE + N_GATE, :].astype(BF16)
    o_ref[OFF_B_IF:OFF_B_IF + N_IF, :] = wt_ref[0, SRC_B_IF:SRC_B_IF + N_IF, :].astype(BF16)


def repack_w_in(w_in_t, layer, *, tc):
    d = w_in_t.shape[2]
    return pl.pallas_call(
        _repack_kernel,
        grid=(d // tc,),
        in_specs=[pl.BlockSpec((1, IN_COLS, tc), lambda i: (layer, 0, i))],
        out_specs=pl.BlockSpec((PROJ_COLS, tc), lambda i: (0, i)),
        out_shape=jax.ShapeDtypeStruct((PROJ_COLS, d), BF16),
        compiler_params=_cparams(("parallel",)),
        name="repack_w_in",
    )(w_in_t)


def _norm_matmul_kernel(x_ref, g_ref, wt_ref, o_ref, h_ref):
    @pl.when(pl.program_id(1) == 0)
    def _():
        x = x_ref[...]
        ms = jnp.mean(x * x, axis=-1, keepdims=True)
        h_ref[...] = (x * lax.rsqrt(ms + EPS) * g_ref[...]).astype(BF16)

    o_ref[...] = _dot_nt(h_ref[...], wt_ref[...])


def norm_matmul(x, g, wt, *, tm, tn):
    n, d = x.shape
    cols = wt.shape[0]
    return pl.pallas_call(
        _norm_matmul_kernel,
        grid=(n // tm, cols // tn),
        in_specs=[
            pl.BlockSpec((tm, d), lambda i, j: (i, 0)),
            pl.BlockSpec((1, d), lambda i, j: (0, 0)),
            pl.BlockSpec((tn, d), lambda i, j: (j, 0)),
        ],
        out_specs=pl.BlockSpec((tm, tn), lambda i, j: (i, j)),
        out_shape=jax.ShapeDtypeStruct((n, cols), F32),
        scratch_shapes=[pltpu.VMEM((tm, d), BF16)],
        compiler_params=_cparams(("parallel", "arbitrary")),
        name="norm_in_proj",
    )(x, g, wt)


def _out_proj_kernel(ya_ref, yb_ref, yc_ref, x_ref, wa_ref, wb_ref, wc_ref, g_ref, o_ref, *, final_norm):
    acc = _dot(ya_ref[...], wa_ref[...]) + _dot(yb_ref[...], wb_ref[...]) + _dot(yc_ref[...], wc_ref[...])
    y = x_ref[...] + acc
    if final_norm:
        ms = jnp.mean(y * y, axis=-1, keepdims=True)
        y = y * lax.rsqrt(ms + EPS) * g_ref[...]
    o_ref[...] = y


def out_proj(ya, yb, yc, x, wa, wb, wc, g, *, tm, final_norm):
    n, d = x.shape
    row = lambda w: pl.BlockSpec((tm, w), lambda i: (i, 0))
    full = lambda a: pl.BlockSpec(a.shape, lambda i: (0, 0))
    return pl.pallas_call(
        functools.partial(_out_proj_kernel, final_norm=final_norm),
        grid=(n // tm,),
        in_specs=[row(ya.shape[1]), row(yb.shape[1]), row(yc.shape[1]), row(d),
                  full(wa), full(wb), full(wc), full(g)],
        out_specs=row(d),
        out_shape=jax.ShapeDtypeStruct((n, d), F32),
        compiler_params=_cparams(("parallel",)),
        name="out_proj",
    )(ya, yb, yc, x, wa, wb, wc, g)


def _compress_kernel(k_ref, v_ref, pe_ref, w1_ref, w2_ref, kc_ref, vct_ref, shift_ref):
    nseg = k_ref.shape[1] // CMP_STRIDE
    for s, src_ref in enumerate((k_ref, v_ref)):
        u1 = jnp.zeros((nseg, LANES), F32)
        u2 = jnp.zeros((nseg, LANES), F32)
        for l in range(CMP_STRIDE):
            x = src_ref[0, pl.ds(l, nseg, stride=CMP_STRIDE), :]
            u1 = u1 + _dot((x + pe_ref[s, l]).astype(BF16), w1_ref[s, l])
            u2 = u2 + _dot((x + pe_ref[s, CMP_STRIDE + l]).astype(BF16), w1_ref[s, CMP_STRIDE + l])
        shift_ref[0:nseg, :] = u2
        shift_ref[nseg:nseg + 8, :] = jnp.zeros((8, LANES), F32)
        pre = u1 + shift_ref[1:nseg + 1, :]
        out = _dot(_silu(pre).astype(BF16), w2_ref[s])
        if s == 0:
            kc_ref[0] = out.astype(BF16)
        else:
            vct_ref[0] = out.T.astype(BF16)


def nsa_compress(proj3, pe, w1, w2):
    bsz, T, _ = proj3.shape
    nseg = T // CMP_STRIDE
    full = lambda a: pl.BlockSpec(a.shape, lambda b: (0,) * a.ndim)
    return pl.pallas_call(
        _compress_kernel,
        grid=(bsz,),
        in_specs=[pl.BlockSpec((1, T, LANES), lambda b: (b, 0, OFF_A_KV // LANES)),
                  pl.BlockSpec((1, T, LANES), lambda b: (b, 0, OFF_A_KV // LANES + 1)),
                  full(pe), full(w1), full(w2)],
        out_specs=[pl.BlockSpec((1, nseg, LANES), lambda b: (b, 0, 0)),
                   pl.BlockSpec((1, LANES, nseg), lambda b: (b, 0, 0))],
        out_shape=[jax.ShapeDtypeStruct((bsz, nseg, LANES), BF16),
                   jax.ShapeDtypeStruct((bsz, LANES, nseg), BF16)],
        scratch_shapes=[pltpu.VMEM((nseg + 8, LANES), F32)],
        compiler_params=_cparams(("parallel",)),
        name="nsa_compress",
    )(proj3, proj3, pe, w1, w2)


def _cmp_select_kernel(q_ref, kc_ref, vct_ref, ovt_ref, ksvs_ref, o_ref, selbt_ref, kaug_ref, vaugt_ref,
                       w_ref, s_ref, p_ref, *, qc, n_top, tk_slc, n_buckets):
    ncmp = kc_ref.shape[1]
    c = pl.program_id(1)
    t0 = c * qc
    t_row = t0 + lax.broadcasted_iota(jnp.int32, (1, qc), 1)
    any_vis = (t_row >= CMP_BLOCK - 1).astype(F32)
    cur = jnp.right_shift(t_row, 6)
    qt = (q_ref[0] * Q_SCALE).T.astype(BF16)
    zero = jnp.zeros((HEAD_DIM, qc), BF16)
    for h in range(A_HEADS):
        g, hh = divmod(h, HEADS_PER_GROUP)
        cols = slice(hh * qc, (hh + 1) * qc)
        w_ref[g, g * HEAD_DIM:(g + 1) * HEAD_DIM, cols] = qt[h * HEAD_DIM:(h + 1) * HEAD_DIM, :]
        w_ref[g, (1 - g) * HEAD_DIM:(2 - g) * HEAD_DIM, cols] = zero

    def attend_and_select(rc, rb):
        cmp_i = lax.broadcasted_iota(jnp.int32, (rc, 1), 0)
        vis = (cmp_i * CMP_STRIDE + (CMP_BLOCK - 1)) <= t_row
        blk = lax.broadcasted_iota(jnp.int32, (rb, qc), 0)
        valid = blk <= cur
        forced = (blk == 0) | (valid & (blk > cur - SEL_LOCAL))
        for g in range(A_KV_HEADS):
            s_ref[g, 0:rc, :] = _dot(kc_ref[0, 0:rc, :], w_ref[g])
        ovt = ovt_ref[0:rb, 0:rc]
        for g in range(A_KV_HEADS):
            psum = jnp.zeros((rc, qc), F32)
            for hh in range(HEADS_PER_GROUP):
                cols = slice(hh * qc, (hh + 1) * qc)
                s = jnp.where(vis, s_ref[g, 0:rc, cols], NEG_INF)
                e = jnp.exp2(s - jnp.max(s, axis=0, keepdims=True))
                p = e * (any_vis / jnp.sum(e, axis=0, keepdims=True))
                p_ref[g, 0:rc, cols] = p.astype(BF16)
                psum = psum + p
            ot = _dot(vct_ref[0, :, 0:rc], p_ref[g, 0:rc, :])
            o_ref[0, :, g * GROUP_WIDTH:(g + 1) * GROUP_WIDTH] = _heads_to_rows(
                ot[g * HEAD_DIM:(g + 1) * HEAD_DIM, :], qc)
            p1 = psum.astype(BF16)
            r1 = psum - p1.astype(F32)
            p2 = r1.astype(BF16)
            p3 = (r1 - p2.astype(F32)).astype(BF16)
            imp = _dot(ovt, p1) + _dot(ovt, p2) + _dot(ovt, p3)
            val = jnp.where(forced, -2.0, jnp.where(valid, imp, -1.0))
            for _ in range(n_top - (SEL_LOCAL + 1)):
                mx = jnp.max(val, axis=0, keepdims=True)
                first = jnp.min(jnp.where(val == mx, blk, MAX_SEL_BLOCKS), axis=0, keepdims=True)
                val = jnp.where(blk == first, -2.0, val)
            chosen = (val == -2.0) & (blk < cur)
            selbt_ref[0, g, 0:rb, :] = jnp.where(chosen, 0.0, NEG_INF).astype(BF16)
            if rb < MAX_SEL_BLOCKS:
                selbt_ref[0, g, rb:MAX_SEL_BLOCKS, :] = jnp.full((MAX_SEL_BLOCKS - rb, qc), NEG_INF, BF16)

    chunks_per_bucket = (ncmp // n_buckets) * CMP_STRIDE // qc
    for b in range(n_buckets):
        @pl.when((c >= b * chunks_per_bucket) & (c < (b + 1) * chunks_per_bucket))
        def _(b=b):
            attend_and_select((b + 1) * (ncmp // n_buckets), (b + 1) * (MAX_SEL_BLOCKS // n_buckets))

    kv = ksvs_ref[0]
    pos = t0 + lax.broadcasted_iota(jnp.int32, (qc, 1), 0)
    lane = lax.broadcasted_iota(jnp.int32, (1, LANES), 1)
    onehot = jnp.where(jnp.right_shift(pos, 6) == lane, 1.0, 0.0).astype(BF16)
    flane = lax.broadcasted_iota(jnp.int32, (1, AUG_DEPTH - LANES - HEAD_DIM), 1)
    in_block = jnp.bitwise_and(pos, SEL_BLOCK - 1).astype(F32)
    in_tile = jnp.bitwise_and(jnp.right_shift(pos, 6), tk_slc // SEL_BLOCK - 1).astype(F32)
    feat = jnp.where(flane < N_SPLIT, in_block, jnp.where(flane < 2 * N_SPLIT, in_tile, 0.0)).astype(BF16)
    vt = kv[:, LANES:2 * LANES].T.astype(BF16)
    ones_row = (lax.broadcasted_iota(jnp.int32, (VAUG_ROWS - HEAD_DIM, qc), 0) == 0).astype(BF16)
    for g in range(A_KV_HEADS):
        kaug_ref[0, g, :, 0:LANES] = onehot
        kaug_ref[0, g, :, LANES:LANES + HEAD_DIM] = kv[:, g * HEAD_DIM:(g + 1) * HEAD_DIM].astype(BF16)
        kaug_ref[0, g, :, LANES + HEAD_DIM:AUG_DEPTH] = feat
        vaugt_ref[0, g, 0:HEAD_DIM, :] = vt[g * HEAD_DIM:(g + 1) * HEAD_DIM, :]
        vaugt_ref[0, g, HEAD_DIM:VAUG_ROWS, :] = ones_row


def cmp_select(proj3, kcmp, vcmp_t, overlap_t, *, qc, n_top, tk_slc):
    bsz, T, _ = proj3.shape
    ncmp = kcmp.shape[1]
    G = A_KV_HEADS
    assert n_top > SEL_LOCAL + 1 and (tk_slc // SEL_BLOCK) & (tk_slc // SEL_BLOCK - 1) == 0
    n_buckets = max(1, min(MAX_SEL_BLOCKS // (2 * n_top), ncmp // LANES))
    assert ncmp % (n_buckets * LANES) == 0 and ncmp == T // CMP_STRIDE and MAX_SEL_BLOCKS // n_buckets > n_top
    return pl.pallas_call(
        functools.partial(_cmp_select_kernel, qc=qc, n_top=n_top, tk_slc=tk_slc, n_buckets=n_buckets),
        grid=(bsz, T // qc),
        in_specs=[
            pl.BlockSpec((1, qc, A_WIDTH), lambda b, c: (b, c, OFF_A_Q // A_WIDTH)),
            pl.BlockSpec((1, ncmp, LANES), lambda b, c: (b, 0, 0)),
            pl.BlockSpec((1, LANES, ncmp), lambda b, c: (b, 0, 0)),
            pl.BlockSpec((MAX_SEL_BLOCKS, ncmp), lambda b, c: (0, 0)),
            pl.BlockSpec((1, qc, 2 * LANES), lambda b, c: (b, c, (OFF_A_KV + 2 * LANES) // (2 * LANES))),
        ],
        out_specs=[
            pl.BlockSpec((1, qc, A_WIDTH), lambda b, c: (b, c, 0)),
            pl.BlockSpec((1, G, MAX_SEL_BLOCKS, qc), lambda b, c: (b, 0, 0, c)),
            pl.BlockSpec((1, G, qc, AUG_DEPTH), lambda b, c: (b, 0, c, 0)),
            pl.BlockSpec((1, G, VAUG_ROWS, qc), lambda b, c: (b, 0, 0, c)),
        ],
        out_shape=[jax.ShapeDtypeStruct((bsz, T, A_WIDTH), F32),
                   jax.ShapeDtypeStruct((bsz, G, MAX_SEL_BLOCKS, T), BF16),
                   jax.ShapeDtypeStruct((bsz, G, T, AUG_DEPTH), BF16),
                   jax.ShapeDtypeStruct((bsz, G, VAUG_ROWS, T), BF16)],
        scratch_shapes=[
            pltpu.VMEM((G, LANES, HEADS_PER_GROUP * qc), BF16),
            pltpu.VMEM((G, ncmp, HEADS_PER_GROUP * qc), F32),
            pltpu.VMEM((G, ncmp, HEADS_PER_GROUP * qc), BF16),
        ],
        compiler_params=_cparams(("parallel", "parallel")),
        name="nsa_cmp_select",
    )(proj3, kcmp, vcmp_t, overlap_t, proj3)


def _slc_kernel(q_ref, selbt_ref, kaug_ref, vaugt_ref, arow_ref, slope_ref, dbias_ref, o_ref,
                qaug_ref, m_ref, acc_ref, *s_refs, qc, tk):
    t0 = pl.multiple_of(pl.program_id(2) * qc, qc)
    selbt = selbt_ref[0, 0]
    qt = (q_ref[0] * Q_SCALE).T.astype(BF16)
    for hh in range(HEADS_PER_GROUP):
        cols = slice(hh * qc, (hh + 1) * qc)
        qaug_ref[0:LANES, cols] = selbt
        qaug_ref[LANES:LANES + HEAD_DIM, cols] = qt[hh * HEAD_DIM:(hh + 1) * HEAD_DIM, :]
    qaug_ref[LANES + HEAD_DIM:AUG_DEPTH, :] = arow_ref[0]
    m_ref[...] = jnp.full(m_ref.shape, NEG_INF, F32)
    acc_ref[...] = jnp.zeros(acc_ref.shape, F32)

    bufs = (s_refs[:HEADS_PER_GROUP], s_refs[HEADS_PER_GROUP:])

    def scores(kt, hh, dst):
        s0 = pl.multiple_of(kt * tk, tk)
        dst[hh][...] = _dot(kaug_ref[0, 0, pl.ds(s0, tk), :], qaug_ref[:, hh * qc:(hh + 1) * qc])

    def diag_scores(hh, dst):
        ka = kaug_ref[0, 0, pl.ds(t0, qc), LANES:AUG_DEPTH]
        st = _dot(ka, qaug_ref[LANES:AUG_DEPTH, hh * qc:(hh + 1) * qc])
        dst[hh][0:qc, :] = st + dbias_ref[...]

    def softmax_pv(hh, src, rows, vt, off):
        c = slope_ref[0, hh] * off
        m_old = m_ref[hh]
        m_new = jnp.maximum(m_old, jnp.max(src[hh][0:rows, :], axis=0, keepdims=True) + c)
        alpha = jnp.exp2(m_old - m_new)
        p = jnp.exp2(src[hh][0:rows, :] - (m_new - c)).astype(BF16)
        acc_ref[hh] = alpha * acc_ref[hh] + _dot(vt, p)
        m_ref[hh] = m_new

    def step(kt, src, next_scores):
        s0 = pl.multiple_of(kt * tk, tk)
        vt = vaugt_ref[0, 0, :, pl.ds(s0, tk)]
        off = (s0 - t0).astype(F32)
        for hh in range(HEADS_PER_GROUP):
            next_scores(hh)
            softmax_pv(hh, src, tk, vt, off)

    def main_step(kt, src, dst):
        step(kt, src, lambda hh: scores(kt + 1, hh, dst))

    n_full = lax.div(t0, tk)
    odd = lax.rem(n_full, 2)

    @pl.when(odd == 0)
    def _():
        for hh in range(HEADS_PER_GROUP):
            scores(0, hh, bufs[0])

    @pl.when(odd == 1)
    def _():
        for hh in range(HEADS_PER_GROUP):
            scores(0, hh, bufs[1])
        main_step(0, bufs[1], bufs[0])

    def body(i, carry):
        kt = odd + 2 * i
        main_step(kt, bufs[0], bufs[1])
        main_step(kt + 1, bufs[1], bufs[0])
        return carry

    lax.fori_loop(0, lax.div(n_full, 2), body, 0)
    step(n_full, bufs[0], lambda hh: diag_scores(hh, bufs[1]))
    base = pl.multiple_of(n_full * tk, tk)
    vt_d = vaugt_ref[0, 0, :, pl.ds(t0, qc)]
    for hh in range(HEADS_PER_GROUP):
        softmax_pv(hh, bufs[1], qc, vt_d, (base - t0).astype(F32))
    ot = jnp.concatenate([acc_ref[hh, 0:HEAD_DIM, :] / acc_ref[hh, HEAD_DIM:HEAD_DIM + 1, :]
                          for hh in range(HEADS_PER_GROUP)], axis=0)
    o_ref[0] = ot.T


def slc_attention(proj3, selbt, kaug, vaugt, arows, slopes, *, qc, tk):
    bsz, T, _ = proj3.shape
    G = A_KV_HEADS
    assert tk % qc == 0 and qc % SEL_BLOCK == 0
    ki = np.arange(qc)[:, None]
    qi = np.arange(qc)[None, :]
    dbias = jnp.asarray(np.where((ki // SEL_BLOCK == qi // SEL_BLOCK) & (ki <= qi), 0.0, NEG_INF), dtype=F32)
    return pl.pallas_call(
        functools.partial(_slc_kernel, qc=qc, tk=tk),
        grid=(bsz, G, T // qc),
        in_specs=[
            pl.BlockSpec((1, qc, GROUP_WIDTH), lambda b, g, c: (b, c, OFF_A_Q // GROUP_WIDTH + g)),
            pl.BlockSpec((1, 1, LANES, qc), lambda b, g, c: (b, g, 0, c)),
            pl.BlockSpec((1, 1, T, AUG_DEPTH), lambda b, g, c: (b, g, 0, 0)),
            pl.BlockSpec((1, 1, VAUG_ROWS, T), lambda b, g, c: (b, g, 0, 0)),
            pl.BlockSpec((1, AUG_DEPTH - LANES - HEAD_DIM, HEADS_PER_GROUP * qc), lambda b, g, c: (g, 0, 0)),
            pl.BlockSpec((1, HEADS_PER_GROUP, 1, qc), lambda b, g, c: (g, 0, 0, 0)),
            pl.BlockSpec((qc, qc), lambda b, g, c: (0, 0)),
        ],
        out_specs=pl.BlockSpec((1, qc, GROUP_WIDTH), lambda b, g, c: (b, c, g)),
        out_shape=jax.ShapeDtypeStruct((bsz, T, A_WIDTH), F32),
        scratch_shapes=[
            pltpu.VMEM((AUG_DEPTH, HEADS_PER_GROUP * qc), BF16),
            pltpu.VMEM((HEADS_PER_GROUP, 1, qc), F32),
            pltpu.VMEM((HEADS_PER_GROUP, VAUG_ROWS, qc), F32),
        ] + [pltpu.VMEM((tk, qc), F32)] * (2 * HEADS_PER_GROUP),
        compiler_params=_cparams(("parallel", "parallel", "arbitrary")),
        name="nsa_slc_attention",
    )(proj3, selbt, kaug, vaugt, arows, slopes, dbias)


def _band_kernel(*refs, qc, window, has_sinks, mix):
    q_refs, kv_ref, bias_ref = refs[0:2], refs[2], refs[3]
    pos = 4
    sink_ref = None
    if has_sinks:
        sink_ref = refs[pos]
        pos += 1
    z_refs = refs[pos:pos + 2]
    pos += 2
    if mix:
        ocmp_ref, oslc_ref, gate_ref = refs[pos:pos + 3]
        pos += 3
    o_ref, w_ref, kwin_ref, vt_ref, p_ref = refs[pos:pos + 5]
    s_refs = refs[pos + 5:pos + 7]
    G = len(q_refs)
    span = qc + window
    lanes = HEADS_PER_GROUP * qc
    n_pad_chunks = -(-window // qc)
    c = pl.program_id(1)
    ones_row = (lax.broadcasted_iota(jnp.int32, (VAUG_ROWS - HEAD_DIM, span), 0) == 0).astype(BF16)

    def assemble(kv):
        kwin_ref[...] = kv[:, 0:LANES].astype(BF16)
        vt = kv[:, LANES:2 * LANES].T.astype(BF16)
        for g in range(G):
            vt_ref[g, 0:HEAD_DIM, :] = vt[g * HEAD_DIM:(g + 1) * HEAD_DIM, :]
            vt_ref[g, HEAD_DIM:VAUG_ROWS, :] = ones_row

    @pl.when(c >= n_pad_chunks)
    def _():
        start = pl.multiple_of(c * qc - window, math.gcd(qc, window))
        assemble(kv_ref[0, pl.ds(start, span), :])

    for j in range(n_pad_chunks):
        @pl.when(c == j)
        def _(j=j):
            n_pad = window - j * qc
            assemble(jnp.concatenate([jnp.zeros((n_pad, 2 * LANES), F32), kv_ref[0, 0:span - n_pad, :]], axis=0))

    zero = jnp.zeros((HEAD_DIM, qc), BF16)
    for g in range(G):
        qt = (q_refs[g][0] * Q_SCALE).T.astype(BF16)
        for hh in range(HEADS_PER_GROUP):
            cols = slice(hh * qc, (hh + 1) * qc)
            w_ref[g, g * HEAD_DIM:(g + 1) * HEAD_DIM, cols] = qt[hh * HEAD_DIM:(hh + 1) * HEAD_DIM, :]
            w_ref[g, (1 - g) * HEAD_DIM:(2 - g) * HEAD_DIM, cols] = zero
    pieces = [slice(r, r + LANES) for r in range(0, span, LANES)]
    for g in range(G):
        for rows in pieces:
            s_refs[g][rows, :] = _dot(kwin_ref[rows, :], w_ref[g])
    if mix:
        gate = _sigmoid(gate_ref[0])
    for g in range(G):
        mx8 = None
        for rows in pieces:
            t = s_refs[g][rows, :] + bias_ref[0, rows, g * lanes:(g + 1) * lanes]
            s_refs[g][rows, :] = t
            m8 = jnp.max(t.reshape(LANES // 8, 8, lanes), axis=0)
            mx8 = m8 if mx8 is None else jnp.maximum(mx8, m8)
        mx = jnp.max(mx8, axis=0, keepdims=True)
        if has_sinks:
            sk = sink_ref[:, g * lanes:(g + 1) * lanes]
            mx = jnp.maximum(mx, sk)
        for rows in pieces:
            p_ref[rows, :] = jnp.exp2(s_refs[g][rows, :] - mx).astype(BF16)
        ot = _dot(vt_ref[g], p_ref[...])
        den = ot[HEAD_DIM:HEAD_DIM + 1, :]
        if has_sinks:
            den = den + jnp.exp2(sk - mx)
        o = _heads_to_rows(ot[0:HEAD_DIM, :] / den, qc)
        z = z_refs[g][0]
        for hh in range(HEADS_PER_GROUP):
            h = g * HEADS_PER_GROUP + hh
            loc = slice(hh * HEAD_DIM, (hh + 1) * HEAD_DIM)
            glob = slice(h * HEAD_DIM, (h + 1) * HEAD_DIM)
            oh = o[:, loc]
            if mix:
                oh = (gate[:, 3 * h:3 * h + 1] * ocmp_ref[0, :, glob]
                      + gate[:, 3 * h + 1:3 * h + 2] * oslc_ref[0, :, glob]
                      + gate[:, 3 * h + 2:3 * h + 3] * oh)
            o_ref[0, :, glob] = (oh * _silu(z[:, loc])).astype(o_ref.dtype)


def _band_bias(qc, window, n_heads):
    span = qc + window
    n_var = -(-window // qc) + 1
    slopes = jnp.asarray(_alibi_np(n_heads)) * LOG2E
    row = lax.broadcasted_iota(jnp.int32, (span, qc), 0)
    dist = lax.broadcasted_iota(jnp.int32, (span, qc), 1) + window - row
    band = (dist >= 0) & (dist < window)
    term = -(slopes[:, None, None] * dist.astype(F32)[None])
    n_pad = window - jnp.arange(n_var, dtype=jnp.int32)[:, None, None, None] * qc
    ok = band[None, None] & (row[None, None] >= n_pad)
    bias = jnp.where(ok, term[None], NEG_INF)
    return bias.transpose(0, 2, 1, 3).reshape(n_var, span, n_heads * qc)


def band_attention(proj3, sinks, mix_in, *, q_off, kv_off, z_off, qc, window):
    bsz, T, _ = proj3.shape
    G = 2
    n_heads = G * HEADS_PER_GROUP
    span = qc + window
    assert span % LANES == 0 and math.gcd(qc, window) % 8 == 0 and T >= span
    has_sinks = sinks is not None
    mix = mix_in is not None
    bias = _band_bias(qc, window, n_heads)
    n_var = bias.shape[0]
    grp = lambda off: [pl.BlockSpec((1, qc, GROUP_WIDTH), lambda b, c, g=g: (b, c, off // GROUP_WIDTH + g))
                       for g in range(G)]
    in_specs = grp(q_off) + [
        pl.BlockSpec((1, T, 2 * LANES), lambda b, c: (b, 0, kv_off // (2 * LANES))),
        pl.BlockSpec((1, span, n_heads * qc), lambda b, c: (jnp.minimum(c, n_var - 1), 0, 0)),
    ]
    args = [proj3, proj3, proj3, bias]
    if has_sinks:
        in_specs.append(pl.BlockSpec((1, n_heads * qc), lambda b, c: (0, 0)))
        args.append(jnp.repeat(sinks.astype(F32) * LOG2E, qc).reshape(1, n_heads * qc))
    in_specs += grp(z_off)
    args += [proj3, proj3]
    if mix:
        full = pl.BlockSpec((1, qc, A_WIDTH), lambda b, c: (b, c, 0))
        in_specs += [full, full, pl.BlockSpec((1, qc, LANES), lambda b, c: (b, c, OFF_A_GATE // LANES))]
        args += [mix_in[0], mix_in[1], proj3]
    return pl.pallas_call(
        functools.partial(_band_kernel, qc=qc, window=window, has_sinks=has_sinks, mix=mix),
        grid=(bsz, T // qc),
        in_specs=in_specs,
        out_specs=pl.BlockSpec((1, qc, G * GROUP_WIDTH), lambda b, c: (b, c, 0)),
        out_shape=jax.ShapeDtypeStruct((bsz, T, G * GROUP_WIDTH), BF16),
        scratch_shapes=[
            pltpu.VMEM((G, LANES, HEADS_PER_GROUP * qc), BF16),
            pltpu.VMEM((span, LANES), BF16),
            pltpu.VMEM((G, VAUG_ROWS, span), BF16),
            pltpu.VMEM((span, HEADS_PER_GROUP * qc), BF16),
        ] + [pltpu.VMEM((span, HEADS_PER_GROUP * qc), F32)] * G,
        compiler_params=_cparams(("parallel", "parallel")),
        name="band_attention_w%d" % window,
    )(*args)


def _log_sigmoid(x):
    return jnp.minimum(x, 0.0) - jnp.log(1.0 + jnp.exp(-jnp.abs(x)))


def _mlstm_kernel(*refs, L, hps):
    per_head_in = [refs[5 * j:5 * j + 5] for j in range(hps)]
    if_ref, ifb_ref = refs[5 * hps:5 * hps + 2]
    base = 5 * hps + 2
    per_head_par = [refs[base + 5 * j:base + 5 * j + 5] for j in range(hps)]
    y_ref, xq_ref, xk_ref, c_ref, n_ref, m_ref, ift_ref = refs[base + 5 * hps:]

    @pl.when(pl.program_id(2) == 0)
    def _():
        xq_ref[:, 0:8, :] = jnp.zeros((hps, 8, B_HEAD_DIM), F32)
        xk_ref[:, 0:8, :] = jnp.zeros((hps, 8, B_HEAD_DIM), F32)
        c_ref[...] = jnp.zeros(c_ref.shape, F32)
        n_ref[...] = jnp.zeros(n_ref.shape, F32)
        m_ref[...] = jnp.zeros(m_ref.shape, F32)

    def conv_silu(x_ref, hist_ref, j, w_ref, b_ref):
        hist_ref[j, 8:8 + L, :] = x_ref[0]
        xx = hist_ref[j]
        y = b_ref[...] + w_ref[CONV_WIDTH - 1:CONV_WIDTH, :] * xx[8:8 + L, :]
        for back in range(1, CONV_WIDTH):
            tap = CONV_WIDTH - 1 - back
            y = y + w_ref[tap:tap + 1, :] * pltpu.roll(xx, back, axis=0)[8:8 + L, :]
        hist_ref[j, 0:8, :] = xx[L:L + 8, :]
        return _silu(y)

    gates = if_ref[0] + ifb_ref[...]
    ift_ref[...] = gates.T
    lane = lax.broadcasted_iota(jnp.int32, (1, LANES), 1)
    r_i = lax.broadcasted_iota(jnp.int32, (L, L), 0)
    c_i = lax.broadcasted_iota(jnp.int32, (L, L), 1)
    causal = c_i <= r_i

    for j in range(hps):
        q_ref, k_ref, v_ref, og_ref, z_ref = per_head_in[j]
        cwq_ref, cwk_ref, cbq_ref, cbk_ref, g_ref = per_head_par[j]
        head = pl.program_id(1) * hps + j
        q = conv_silu(q_ref, xq_ref, j, cwq_ref, cbq_ref)
        k = conv_silu(k_ref, xk_ref, j, cwk_ref, cbk_ref) * (B_HEAD_DIM ** -0.5)
        v = v_ref[0]
        i_col = jnp.sum(jnp.where(lane == head, gates, 0.0), axis=1, keepdims=True)
        f_col = jnp.sum(jnp.where(lane == B_HEADS + head, gates, 0.0), axis=1, keepdims=True)
        i_row = ift_ref[pl.ds(head, 1), :]
        f_row = ift_ref[pl.ds(B_HEADS + head, 1), :]
        lf_col = _log_sigmoid(f_col)
        lf_row = _log_sigmoid(f_row)
        b_col = jnp.sum(jnp.where(causal, lf_row, 0.0), axis=1, keepdims=True)
        b_row = jnp.sum(jnp.where(r_i <= c_i, lf_col, 0.0), axis=0, keepdims=True)
        b_last = jnp.sum(lf_row, axis=1, keepdims=True)
        m_prev = m_ref[j]
        log_d = jnp.where(causal, b_col - b_row + i_row, NEG_INF)
        log_inter = b_col + m_prev
        m_t = jnp.maximum(log_inter, jnp.max(log_d, axis=1, keepdims=True))
        w_intra = jnp.exp(log_d - m_t)
        w_inter = jnp.exp(log_inter - m_t)
        qb = q.astype(BF16)
        vb = v.astype(BF16)
        qk = _dot_nt(qb, k.astype(BF16)) * w_intra
        num = w_inter * _dot(qb, c_ref[j].astype(BF16)) + _dot(qk.astype(BF16), vb)
        den = (w_inter * jnp.sum(q * n_ref[j], axis=1, keepdims=True)
               + jnp.sum(qk, axis=1, keepdims=True))
        h = num / jnp.maximum(jnp.abs(den), jnp.exp(-m_t))
        log_g_row = b_last - b_row + i_row
        m_new = jnp.maximum(b_last + m_prev, jnp.max(log_g_row, axis=1, keepdims=True))
        w_g = jnp.exp(b_last - b_col + i_col - m_new)
        decay = jnp.exp(b_last + m_prev - m_new)
        kw = k * w_g
        c_ref[j] = decay * c_ref[j] + _dot_tn(kw.astype(BF16), vb)
        n_ref[j] = decay * n_ref[j] + jnp.sum(kw, axis=0, keepdims=True)
        m_ref[j] = m_new
        hb = _sigmoid(og_ref[0]) * h
        ms = jnp.mean(hb * hb, axis=-1, keepdims=True)
        hb = hb * lax.rsqrt(ms + EPS) * g_ref[...]
        y_ref[0, :, j * B_HEAD_DIM:(j + 1) * B_HEAD_DIM] = (hb * _silu(z_ref[0])).astype(y_ref.dtype)


def mlstm(proj3, if_bias, conv_w, conv_b, norm_g, *, L, hps):
    bsz, T, _ = proj3.shape
    H = B_HEADS
    D = B_HEAD_DIM
    assert H % hps == 0

    def col(off, j, k_half=False):
        base = off // D + (H if k_half else 0) + j
        return pl.BlockSpec((1, L, D), lambda b, h, c: (b, c, base + h * hps))

    def par(rows, j, k_half=False):
        base = (H if k_half else 0) + j
        return pl.BlockSpec((rows, D), lambda b, h, c: (0, base + h * hps))

    in_specs, args = [], []
    for j in range(hps):
        in_specs += [col(OFF_B_QK, j), col(OFF_B_QK, j, True), col(OFF_B_V, j), col(OFF_B_O, j), col(OFF_B_Z, j)]
        args += [proj3] * 5
    in_specs += [pl.BlockSpec((1, L, LANES), lambda b, h, c: (b, c, OFF_B_IF // LANES)),
                 pl.BlockSpec((1, LANES), lambda b, h, c: (0, 0))]
    args += [proj3, if_bias]
    for j in range(hps):
        in_specs += [par(CONV_WIDTH, j), par(CONV_WIDTH, j, True), par(1, j), par(1, j, True),
                     pl.BlockSpec((1, D), lambda b, h, c, j=j: (0, h * hps + j))]
        args += [conv_w, conv_w, conv_b, conv_b, norm_g]
    return pl.pallas_call(
        functools.partial(_mlstm_kernel, L=L, hps=hps),
        grid=(bsz, H // hps, T // L),
        in_specs=in_specs,
        out_specs=pl.BlockSpec((1, L, hps * D), lambda b, h, c: (b, c, h)),
        out_shape=jax.ShapeDtypeStruct((bsz, T, B_WIDTH), BF16),
        scratch_shapes=[
            pltpu.VMEM((hps, L + 8, D), F32), pltpu.VMEM((hps, L + 8, D), F32),
            pltpu.VMEM((hps, D, D), F32), pltpu.VMEM((hps, 1, D), F32), pltpu.VMEM((hps, 1, 1), F32),
            pltpu.VMEM((LANES, L), F32),
        ],
        compiler_params=_cparams(("parallel", "parallel", "arbitrary")),
        name="mlstm_chunkwise",
    )(*args)


def _alibi_np(n_heads):
    return (2.0 ** (-8.0 * np.arange(1, n_heads + 1, dtype=np.float64) / n_heads)).astype(np.float32)


def _bf16_pieces(c):
    pieces, rest = [], np.asarray(c, np.float64)
    for _ in range(N_SPLIT):
        piece = rest.astype(np.float32).astype(jnp.bfloat16).astype(np.float64)
        pieces.append(piece)
        rest = rest - piece
    return np.stack(pieces, axis=1).astype(np.float32)


def _lane_rep(v, groups, width=LANES):
    return jnp.broadcast_to(v.astype(F32).reshape(groups, -1, 1, 1), (groups, v.shape[0] // groups, 1, width))


def _block_diag2(w):
    z = jnp.zeros_like(w)
    return jnp.concatenate([jnp.concatenate([w, z], axis=-1), jnp.concatenate([z, w], axis=-1)], axis=-2)


def _mixers(x2d, bsz, T, p, layer, cfg):
    n, d = x2d.shape
    G = A_KV_HEADS
    w_packed = repack_w_in(jnp.swapaxes(p["w_in"], 1, 2), layer, tc=cfg["tc_repack"])
    proj = norm_matmul(x2d, p["norm_g"].reshape(1, d), w_packed, tm=cfg["tm_in"], tn=cfg["tn_in"])
    proj3 = proj.reshape(bsz, T, PROJ_COLS)

    nseg = T // CMP_STRIDE
    pe = jnp.stack([p["cmp_pe_k"], p["cmp_pe_v"]])
    pe = jnp.concatenate([pe, pe], axis=-1).reshape(2, CMP_BLOCK, 1, LANES)
    w1 = jnp.stack([p["cmp_w1_k"], p["cmp_w1_v"]]).reshape(2, CMP_BLOCK, HEAD_DIM, HEAD_DIM)
    w2 = jnp.stack([p["cmp_w2_k"], p["cmp_w2_v"]])
    k_cmp, v_cmp_t = nsa_compress(proj3, pe, _block_diag2(w1).astype(BF16), _block_diag2(w2).astype(BF16))

    n_sel = T // SEL_BLOCK
    assert n_sel <= MAX_SEL_BLOCKS
    qc_s, tk_s = cfg["qc_slc"], cfg["tk_slc"]
    ci = np.arange(nseg)[None, :] * CMP_STRIDE
    sj = np.arange(MAX_SEL_BLOCKS)[:, None] * SEL_BLOCK
    overlap_t = jnp.asarray(((ci < sj + SEL_BLOCK) & (ci + CMP_BLOCK > sj)).astype(np.float32), dtype=BF16)
    o_cmp, selbt, kaug, vaugt = cmp_select(proj3, k_cmp, v_cmp_t, overlap_t, qc=cfg["qc_cmp"],
                                           n_top=min(SEL_TOPK, n_sel), tk_slc=tk_s)

    sl2 = _alibi_np(A_HEADS).astype(np.float64) * LOG2E
    arows = np.zeros((A_HEADS, AUG_DEPTH - LANES - HEAD_DIM, qc_s), np.float32)
    arows[:, 0:N_SPLIT, :] = _bf16_pieces(sl2)[:, :, None]
    arows[:, N_SPLIT:2 * N_SPLIT, :] = _bf16_pieces(SEL_BLOCK * sl2)[:, :, None]
    arows = arows.reshape(G, HEADS_PER_GROUP, -1, qc_s).transpose(0, 2, 1, 3).reshape(G, -1, HEADS_PER_GROUP * qc_s)
    o_slc = slc_attention(proj3, selbt, kaug, vaugt, jnp.asarray(arows, dtype=BF16),
                          _lane_rep(jnp.asarray(sl2.astype(np.float32)), G, qc_s), qc=qc_s, tk=tk_s)

    y_a = band_attention(proj3, None, (o_cmp, o_slc), q_off=OFF_A_Q, kv_off=OFF_A_KV + 4 * LANES, z_off=OFF_A_Z,
                         qc=cfg["qc_win"], window=NSA_WINDOW)

    if_bias = jnp.concatenate([p["mlstm_i_bias"], p["mlstm_f_bias"], jnp.zeros((LANES - N_IF,), F32)]).reshape(1, LANES)
    y_b = mlstm(proj3, if_bias, p["mlstm_conv_w"], p["mlstm_conv_b"].reshape(1, -1),
                p["mlstm_norm_g"].reshape(1, -1), L=cfg["mlstm_chunk"], hps=cfg["mlstm_heads_per_step"])

    y_c = band_attention(proj3, p["swa_sinks"], None, q_off=OFF_C_Q, kv_off=OFF_C_KV, z_off=OFF_C_Z,
                         qc=cfg["qc_swa"], window=SWA_WINDOW)
    return y_a.reshape(n, A_WIDTH), y_b.reshape(n, B_WIDTH), y_c.reshape(n, C_WIDTH), (o_cmp, o_slc, selbt)


def _layer(x2d, bsz, T, p, layer, final_g, cfg):
    n, d = x2d.shape
    y_a, y_b, y_c, _ = _mixers(x2d, bsz, T, p, layer, cfg)
    w_out = p["w_out"].astype(BF16)
    g = final_g if final_g is not None else p["norm_g"]
    return out_proj(y_a, y_b, y_c, x2d,
                    w_out[:A_WIDTH], w_out[A_WIDTH:A_WIDTH + B_WIDTH], w_out[A_WIDTH + B_WIDTH:],
                    g.reshape(1, d), tm=cfg["tm_out"], final_norm=final_g is not None)


def _config(T):
    return dict(tc_repack=256, tm_in=min(1024, T), tn_in=1408, tm_out=min(512, T),
                qc_cmp=128, qc_slc=256, tk_slc=512, qc_win=128, qc_swa=min(256, T), mlstm_chunk=min(256, T), mlstm_heads_per_step=4)


def kernel(x, norm_g, w_in, w_out, cmp_pe_k, cmp_w1_k, cmp_w2_k, cmp_pe_v, cmp_w1_v, cmp_w2_v, mlstm_conv_w, mlstm_conv_b, mlstm_i_bias, mlstm_f_bias, mlstm_norm_g, swa_sinks, final_norm_g):
    bsz, T, d = x.shape
    depth = w_in.shape[0]
    cfg = _config(T)
    x2d = x.reshape(bsz * T, d)
    for l in range(depth):
        p = dict(norm_g=norm_g[l], w_in=w_in, w_out=w_out[l],
                 cmp_pe_k=cmp_pe_k[l], cmp_w1_k=cmp_w1_k[l], cmp_w2_k=cmp_w2_k[l],
                 cmp_pe_v=cmp_pe_v[l], cmp_w1_v=cmp_w1_v[l], cmp_w2_v=cmp_w2_v[l],
                 mlstm_conv_w=mlstm_conv_w[l], mlstm_conv_b=mlstm_conv_b[l],
                 mlstm_i_bias=mlstm_i_bias[l], mlstm_f_bias=mlstm_f_bias[l],
                 mlstm_norm_g=mlstm_norm_g[l], swa_sinks=swa_sinks[l])
        x2d = _layer(x2d, bsz, T, p, l, final_norm_g if l == depth - 1 else None, cfg)
    return x2d.reshape(bsz, T, d)
```

```python
import functools
import math

import numpy as np
import jax
import jax.numpy as jnp
from jax import lax
from jax.experimental import pallas as pl
from jax.experimental.pallas import tpu as pltpu

F32 = jnp.float32
BF16 = jnp.bfloat16

HEAD_DIM = 64
A_HEADS = 8
A_KV_HEADS = 2
HEADS_PER_GROUP = A_HEADS // A_KV_HEADS
GROUP_WIDTH = HEADS_PER_GROUP * HEAD_DIM
A_WIDTH = A_HEADS * HEAD_DIM
C_HEADS = 8
C_WIDTH = C_HEADS * HEAD_DIM
B_HEADS = 4
B_HEAD_DIM = 256
B_WIDTH = B_HEADS * B_HEAD_DIM
CMP_BLOCK = 32
CMP_STRIDE = 16
SEL_BLOCK = 64
SEL_TOPK = 16
SEL_LOCAL = 2
NSA_WINDOW = 512
SWA_WINDOW = 128
CONV_WIDTH = 4
NEG_INF = -1.0e30
EPS = 1.0e-6
SCALE = HEAD_DIM ** -0.5
LOG2E = 1.4426950408889634
Q_SCALE = SCALE * LOG2E
LANES = 128
MAX_SEL_BLOCKS = LANES
AUG_DEPTH = 256
VAUG_ROWS = 80
N_SPLIT = 3

IN_SPLITS = (A_WIDTH, 128, 128, 128, 128, 128, 128, A_HEADS * 3, A_WIDTH,
             2 * B_WIDTH, B_WIDTH, B_HEADS, B_HEADS, B_WIDTH, B_WIDTH,
             C_WIDTH, 128, 128, C_WIDTH)
IN_COLS = sum(IN_SPLITS)
_SRC = np.concatenate([[0], np.cumsum(IN_SPLITS)]).astype(int)
SRC_GATE, SRC_A_Z, SRC_B_IF, SRC_B_O = int(_SRC[7]), int(_SRC[8]), int(_SRC[11]), int(_SRC[13])
N_GATE = A_HEADS * 3
N_IF = 2 * B_HEADS
OFF_A_Q = 0
OFF_A_KV = 512
OFF_A_Z = 1280
OFF_B_QK = 1792
OFF_B_V = 3840
OFF_B_O = 4864
OFF_B_Z = 5888
OFF_C_Q = 6912
OFF_C_KV = 7424
OFF_C_Z = 7680
OFF_A_GATE = 8192
OFF_B_IF = 8320
PROJ_COLS = 8448
W_RUNS = ((0, OFF_A_Q, SRC_GATE), (SRC_A_Z, OFF_A_Z, SRC_B_IF - SRC_A_Z), (SRC_B_O, OFF_B_O, IN_COLS - SRC_B_O))
assert OFF_A_Z + (SRC_B_IF - SRC_A_Z) == OFF_B_O and OFF_B_O + (IN_COLS - SRC_B_O) == OFF_A_GATE

V7X_VMEM_BYTES = 64 * 1024 * 1024
VMEM_LIMIT = V7X_VMEM_BYTES * 3 // 4


def _cparams(sem):
    return pltpu.CompilerParams(dimension_semantics=sem, vmem_limit_bytes=VMEM_LIMIT)


def _dot(a, b):
    return jnp.dot(a, b, preferred_element_type=F32)


def _dot_nt(a, b):
    return lax.dot_general(a, b, (((1,), (1,)), ((), ())), preferred_element_type=F32)


def _dot_tn(a, b):
    return lax.dot_general(a, b, (((0,), (0,)), ((), ())), preferred_element_type=F32)


def _sigmoid(x):
    return 1.0 / (1.0 + jnp.exp(-x))


def _silu(x):
    return x * _sigmoid(x)


def _heads_to_rows(ot, qc):
    return jnp.concatenate([ot[:, hh * qc:(hh + 1) * qc] for hh in range(HEADS_PER_GROUP)], axis=0).T


def _repack_kernel(wt_ref, o_ref):
    for src, dst, width in W_RUNS:
        o_ref[dst:dst + width, :] = wt_ref[0, src:src + width, :].astype(BF16)
    o_ref[OFF_A_GATE:PROJ_COLS, :] = jnp.zeros((PROJ_COLS - OFF_A_GATE, o_ref.shape[1]), BF16)
    o_ref[OFF_A_GATE:OFF_A_GATE + N_GATE, :] = wt_ref[0, SRC_GATE:SRC_GATE + N_GATE, :].astype(BF16)
    o_ref[OFF_B_IF:OFF_B_IF + N_IF, :] = wt_ref[0, SRC_B_IF:SRC_B_IF + N_IF, :].astype(BF16)


def repack_w_in(w_in_t, layer, *, tc):
    d = w_in_t.shape[2]
    return pl.pallas_call(
        _repack_kernel,
        grid=(d // tc,),
        in_specs=[pl.BlockSpec((1, IN_COLS, tc), lambda i: (layer, 0, i))],
        out_specs=pl.BlockSpec((PROJ_COLS, tc), lambda i: (0, i)),
        out_shape=jax.ShapeDtypeStruct((PROJ_COLS, d), BF16),
        compiler_params=_cparams(("parallel",)),
        name="repack_w_in",
    )(w_in_t)


def _norm_matmul_kernel(x_ref, g_ref, wt_ref, o_ref, h_ref):
    @pl.when(pl.program_id(1) == 0)
    def _():
        x = x_ref[...]
        ms = jnp.mean(x * x, axis=-1, keepdims=True)
        h_ref[...] = (x * lax.rsqrt(ms + EPS) * g_ref[...]).astype(BF16)

    o_ref[...] = _dot_nt(h_ref[...], wt_ref[...])


def norm_matmul(x, g, wt, *, tm, tn):
    n, d = x.shape
    cols = wt.shape[0]
    return pl.pallas_call(
        _norm_matmul_kernel,
        grid=(n // tm, cols // tn),
        in_specs=[
            pl.BlockSpec((tm, d), lambda i, j: (i, 0)),
            pl.BlockSpec((1, d), lambda i, j: (0, 0)),
            pl.BlockSpec((tn, d), lambda i, j: (j, 0)),
        ],
        out_specs=pl.BlockSpec((tm, tn), lambda i, j: (i, j)),
        out_shape=jax.ShapeDtypeStruct((n, cols), F32),
        scratch_shapes=[pltpu.VMEM((tm, d), BF16)],
        compiler_params=_cparams(("parallel", "arbitrary")),
        name="norm_in_proj",
    )(x, g, wt)


def _out_proj_kernel(ya_ref, yb_ref, yc_ref, x_ref, wa_ref, wb_ref, wc_ref, g_ref, o_ref, *, final_norm):
    acc = _dot(ya_ref[...], wa_ref[...]) + _dot(yb_ref[...], wb_ref[...]) + _dot(yc_ref[...], wc_ref[...])
    y = x_ref[...] + acc
    if final_norm:
        ms = jnp.mean(y * y, axis=-1, keepdims=True)
        y = y * lax.rsqrt(ms + EPS) * g_ref[...]
    o_ref[...] = y


def out_proj(ya, yb, yc, x, wa, wb, wc, g, *, tm, final_norm):
    n, d = x.shape
    row = lambda w: pl.BlockSpec((tm, w), lambda i: (i, 0))
    full = lambda a: pl.BlockSpec(a.shape, lambda i: (0, 0))
    return pl.pallas_call(
        functools.partial(_out_proj_kernel, final_norm=final_norm),
        grid=(n // tm,),
        in_specs=[row(ya.shape[1]), row(yb.shape[1]), row(yc.shape[1]), row(d),
                  full(wa), full(wb), full(wc), full(g)],
        out_specs=row(d),
        out_shape=jax.ShapeDtypeStruct((n, d), F32),
        compiler_params=_cparams(("parallel",)),
        name="out_proj",
    )(ya, yb, yc, x, wa, wb, wc, g)


def _compress_kernel(k_ref, v_ref, pe_ref, w1_ref, w2_ref, kc_ref, vct_ref, shift_ref):
    nseg = k_ref.shape[1] // CMP_STRIDE
    for s, src_ref in enumerate((k_ref, v_ref)):
        u1 = jnp.zeros((nseg, LANES), F32)
        u2 = jnp.zeros((nseg, LANES), F32)
        for l in range(CMP_STRIDE):
            x = src_ref[0, pl.ds(l, nseg, stride=CMP_STRIDE), :]
            u1 = u1 + _dot((x + pe_ref[s, l]).astype(BF16), w1_ref[s, l])
            u2 = u2 + _dot((x + pe_ref[s, CMP_STRIDE + l]).astype(BF16), w1_ref[s, CMP_STRIDE + l])
        shift_ref[0:nseg, :] = u2
        shift_ref[nseg:nseg + 8, :] = jnp.zeros((8, LANES), F32)
        pre = u1 + shift_ref[1:nseg + 1, :]
        out = _dot(_silu(pre).astype(BF16), w2_ref[s])
        if s == 0:
            kc_ref[0] = out.astype(BF16)
        else:
            vct_ref[0] = out.T.astype(BF16)


def nsa_compress(proj3, pe, w1, w2):
    bsz, T, _ = proj3.shape
    nseg = T // CMP_STRIDE
    full = lambda a: pl.BlockSpec(a.shape, lambda b: (0,) * a.ndim)
    return pl.pallas_call(
        _compress_kernel,
        grid=(bsz,),
        in_specs=[pl.BlockSpec((1, T, LANES), lambda b: (b, 0, OFF_A_KV // LANES)),
                  pl.BlockSpec((1, T, LANES), lambda b: (b, 0, OFF_A_KV // LANES + 1)),
                  full(pe), full(w1), full(w2)],
        out_specs=[pl.BlockSpec((1, nseg, LANES), lambda b: (b, 0, 0)),
                   pl.BlockSpec((1, LANES, nseg), lambda b: (b, 0, 0))],
        out_shape=[jax.ShapeDtypeStruct((bsz, nseg, LANES), BF16),
                   jax.ShapeDtypeStruct((bsz, LANES, nseg), BF16)],
        scratch_shapes=[pltpu.VMEM((nseg + 8, LANES), F32)],
        compiler_params=_cparams(("parallel",)),
        name="nsa_compress",
    )(proj3, proj3, pe, w1, w2)


def _cmp_select_kernel(q_ref, kc_ref, vct_ref, ovt_ref, ksvs_ref, o_ref, selbt_ref, kaug_ref, vaugt_ref,
                       w_ref, s_ref, p_ref, *, qc, n_top, tk_slc, n_buckets):
    ncmp = kc_ref.shape[1]
    c = pl.program_id(1)
    t0 = c * qc
    t_row = t0 + lax.broadcasted_iota(jnp.int32, (1, qc), 1)
    any_vis = (t_row >= CMP_BLOCK - 1).astype(F32)
    cur = jnp.right_shift(t_row, 6)
    qt = (q_ref[0] * Q_SCALE).T.astype(BF16)
    zero = jnp.zeros((HEAD_DIM, qc), BF16)
    for h in range(A_HEADS):
        g, hh = divmod(h, HEADS_PER_GROUP)
        cols = slice(hh * qc, (hh + 1) * qc)
        w_ref[g, g * HEAD_DIM:(g + 1) * HEAD_DIM, cols] = qt[h * HEAD_DIM:(h + 1) * HEAD_DIM, :]
        w_ref[g, (1 - g) * HEAD_DIM:(2 - g) * HEAD_DIM, cols] = zero

    def attend_and_select(rc, rb):
        cmp_i = lax.broadcasted_iota(jnp.int32, (rc, 1), 0)
        vis = (cmp_i * CMP_STRIDE + (CMP_BLOCK - 1)) <= t_row
        blk = lax.broadcasted_iota(jnp.int32, (rb, qc), 0)
        valid = blk <= cur
        forced = (blk == 0) | (valid & (blk > cur - SEL_LOCAL))
        for g in range(A_KV_HEADS):
            s_ref[g, 0:rc, :] = _dot(kc_ref[0, 0:rc, :], w_ref[g])
        ovt = ovt_ref[0:rb, 0:rc]
        for g in range(A_KV_HEADS):
            psum = jnp.zeros((rc, qc), F32)
            for hh in range(HEADS_PER_GROUP):
                cols = slice(hh * qc, (hh + 1) * qc)
                s = jnp.where(vis, s_ref[g, 0:rc, cols], NEG_INF)
                e = jnp.exp2(s - jnp.max(s, axis=0, keepdims=True))
                p = e * (any_vis / jnp.sum(e, axis=0, keepdims=True))
                p_ref[g, 0:rc, cols] = p.astype(BF16)
                psum = psum + p
            ot = _dot(vct_ref[0, :, 0:rc], p_ref[g, 0:rc, :])
            o_ref[0, :, g * GROUP_WIDTH:(g + 1) * GROUP_WIDTH] = _heads_to_rows(
                ot[g * HEAD_DIM:(g + 1) * HEAD_DIM, :], qc)
            p1 = psum.astype(BF16)
            r1 = psum - p1.astype(F32)
            p2 = r1.astype(BF16)
            p3 = (r1 - p2.astype(F32)).astype(BF16)
            imp = _dot(ovt, p1) + _dot(ovt, p2) + _dot(ovt, p3)
            val = jnp.where(forced, -2.0, jnp.where(valid, imp, -1.0))
            for _ in range(n_top - (SEL_LOCAL + 1)):
                mx = jnp.max(val, axis=0, keepdims=True)
                first = jnp.min(jnp.where(val == mx, blk, MAX_SEL_BLOCKS), axis=0, keepdims=True)
                val = jnp.where(blk == first, -2.0, val)
            chosen = (val == -2.0) & (blk < cur)
            selbt_ref[0, g, 0:rb, :] = jnp.where(chosen, 0.0, NEG_INF).astype(BF16)
            if rb < MAX_SEL_BLOCKS:
                selbt_ref[0, g, rb:MAX_SEL_BLOCKS, :] = jnp.full((MAX_SEL_BLOCKS - rb, qc), NEG_INF, BF16)

    chunks_per_bucket = (ncmp // n_buckets) * CMP_STRIDE // qc
    for b in range(n_buckets):
        @pl.when((c >= b * chunks_per_bucket) & (c < (b + 1) * chunks_per_bucket))
        def _(b=b):
            attend_and_select((b + 1) * (ncmp // n_buckets), (b + 1) * (MAX_SEL_BLOCKS // n_buckets))

    kv = ksvs_ref[0]
    pos = t0 + lax.broadcasted_iota(jnp.int32, (qc, 1), 0)
    lane = lax.broadcasted_iota(jnp.int32, (1, LANES), 1)
    onehot = jnp.where(jnp.right_shift(pos, 6) == lane, 1.0, 0.0).astype(BF16)
    flane = lax.broadcasted_iota(jnp.int32, (1, AUG_DEPTH - LANES - HEAD_DIM), 1)
    in_block = jnp.bitwise_and(pos, SEL_BLOCK - 1).astype(F32)
    in_tile = jnp.bitwise_and(jnp.right_shift(pos, 6), tk_slc // SEL_BLOCK - 1).astype(F32)
    feat = jnp.where(flane < N_SPLIT, in_block, jnp.where(flane < 2 * N_SPLIT, in_tile, 0.0)).astype(BF16)
    vt = kv[:, LANES:2 * LANES].T.astype(BF16)
    ones_row = (lax.broadcasted_iota(jnp.int32, (VAUG_ROWS - HEAD_DIM, qc), 0) == 0).astype(BF16)
    for g in range(A_KV_HEADS):
        kaug_ref[0, g, :, 0:LANES] = onehot
        kaug_ref[0, g, :, LANES:LANES + HEAD_DIM] = kv[:, g * HEAD_DIM:(g + 1) * HEAD_DIM].astype(BF16)
        kaug_ref[0, g, :, LANES + HEAD_DIM:AUG_DEPTH] = feat
        vaugt_ref[0, g, 0:HEAD_DIM, :] = vt[g * HEAD_DIM:(g + 1) * HEAD_DIM, :]
        vaugt_ref[0, g, HEAD_DIM:VAUG_ROWS, :] = ones_row


def cmp_select(proj3, kcmp, vcmp_t, overlap_t, *, qc, n_top, tk_slc):
    bsz, T, _ = proj3.shape
    ncmp = kcmp.shape[1]
    G = A_KV_HEADS
    assert n_top > SEL_LOCAL + 1 and (tk_slc // SEL_BLOCK) & (tk_slc // SEL_BLOCK - 1) == 0
    n_buckets = max(1, min(MAX_SEL_BLOCKS // (2 * n_top), ncmp // LANES))
    assert ncmp % (n_buckets * LANES) == 0 and ncmp == T // CMP_STRIDE and MAX_SEL_BLOCKS // n_buckets > n_top
    return pl.pallas_call(
        functools.partial(_cmp_select_kernel, qc=qc, n_top=n_top, tk_slc=tk_slc, n_buckets=n_buckets),
        grid=(bsz, T // qc),
        in_specs=[
            pl.BlockSpec((1, qc, A_WIDTH), lambda b, c: (b, c, OFF_A_Q // A_WIDTH)),
            pl.BlockSpec((1, ncmp, LANES), lambda b, c: (b, 0, 0)),
            pl.BlockSpec((1, LANES, ncmp), lambda b, c: (b, 0, 0)),
            pl.BlockSpec((MAX_SEL_BLOCKS, ncmp), lambda b, c: (0, 0)),
            pl.BlockSpec((1, qc, 2 * LANES), lambda b, c: (b, c, (OFF_A_KV + 2 * LANES) // (2 * LANES))),
        ],
        out_specs=[
            pl.BlockSpec((1, qc, A_WIDTH), lambda b, c: (b, c, 0)),
            pl.BlockSpec((1, G, MAX_SEL_BLOCKS, qc), lambda b, c: (b, 0, 0, c)),
            pl.BlockSpec((1, G, qc, AUG_DEPTH), lambda b, c: (b, 0, c, 0)),
            pl.BlockSpec((1, G, VAUG_ROWS, qc), lambda b, c: (b, 0, 0, c)),
        ],
        out_shape=[jax.ShapeDtypeStruct((bsz, T, A_WIDTH), F32),
                   jax.ShapeDtypeStruct((bsz, G, MAX_SEL_BLOCKS, T), BF16),
                   jax.ShapeDtypeStruct((bsz, G, T, AUG_DEPTH), BF16),
                   jax.ShapeDtypeStruct((bsz, G, VAUG_ROWS, T), BF16)],
        scratch_shapes=[
            pltpu.VMEM((G, LANES, HEADS_PER_GROUP * qc), BF16),
            pltpu.VMEM((G, ncmp, HEADS_PER_GROUP * qc), F32),
            pltpu.VMEM((G, ncmp, HEADS_PER_GROUP * qc), BF16),
        ],
        compiler_params=_cparams(("parallel", "parallel")),
        name="nsa_cmp_select",
    )(proj3, kcmp, vcmp_t, overlap_t, proj3)


def _slc_kernel(q_ref, selbt_ref, kaug_ref, vaugt_ref, arow_ref, slope_ref, dbias_ref, o_ref,
                qaug_ref, m_ref, acc_ref, *s_refs, qc, tk):
    t0 = pl.multiple_of(pl.program_id(2) * qc, qc)
    selbt = selbt_ref[0, 0]
    qt = (q_ref[0] * Q_SCALE).T.astype(BF16)
    for hh in range(HEADS_PER_GROUP):
        cols = slice(hh * qc, (hh + 1) * qc)
        qaug_ref[0:LANES, cols] = selbt
        qaug_ref[LANES:LANES + HEAD_DIM, cols] = qt[hh * HEAD_DIM:(hh + 1) * HEAD_DIM, :]
    qaug_ref[LANES + HEAD_DIM:AUG_DEPTH, :] = arow_ref[0]
    m_ref[...] = jnp.full(m_ref.shape, NEG_INF, F32)
    acc_ref[...] = jnp.zeros(acc_ref.shape, F32)

    bufs = (s_refs[:HEADS_PER_GROUP], s_refs[HEADS_PER_GROUP:])

    def scores(kt, hh, dst):
        s0 = pl.multiple_of(kt * tk, tk)
        dst[hh][...] = _dot(kaug_ref[0, 0, pl.ds(s0, tk), :], qaug_ref[:, hh * qc:(hh + 1) * qc])

    def diag_scores(hh, dst):
        ka = kaug_ref[0, 0, pl.ds(t0, qc), LANES:AUG_DEPTH]
        st = _dot(ka, qaug_ref[LANES:AUG_DEPTH, hh * qc:(hh + 1) * qc])
        dst[hh][0:qc, :] = st + dbias_ref[...]

    def softmax_pv(hh, src, rows, vt, off):
        c = slope_ref[0, hh] * off
        m_old = m_ref[hh]
        m_new = jnp.maximum(m_old, jnp.max(src[hh][0:rows, :], axis=0, keepdims=True) + c)
        alpha = jnp.exp2(m_old - m_new)
        p = jnp.exp2(src[hh][0:rows, :] - (m_new - c)).astype(BF16)
        acc_ref[hh] = alpha * acc_ref[hh] + _dot(vt, p)
        m_ref[hh] = m_new

    def step(kt, src, next_scores):
        s0 = pl.multiple_of(kt * tk, tk)
        vt = vaugt_ref[0, 0, :, pl.ds(s0, tk)]
        off = (s0 - t0).astype(F32)
        for hh in range(HEADS_PER_GROUP):
            next_scores(hh)
            softmax_pv(hh, src, tk, vt, off)

    def main_step(kt, src, dst):
        step(kt, src, lambda hh: scores(kt + 1, hh, dst))

    n_full = lax.div(t0, tk)
    odd = lax.rem(n_full, 2)

    @pl.when(odd == 0)
    def _():
        for hh in range(HEADS_PER_GROUP):
            scores(0, hh, bufs[0])

    @pl.when(odd == 1)
    def _():
        for hh in range(HEADS_PER_GROUP):
            scores(0, hh, bufs[1])
        main_step(0, bufs[1], bufs[0])

    def body(i, carry):
        kt = odd + 2 * i
        main_step(kt, bufs[0], bufs[1])
        main_step(kt + 1, bufs[1], bufs[0])
        return carry

    lax.fori_loop(0, lax.div(n_full, 2), body, 0)
    step(n_full, bufs[0], lambda hh: diag_scores(hh, bufs[1]))
    base = pl.multiple_of(n_full * tk, tk)
    vt_d = vaugt_ref[0, 0, :, pl.ds(t0, qc)]
    for hh in range(HEADS_PER_GROUP):
        softmax_pv(hh, bufs[1], qc, vt_d, (base - t0).astype(F32))
    ot = jnp.concatenate([acc_ref[hh, 0:HEAD_DIM, :] / acc_ref[hh, HEAD_DIM:HEAD_DIM + 1, :]
                          for hh in range(HEADS_PER_GROUP)], axis=0)
    o_ref[0] = ot.T


def slc_attention(proj3, selbt, kaug, vaugt, arows, slopes, *, qc, tk):
    bsz, T, _ = proj3.shape
    G = A_KV_HEADS
    assert tk % qc == 0 and qc % SEL_BLOCK == 0
    ki = np.arange(qc)[:, None]
    qi = np.arange(qc)[None, :]
    dbias = jnp.asarray(np.where((ki // SEL_BLOCK == qi // SEL_BLOCK) & (ki <= qi), 0.0, NEG_INF), dtype=F32)
    return pl.pallas_call(
        functools.partial(_slc_kernel, qc=qc, tk=tk),
        grid=(bsz, G, T // qc),
        in_specs=[
            pl.BlockSpec((1, qc, GROUP_WIDTH), lambda b, g, c: (b, c, OFF_A_Q // GROUP_WIDTH + g)),
            pl.BlockSpec((1, 1, LANES, qc), lambda b, g, c: (b, g, 0, c)),
            pl.BlockSpec((1, 1, T, AUG_DEPTH), lambda b, g, c: (b, g, 0, 0)),
            pl.BlockSpec((1, 1, VAUG_ROWS, T), lambda b, g, c: (b, g, 0, 0)),
            pl.BlockSpec((1, AUG_DEPTH - LANES - HEAD_DIM, HEADS_PER_GROUP * qc), lambda b, g, c: (g, 0, 0)),
            pl.BlockSpec((1, HEADS_PER_GROUP, 1, qc), lambda b, g, c: (g, 0, 0, 0)),
            pl.BlockSpec((qc, qc), lambda b, g, c: (0, 0)),
        ],
        out_specs=pl.BlockSpec((1, qc, GROUP_WIDTH), lambda b, g, c: (b, c, g)),
        out_shape=jax.ShapeDtypeStruct((bsz, T, A_WIDTH), F32),
        scratch_shapes=[
            pltpu.VMEM((AUG_DEPTH, HEADS_PER_GROUP * qc), BF16),
            pltpu.VMEM((HEADS_PER_GROUP, 1, qc), F32),
            pltpu.VMEM((HEADS_PER_GROUP, VAUG_ROWS, qc), F32),
        ] + [pltpu.VMEM((tk, qc), F32)] * (2 * HEADS_PER_GROUP),
        compiler_params=_cparams(("parallel", "parallel", "arbitrary")),
        name="nsa_slc_attention",
    )(proj3, selbt, kaug, vaugt, arows, slopes, dbias)


def _band_kernel(*refs, qc, window, has_sinks, mix):
    q_refs, kv_ref, bias_ref = refs[0:2], refs[2], refs[3]
    pos = 4
    sink_ref = None
    if has_sinks:
        sink_ref = refs[pos]
        pos += 1
    z_refs = refs[pos:pos + 2]
    pos += 2
    if mix:
        ocmp_ref, oslc_ref, gate_ref = refs[pos:pos + 3]
        pos += 3
    o_ref, w_ref, kwin_ref, vt_ref, p_ref = refs[pos:pos + 5]
    s_refs = refs[pos + 5:pos + 7]
    G = len(q_refs)
    span = qc + window
    lanes = HEADS_PER_GROUP * qc
    n_pad_chunks = -(-window // qc)
    c = pl.program_id(1)
    ones_row = (lax.broadcasted_iota(jnp.int32, (VAUG_ROWS - HEAD_DIM, span), 0) == 0).astype(BF16)

    def assemble(kv):
        kwin_ref[...] = kv[:, 0:LANES].astype(BF16)
        vt = kv[:, LANES:2 * LANES].T.astype(BF16)
        for g in range(G):
            vt_ref[g, 0:HEAD_DIM, :] = vt[g * HEAD_DIM:(g + 1) * HEAD_DIM, :]
            vt_ref[g, HEAD_DIM:VAUG_ROWS, :] = ones_row

    @pl.when(c >= n_pad_chunks)
    def _():
        start = pl.multiple_of(c * qc - window, math.gcd(qc, window))
        assemble(kv_ref[0, pl.ds(start, span), :])

    for j in range(n_pad_chunks):
        @pl.when(c == j)
        def _(j=j):
            n_pad = window - j * qc
            assemble(jnp.concatenate([jnp.zeros((n_pad, 2 * LANES), F32), kv_ref[0, 0:span - n_pad, :]], axis=0))

    zero = jnp.zeros((HEAD_DIM, qc), BF16)
    for g in range(G):
        qt = (q_refs[g][0] * Q_SCALE).T.astype(BF16)
        for hh in range(HEADS_PER_GROUP):
            cols = slice(hh * qc, (hh + 1) * qc)
            w_ref[g, g * HEAD_DIM:(g + 1) * HEAD_DIM, cols] = qt[hh * HEAD_DIM:(hh + 1) * HEAD_DIM, :]
            w_ref[g, (1 - g) * HEAD_DIM:(2 - g) * HEAD_DIM, cols] = zero
    pieces = [slice(r, r + LANES) for r in range(0, span, LANES)]
    for g in range(G):
        for rows in pieces:
            s_refs[g][rows, :] = _dot(kwin_ref[rows, :], w_ref[g])
    if mix:
        gate = _sigmoid(gate_ref[0])
    for g in range(G):
        mx8 = None
        for rows in pieces:
            t = s_refs[g][rows, :] + bias_ref[0, rows, g * lanes:(g + 1) * lanes]
            s_refs[g][rows, :] = t
            m8 = jnp.max(t.reshape(LANES // 8, 8, lanes), axis=0)
            mx8 = m8 if mx8 is None else jnp.maximum(mx8, m8)
        mx = jnp.max(mx8, axis=0, keepdims=True)
        if has_sinks:
            sk = sink_ref[:, g * lanes:(g + 1) * lanes]
            mx = jnp.maximum(mx, sk)
        for rows in pieces:
            p_ref[rows, :] = jnp.exp2(s_refs[g][rows, :] - mx).astype(BF16)
        ot = _dot(vt_ref[g], p_ref[...])
        den = ot[HEAD_DIM:HEAD_DIM + 1, :]
        if has_sinks:
            den = den + jnp.exp2(sk - mx)
        o = _heads_to_rows(ot[0:HEAD_DIM, :] / den, qc)
        z = z_refs[g][0]
        for hh in range(HEADS_PER_GROUP):
            h = g * HEADS_PER_GROUP + hh
            loc = slice(hh * HEAD_DIM, (hh + 1) * HEAD_DIM)
            glob = slice(h * HEAD_DIM, (h + 1) * HEAD_DIM)
            oh = o[:, loc]
            if mix:
                oh = (gate[:, 3 * h:3 * h + 1] * ocmp_ref[0, :, glob]
                      + gate[:, 3 * h + 1:3 * h + 2] * oslc_ref[0, :, glob]
                      + gate[:, 3 * h + 2:3 * h + 3] * oh)
            o_ref[0, :, glob] = (oh * _silu(z[:, loc])).astype(o_ref.dtype)


def _band_bias(qc, window, n_heads):
    span = qc + window
    n_var = -(-window // qc) + 1
    slopes = jnp.asarray(_alibi_np(n_heads)) * LOG2E
    row = lax.broadcasted_iota(jnp.int32, (span, qc), 0)
    dist = lax.broadcasted_iota(jnp.int32, (span, qc), 1) + window - row
    band = (dist >= 0) & (dist < window)
    term = -(slopes[:, None, None] * dist.astype(F32)[None])
    n_pad = window - jnp.arange(n_var, dtype=jnp.int32)[:, None, None, None] * qc
    ok = band[None, None] & (row[None, None] >= n_pad)
    bias = jnp.where(ok, term[None], NEG_INF)
    return bias.transpose(0, 2, 1, 3).reshape(n_var, span, n_heads * qc)


def band_attention(proj3, sinks, mix_in, *, q_off, kv_off, z_off, qc, window):
    bsz, T, _ = proj3.shape
    G = 2
    n_heads = G * HEADS_PER_GROUP
    span = qc + window
    assert span % LANES == 0 and math.gcd(qc, window) % 8 == 0 and T >= span
    has_sinks = sinks is not None
    mix = mix_in is not None
    bias = _band_bias(qc, window, n_heads)
    n_var = bias.shape[0]
    grp = lambda off: [pl.BlockSpec((1, qc, GROUP_WIDTH), lambda b, c, g=g: (b, c, off // GROUP_WIDTH + g))
                       for g in range(G)]
    in_specs = grp(q_off) + [
        pl.BlockSpec((1, T, 2 * LANES), lambda b, c: (b, 0, kv_off // (2 * LANES))),
        pl.BlockSpec((1, span, n_heads * qc), lambda b, c: (jnp.minimum(c, n_var - 1), 0, 0)),
    ]
    args = [proj3, proj3, proj3, bias]
    if has_sinks:
        in_specs.append(pl.BlockSpec((1, n_heads * qc), lambda b, c: (0, 0)))
        args.append(jnp.repeat(sinks.astype(F32) * LOG2E, qc).reshape(1, n_heads * qc))
    in_specs += grp(z_off)
    args += [proj3, proj3]
    if mix:
        full = pl.BlockSpec((1, qc, A_WIDTH), lambda b, c: (b, c, 0))
        in_specs += [full, full, pl.BlockSpec((1, qc, LANES), lambda b, c: (b, c, OFF_A_GATE // LANES))]
        args += [mix_in[0], mix_in[1], proj3]
    return pl.pallas_call(
        functools.partial(_band_kernel, qc=qc, window=window, has_sinks=has_sinks, mix=mix),
        grid=(bsz, T // qc),
        in_specs=in_specs,
        out_specs=pl.BlockSpec((1, qc, G * GROUP_WIDTH), lambda b, c: (b, c, 0)),
        out_shape=jax.ShapeDtypeStruct((bsz, T, G * GROUP_WIDTH), BF16),
        scratch_shapes=[
            pltpu.VMEM((G, LANES, HEADS_PER_GROUP * qc), BF16),
            pltpu.VMEM((span, LANES), BF16),
            pltpu.VMEM((G, VAUG_ROWS, span), BF16),
            pltpu.VMEM((span, HEADS_PER_GROUP * qc), BF16),
        ] + [pltpu.VMEM((span, HEADS_PER_GROUP * qc), F32)] * G,
        compiler_params=_cparams(("parallel", "parallel")),
        name="band_attention_w%d" % window,
    )(*args)


def _log_sigmoid(x):
    return jnp.minimum(x, 0.0) - jnp.log(1.0 + jnp.exp(-jnp.abs(x)))


def _mlstm_kernel(*refs, L, hps):
    per_head_in = [refs[5 * j:5 * j + 5] for j in range(hps)]
    if_ref, ifb_ref = refs[5 * hps:5 * hps + 2]
    base = 5 * hps + 2
    per_head_par = [refs[base + 5 * j:base + 5 * j + 5] for j in range(hps)]
    y_ref, xq_ref, xk_ref, c_ref, n_ref, m_ref, ift_ref = refs[base + 5 * hps:]

    @pl.when(pl.program_id(2) == 0)
    def _():
        xq_ref[:, 0:8, :] = jnp.zeros((hps, 8, B_HEAD_DIM), F32)
        xk_ref[:, 0:8, :] = jnp.zeros((hps, 8, B_HEAD_DIM), F32)
        c_ref[...] = jnp.zeros(c_ref.shape, F32)
        n_ref[...] = jnp.zeros(n_ref.shape, F32)
        m_ref[...] = jnp.zeros(m_ref.shape, F32)

    def conv_silu(x_ref, hist_ref, j, w_ref, b_ref):
        hist_ref[j, 8:8 + L, :] = x_ref[0]
        xx = hist_ref[j]
        y = b_ref[...] + w_ref[CONV_WIDTH - 1:CONV_WIDTH, :] * xx[8:8 + L, :]
        for back in range(1, CONV_WIDTH):
            tap = CONV_WIDTH - 1 - back
            y = y + w_ref[tap:tap + 1, :] * pltpu.roll(xx, back, axis=0)[8:8 + L, :]
        hist_ref[j, 0:8, :] = xx[L:L + 8, :]
        return _silu(y)

    gates = if_ref[0] + ifb_ref[...]
    ift_ref[...] = gates.T
    lane = lax.broadcasted_iota(jnp.int32, (1, LANES), 1)
    r_i = lax.broadcasted_iota(jnp.int32, (L, L), 0)
    c_i = lax.broadcasted_iota(jnp.int32, (L, L), 1)
    causal = c_i <= r_i

    for j in range(hps):
        q_ref, k_ref, v_ref, og_ref, z_ref = per_head_in[j]
        cwq_ref, cwk_ref, cbq_ref, cbk_ref, g_ref = per_head_par[j]
        head = pl.program_id(1) * hps + j
        q = conv_silu(q_ref, xq_ref, j, cwq_ref, cbq_ref)
        k = conv_silu(k_ref, xk_ref, j, cwk_ref, cbk_ref) * (B_HEAD_DIM ** -0.5)
        v = v_ref[0]
        i_col = jnp.sum(jnp.where(lane == head, gates, 0.0), axis=1, keepdims=True)
        f_col = jnp.sum(jnp.where(lane == B_HEADS + head, gates, 0.0), axis=1, keepdims=True)
        i_row = ift_ref[pl.ds(head, 1), :]
        f_row = ift_ref[pl.ds(B_HEADS + head, 1), :]
        lf_col = _log_sigmoid(f_col)
        lf_row = _log_sigmoid(f_row)
        b_col = jnp.sum(jnp.where(causal, lf_row, 0.0), axis=1, keepdims=True)
        b_row = jnp.sum(jnp.where(r_i <= c_i, lf_col, 0.0), axis=0, keepdims=True)
        b_last = jnp.sum(lf_row, axis=1, keepdims=True)
        m_prev = m_ref[j]
        log_d = jnp.where(causal, b_col - b_row + i_row, NEG_INF)
        log_inter = b_col + m_prev
        m_t = jnp.maximum(log_inter, jnp.max(log_d, axis=1, keepdims=True))
        w_intra = jnp.exp(log_d - m_t)
        w_inter = jnp.exp(log_inter - m_t)
        qb = q.astype(BF16)
        vb = v.astype(BF16)
        qk = _dot_nt(qb, k.astype(BF16)) * w_intra
        num = w_inter * _dot(qb, c_ref[j].astype(BF16)) + _dot(qk.astype(BF16), vb)
        den = (w_inter * jnp.sum(q * n_ref[j], axis=1, keepdims=True)
               + jnp.sum(qk, axis=1, keepdims=True))
        h = num / jnp.maximum(jnp.abs(den), jnp.exp(-m_t))
        log_g_row = b_last - b_row + i_row
        m_new = jnp.maximum(b_last + m_prev, jnp.max(log_g_row, axis=1, keepdims=True))
        w_g = jnp.exp(b_last - b_col + i_col - m_new)
        decay = jnp.exp(b_last + m_prev - m_new)
        kw = k * w_g
        c_ref[j] = decay * c_ref[j] + _dot_tn(kw.astype(BF16), vb)
        n_ref[j] = decay * n_ref[j] + jnp.sum(kw, axis=0, keepdims=True)
        m_ref[j] = m_new
        hb = _sigmoid(og_ref[0]) * h
        ms = jnp.mean(hb * hb, axis=-1, keepdims=True)
        hb = hb * lax.rsqrt(ms + EPS) * g_ref[...]
        y_ref[0, :, j * B_HEAD_DIM:(j + 1) * B_HEAD_DIM] = (hb * _silu(z_ref[0])).astype(y_ref.dtype)


def mlstm(proj3, if_bias, conv_w, conv_b, norm_g, *, L, hps):
    bsz, T, _ = proj3.shape
    H = B_HEADS
    D = B_HEAD_DIM
    assert H % hps == 0

    def col(off, j, k_half=False):
        base = off // D + (H if k_half else 0) + j
        return pl.BlockSpec((1, L, D), lambda b, h, c: (b, c, base + h * hps))

    def par(rows, j, k_half=False):
        base = (H if k_half else 0) + j
        return pl.BlockSpec((rows, D), lambda b, h, c: (0, base + h * hps))

    in_specs, args = [], []
    for j in range(hps):
        in_specs += [col(OFF_B_QK, j), col(OFF_B_QK, j, True), col(OFF_B_V, j), col(OFF_B_O, j), col(OFF_B_Z, j)]
        args += [proj3] * 5
    in_specs += [pl.BlockSpec((1, L, LANES), lambda b, h, c: (b, c, OFF_B_IF // LANES)),
                 pl.BlockSpec((1, LANES), lambda b, h, c: (0, 0))]
    args += [proj3, if_bias]
    for j in range(hps):
        in_specs += [par(CONV_WIDTH, j), par(CONV_WIDTH, j, True), par(1, j), par(1, j, True),
                     pl.BlockSpec((1, D), lambda b, h, c, j=j: (0, h * hps + j))]
        args += [conv_w, conv_w, conv_b, conv_b, norm_g]
    return pl.pallas_call(
        functools.partial(_mlstm_kernel, L=L, hps=hps),
        grid=(bsz, H // hps, T // L),
        in_specs=in_specs,
        out_specs=pl.BlockSpec((1, L, hps * D), lambda b, h, c: (b, c, h)),
        out_shape=jax.ShapeDtypeStruct((bsz, T, B_WIDTH), BF16),
        scratch_shapes=[
            pltpu.VMEM((hps, L + 8, D), F32), pltpu.VMEM((hps, L + 8, D), F32),
            pltpu.VMEM((hps, D, D), F32), pltpu.VMEM((hps, 1, D), F32), pltpu.VMEM((hps, 1, 1), F32),
            pltpu.VMEM((LANES, L), F32),
        ],
        compiler_params=_cparams(("parallel", "parallel", "arbitrary")),
        name="mlstm_chunkwise",
    )(*args)


def _alibi_np(n_heads):
    return (2.0 ** (-8.0 * np.arange(1, n_heads + 1, dtype=np.float64) / n_heads)).astype(np.float32)


def _bf16_pieces(c):
    pieces, rest = [], np.asarray(c, np.float64)
    for _ in range(N_SPLIT):
        piece = rest.astype(np.float32).astype(jnp.bfloat16).astype(np.float64)
        pieces.append(piece)
        rest = rest - piece
    return np.stack(pieces, axis=1).astype(np.float32)


def _lane_rep(v, groups, width=LANES):
    return jnp.broadcast_to(v.astype(F32).reshape(groups, -1, 1, 1), (groups, v.shape[0] // groups, 1, width))


def _block_diag2(w):
    z = jnp.zeros_like(w)
    return jnp.concatenate([jnp.concatenate([w, z], axis=-1), jnp.concatenate([z, w], axis=-1)], axis=-2)


def _mixers(x2d, bsz, T, p, layer, cfg):
    n, d = x2d.shape
    G = A_KV_HEADS
    w_packed = repack_w_in(jnp.swapaxes(p["w_in"], 1, 2), layer, tc=cfg["tc_repack"])
    proj = norm_matmul(x2d, p["norm_g"].reshape(1, d), w_packed, tm=cfg["tm_in"], tn=cfg["tn_in"])
    proj3 = proj.reshape(bsz, T, PROJ_COLS)

    nseg = T // CMP_STRIDE
    pe = jnp.stack([p["cmp_pe_k"], p["cmp_pe_v"]])
    pe = jnp.concatenate([pe, pe], axis=-1).reshape(2, CMP_BLOCK, 1, LANES)
    w1 = jnp.stack([p["cmp_w1_k"], p["cmp_w1_v"]]).reshape(2, CMP_BLOCK, HEAD_DIM, HEAD_DIM)
    w2 = jnp.stack([p["cmp_w2_k"], p["cmp_w2_v"]])
    k_cmp, v_cmp_t = nsa_compress(proj3, pe, _block_diag2(w1).astype(BF16), _block_diag2(w2).astype(BF16))

    n_sel = T // SEL_BLOCK
    assert n_sel <= MAX_SEL_BLOCKS
    qc_s, tk_s = cfg["qc_slc"], cfg["tk_slc"]
    ci = np.arange(nseg)[None, :] * CMP_STRIDE
    sj = np.arange(MAX_SEL_BLOCKS)[:, None] * SEL_BLOCK
    overlap_t = jnp.asarray(((ci < sj + SEL_BLOCK) & (ci + CMP_BLOCK > sj)).astype(np.float32), dtype=BF16)
    o_cmp, selbt, kaug, vaugt = cmp_select(proj3, k_cmp, v_cmp_t, overlap_t, qc=cfg["qc_cmp"],
                                           n_top=min(SEL_TOPK, n_sel), tk_slc=tk_s)

    sl2 = _alibi_np(A_HEADS).astype(np.float64) * LOG2E
    arows = np.zeros((A_HEADS, AUG_DEPTH - LANES - HEAD_DIM, qc_s), np.float32)
    arows[:, 0:N_SPLIT, :] = _bf16_pieces(sl2)[:, :, None]
    arows[:, N_SPLIT:2 * N_SPLIT, :] = _bf16_pieces(SEL_BLOCK * sl2)[:, :, None]
    arows = arows.reshape(G, HEADS_PER_GROUP, -1, qc_s).transpose(0, 2, 1, 3).reshape(G, -1, HEADS_PER_GROUP * qc_s)
    o_slc = slc_attention(proj3, selbt, kaug, vaugt, jnp.asarray(arows, dtype=BF16),
                          _lane_rep(jnp.asarray(sl2.astype(np.float32)), G, qc_s), qc=qc_s, tk=tk_s)

    y_a = band_attention(proj3, None, (o_cmp, o_slc), q_off=OFF_A_Q, kv_off=OFF_A_KV + 4 * LANES, z_off=OFF_A_Z,
                         qc=cfg["qc_win"], window=NSA_WINDOW)

    if_bias = jnp.concatenate([p["mlstm_i_bias"], p["mlstm_f_bias"], jnp.zeros((LANES - N_IF,), F32)]).reshape(1, LANES)
    y_b = mlstm(proj3, if_bias, p["mlstm_conv_w"], p["mlstm_conv_b"].reshape(1, -1),
                p["mlstm_norm_g"].reshape(1, -1), L=cfg["mlstm_chunk"], hps=cfg["mlstm_heads_per_step"])

    y_c = band_attention(proj3, p["swa_sinks"], None, q_off=OFF_C_Q, kv_off=OFF_C_KV, z_off=OFF_C_Z,
                         qc=cfg["qc_swa"], window=SWA_WINDOW)
    return y_a.reshape(n, A_WIDTH), y_b.reshape(n, B_WIDTH), y_c.reshape(n, C_WIDTH), (o_cmp, o_slc, selbt)


def _layer(x2d, bsz, T, p, layer, final_g, cfg):
    n, d = x2d.shape
    y_a, y_b, y_c, _ = _mixers(x2d, bsz, T, p, layer, cfg)
    w_out = p["w_out"].astype(BF16)
    g = final_g if final_g is not None else p["norm_g"]
    return out_proj(y_a, y_b, y_c, x2d,
                    w_out[:A_WIDTH], w_out[A_WIDTH:A_WIDTH + B_WIDTH], w_out[A_WIDTH + B_WIDTH:],
                    g.reshape(1, d), tm=cfg["tm_out"], final_norm=final_g is not None)


def _config(T):
    return dict(tc_repack=256, tm_in=min(1024, T), tn_in=1408, tm_out=min(512, T),
                qc_cmp=128, qc_slc=512, tk_slc=512, qc_win=128, qc_swa=min(256, T), mlstm_chunk=min(256, T), mlstm_heads_per_step=4)


def kernel(x, norm_g, w_in, w_out, cmp_pe_k, cmp_w1_k, cmp_w2_k, cmp_pe_v, cmp_w1_v, cmp_w2_v, mlstm_conv_w, mlstm_conv_b, mlstm_i_bias, mlstm_f_bias, mlstm_norm_g, swa_sinks, final_norm_g):
    bsz, T, d = x.shape
    depth = w_in.shape[0]
    cfg = _config(T)
    x2d = x.reshape(bsz * T, d)
    for l in range(depth):
        p = dict(norm_g=norm_g[l], w_in=w_in, w_out=w_out[l],
                 cmp_pe_k=cmp_pe_k[l], cmp_w1_k=cmp_w1_k[l], cmp_w2_k=cmp_w2_k[l],
                 cmp_pe_v=cmp_pe_v[l], cmp_w1_v=cmp_w1_v[l], cmp_w2_v=cmp_w2_v[l],
                 mlstm_conv_w=mlstm_conv_w[l], mlstm_conv_b=mlstm_conv_b[l],
                 mlstm_i_bias=mlstm_i_bias[l], mlstm_f_bias=mlstm_f_bias[l],
                 mlstm_norm_g=mlstm_norm_g[l], swa_sinks=swa_sinks[l])
        x2d = _layer(x2d, bsz, T, p, l, final_norm_g if l == depth - 1 else None, cfg)
    return x2d.reshape(bsz, T, d)
```

```python
import functools
import math

import numpy as np
import jax
import jax.numpy as jnp
from jax import lax
from jax.experimental import pallas as pl
from jax.experimental.pallas import tpu as pltpu

F32 = jnp.float32
BF16 = jnp.bfloat16

HEAD_DIM = 64
A_HEADS = 8
A_KV_HEADS = 2
HEADS_PER_GROUP = A_HEADS // A_KV_HEADS
GROUP_WIDTH = HEADS_PER_GROUP * HEAD_DIM
A_WIDTH = A_HEADS * HEAD_DIM
C_HEADS = 8
C_WIDTH = C_HEADS * HEAD_DIM
B_HEADS = 4
B_HEAD_DIM = 256
B_WIDTH = B_HEADS * B_HEAD_DIM
CMP_BLOCK = 32
CMP_STRIDE = 16
SEL_BLOCK = 64
SEL_TOPK = 16
SEL_LOCAL = 2
NSA_WINDOW = 512
SWA_WINDOW = 128
CONV_WIDTH = 4
NEG_INF = -1.0e30
EPS = 1.0e-6
SCALE = HEAD_DIM ** -0.5
LOG2E = 1.4426950408889634
Q_SCALE = SCALE * LOG2E
LANES = 128
MAX_SEL_BLOCKS = LANES
AUG_DEPTH = 256
VAUG_ROWS = 80
N_SPLIT = 3

IN_SPLITS = (A_WIDTH, 128, 128, 128, 128, 128, 128, A_HEADS * 3, A_WIDTH,
             2 * B_WIDTH, B_WIDTH, B_HEADS, B_HEADS, B_WIDTH, B_WIDTH,
             C_WIDTH, 128, 128, C_WIDTH)
IN_COLS = sum(IN_SPLITS)
_SRC = np.concatenate([[0], np.cumsum(IN_SPLITS)]).astype(int)
SRC_GATE, SRC_A_Z, SRC_B_IF, SRC_B_O = int(_SRC[7]), int(_SRC[8]), int(_SRC[11]), int(_SRC[13])
N_GATE = A_HEADS * 3
N_IF = 2 * B_HEADS
OFF_A_Q = 0
OFF_A_KV = 512
OFF_A_Z = 1280
OFF_B_QK = 1792
OFF_B_V = 3840
OFF_B_O = 4864
OFF_B_Z = 5888
OFF_C_Q = 6912
OFF_C_KV = 7424
OFF_C_Z = 7680
OFF_A_GATE = 8192
OFF_B_IF = 8320
PROJ_COLS = 8448
W_RUNS = ((0, OFF_A_Q, SRC_GATE), (SRC_A_Z, OFF_A_Z, SRC_B_IF - SRC_A_Z), (SRC_B_O, OFF_B_O, IN_COLS - SRC_B_O))
assert OFF_A_Z + (SRC_B_IF - SRC_A_Z) == OFF_B_O and OFF_B_O + (IN_COLS - SRC_B_O) == OFF_A_GATE

V7X_VMEM_BYTES = 64 * 1024 * 1024
VMEM_LIMIT = V7X_VMEM_BYTES * 3 // 4


def _cparams(sem):
    return pltpu.CompilerParams(dimension_semantics=sem, vmem_limit_bytes=VMEM_LIMIT)


def _dot(a, b):
    return jnp.dot(a, b, preferred_element_type=F32)


def _dot_nt(a, b):
    return lax.dot_general(a, b, (((1,), (1,)), ((), ())), preferred_element_type=F32)


def _dot_tn(a, b):
    return lax.dot_general(a, b, (((0,), (0,)), ((), ())), preferred_element_type=F32)


def _sigmoid(x):
    return 1.0 / (1.0 + jnp.exp(-x))


def _silu(x):
    return x * _sigmoid(x)


def _heads_to_rows(ot, qc):
    return jnp.concatenate([ot[:, hh * qc:(hh + 1) * qc] for hh in range(HEADS_PER_GROUP)], axis=0).T


def _repack_kernel(wt_ref, o_ref):
    for src, dst, width in W_RUNS:
        o_ref[dst:dst + width, :] = wt_ref[0, src:src + width, :].astype(BF16)
    o_ref[OFF_A_GATE:PROJ_COLS, :] = jnp.zeros((PROJ_COLS - OFF_A_GATE, o_ref.shape[1]), BF16)
    o_ref[OFF_A_GATE:OFF_A_GATE + N_GATE, :] = wt_ref[0, SRC_GATE:SRC_GATE + N_GATE, :].astype(BF16)
    o_ref[OFF_B_IF:OFF_B_IF + N_IF, :] = wt_ref[0, SRC_B_IF:SRC_B_IF + N_IF, :].astype(BF16)


def repack_w_in(w_in_t, layer, *, tc):
    d = w_in_t.shape[2]
    return pl.pallas_call(
        _repack_kernel,
        grid=(d // tc,),
        in_specs=[pl.BlockSpec((1, IN_COLS, tc), lambda i: (layer, 0, i))],
        out_specs=pl.BlockSpec((PROJ_COLS, tc), lambda i: (0, i)),
        out_shape=jax.ShapeDtypeStruct((PROJ_COLS, d), BF16),
        compiler_params=_cparams(("parallel",)),
        name="repack_w_in",
    )(w_in_t)


def _norm_matmul_kernel(x_ref, g_ref, wt_ref, o_ref, h_ref):
    @pl.when(pl.program_id(1) == 0)
    def _():
        x = x_ref[...]
        ms = jnp.mean(x * x, axis=-1, keepdims=True)
        h_ref[...] = (x * lax.rsqrt(ms + EPS) * g_ref[...]).astype(BF16)

    o_ref[...] = _dot_nt(h_ref[...], wt_ref[...])


def norm_matmul(x, g, wt, *, tm, tn):
    n, d = x.shape
    cols = wt.shape[0]
    return pl.pallas_call(
        _norm_matmul_kernel,
        grid=(n // tm, cols // tn),
        in_specs=[
            pl.BlockSpec((tm, d), lambda i, j: (i, 0)),
            pl.BlockSpec((1, d), lambda i, j: (0, 0)),
            pl.BlockSpec((tn, d), lambda i, j: (j, 0)),
        ],
        out_specs=pl.BlockSpec((tm, tn), lambda i, j: (i, j)),
        out_shape=jax.ShapeDtypeStruct((n, cols), F32),
        scratch_shapes=[pltpu.VMEM((tm, d), BF16)],
        compiler_params=_cparams(("parallel", "arbitrary")),
        name="norm_in_proj",
    )(x, g, wt)


def _out_proj_kernel(ya_ref, yb_ref, yc_ref, x_ref, wa_ref, wb_ref, wc_ref, g_ref, o_ref, *, final_norm):
    acc = _dot(ya_ref[...], wa_ref[...]) + _dot(yb_ref[...], wb_ref[...]) + _dot(yc_ref[...], wc_ref[...])
    y = x_ref[...] + acc
    if final_norm:
        ms = jnp.mean(y * y, axis=-1, keepdims=True)
        y = y * lax.rsqrt(ms + EPS) * g_ref[...]
    o_ref[...] = y


def out_proj(ya, yb, yc, x, wa, wb, wc, g, *, tm, final_norm):
    n, d = x.shape
    row = lambda w: pl.BlockSpec((tm, w), lambda i: (i, 0))
    full = lambda a: pl.BlockSpec(a.shape, lambda i: (0, 0))
    return pl.pallas_call(
        functools.partial(_out_proj_kernel, final_norm=final_norm),
        grid=(n // tm,),
        in_specs=[row(ya.shape[1]), row(yb.shape[1]), row(yc.shape[1]), row(d),
                  full(wa), full(wb), full(wc), full(g)],
        out_specs=row(d),
        out_shape=jax.ShapeDtypeStruct((n, d), F32),
        compiler_params=_cparams(("parallel",)),
        name="out_proj",
    )(ya, yb, yc, x, wa, wb, wc, g)


def _compress_kernel(k_ref, v_ref, pe_ref, w1_ref, w2_ref, kc_ref, vct_ref, shift_ref):
    nseg = k_ref.shape[1] // CMP_STRIDE
    for s, src_ref in enumerate((k_ref, v_ref)):
        u1 = jnp.zeros((nseg, LANES), F32)
        u2 = jnp.zeros((nseg, LANES), F32)
        for l in range(CMP_STRIDE):
            x = src_ref[0, pl.ds(l, nseg, stride=CMP_STRIDE), :]
            u1 = u1 + _dot((x + pe_ref[s, l]).astype(BF16), w1_ref[s, l])
            u2 = u2 + _dot((x + pe_ref[s, CMP_STRIDE + l]).astype(BF16), w1_ref[s, CMP_STRIDE + l])
        shift_ref[0:nseg, :] = u2
        shift_ref[nseg:nseg + 8, :] = jnp.zeros((8, LANES), F32)
        pre = u1 + shift_ref[1:nseg + 1, :]
        out = _dot(_silu(pre).astype(BF16), w2_ref[s])
        if s == 0:
            kc_ref[0] = out.astype(BF16)
        else:
            vct_ref[0] = out.T.astype(BF16)


def nsa_compress(proj3, pe, w1, w2):
    bsz, T, _ = proj3.shape
    nseg = T // CMP_STRIDE
    full = lambda a: pl.BlockSpec(a.shape, lambda b: (0,) * a.ndim)
    return pl.pallas_call(
        _compress_kernel,
        grid=(bsz,),
        in_specs=[pl.BlockSpec((1, T, LANES), lambda b: (b, 0, OFF_A_KV // LANES)),
                  pl.BlockSpec((1, T, LANES), lambda b: (b, 0, OFF_A_KV // LANES + 1)),
                  full(pe), full(w1), full(w2)],
        out_specs=[pl.BlockSpec((1, nseg, LANES), lambda b: (b, 0, 0)),
                   pl.BlockSpec((1, LANES, nseg), lambda b: (b, 0, 0))],
        out_shape=[jax.ShapeDtypeStruct((bsz, nseg, LANES), BF16),
                   jax.ShapeDtypeStruct((bsz, LANES, nseg), BF16)],
        scratch_shapes=[pltpu.VMEM((nseg + 8, LANES), F32)],
        compiler_params=_cparams(("parallel",)),
        name="nsa_compress",
    )(proj3, proj3, pe, w1, w2)


def _cmp_select_kernel(q_ref, kc_ref, vct_ref, ovt_ref, ksvs_ref, o_ref, selbt_ref, kaug_ref, vaugt_ref,
                       w_ref, s_ref, p_ref, *, qc, n_top, tk_slc, n_buckets):
    ncmp = kc_ref.shape[1]
    c = pl.program_id(1)
    t0 = c * qc
    t_row = t0 + lax.broadcasted_iota(jnp.int32, (1, qc), 1)
    any_vis = (t_row >= CMP_BLOCK - 1).astype(F32)
    cur = jnp.right_shift(t_row, 6)
    qt = (q_ref[0] * Q_SCALE).T.astype(BF16)
    zero = jnp.zeros((HEAD_DIM, qc), BF16)
    for h in range(A_HEADS):
        g, hh = divmod(h, HEADS_PER_GROUP)
        cols = slice(hh * qc, (hh + 1) * qc)
        w_ref[g, g * HEAD_DIM:(g + 1) * HEAD_DIM, cols] = qt[h * HEAD_DIM:(h + 1) * HEAD_DIM, :]
        w_ref[g, (1 - g) * HEAD_DIM:(2 - g) * HEAD_DIM, cols] = zero

    def attend_and_select(rc, rb):
        cmp_i = lax.broadcasted_iota(jnp.int32, (rc, 1), 0)
        vis = (cmp_i * CMP_STRIDE + (CMP_BLOCK - 1)) <= t_row
        blk = lax.broadcasted_iota(jnp.int32, (rb, qc), 0)
        valid = blk <= cur
        forced = (blk == 0) | (valid & (blk > cur - SEL_LOCAL))
        for g in range(A_KV_HEADS):
            s_ref[g, 0:rc, :] = _dot(kc_ref[0, 0:rc, :], w_ref[g])
        ovt = ovt_ref[0:rb, 0:rc]
        for g in range(A_KV_HEADS):
            psum = jnp.zeros((rc, qc), F32)
            for hh in range(HEADS_PER_GROUP):
                cols = slice(hh * qc, (hh + 1) * qc)
                s = jnp.where(vis, s_ref[g, 0:rc, cols], NEG_INF)
                e = jnp.exp2(s - jnp.max(s, axis=0, keepdims=True))
                p = e * (any_vis / jnp.sum(e, axis=0, keepdims=True))
                p_ref[g, 0:rc, cols] = p.astype(BF16)
                psum = psum + p
            ot = _dot(vct_ref[0, :, 0:rc], p_ref[g, 0:rc, :])
            o_ref[0, :, g * GROUP_WIDTH:(g + 1) * GROUP_WIDTH] = _heads_to_rows(
                ot[g * HEAD_DIM:(g + 1) * HEAD_DIM, :], qc)
            p1 = psum.astype(BF16)
            r1 = psum - p1.astype(F32)
            p2 = r1.astype(BF16)
            p3 = (r1 - p2.astype(F32)).astype(BF16)
            imp = _dot(ovt, p1) + _dot(ovt, p2) + _dot(ovt, p3)
            val = jnp.where(forced, -2.0, jnp.where(valid, imp, -1.0))
            for _ in range(n_top - (SEL_LOCAL + 1)):
                mx = jnp.max(val, axis=0, keepdims=True)
                first = jnp.min(jnp.where(val == mx, blk, MAX_SEL_BLOCKS), axis=0, keepdims=True)
                val = jnp.where(blk == first, -2.0, val)
            chosen = (val == -2.0) & (blk < cur)
            selbt_ref[0, g, 0:rb, :] = jnp.where(chosen, 0.0, NEG_INF).astype(BF16)
            if rb < MAX_SEL_BLOCKS:
                selbt_ref[0, g, rb:MAX_SEL_BLOCKS, :] = jnp.full((MAX_SEL_BLOCKS - rb, qc), NEG_INF, BF16)

    chunks_per_bucket = (ncmp // n_buckets) * CMP_STRIDE // qc
    for b in range(n_buckets):
        @pl.when((c >= b * chunks_per_bucket) & (c < (b + 1) * chunks_per_bucket))
        def _(b=b):
            attend_and_select((b + 1) * (ncmp // n_buckets), (b + 1) * (MAX_SEL_BLOCKS // n_buckets))

    kv = ksvs_ref[0]
    pos = t0 + lax.broadcasted_iota(jnp.int32, (qc, 1), 0)
    lane = lax.broadcasted_iota(jnp.int32, (1, LANES), 1)
    onehot = jnp.where(jnp.right_shift(pos, 6) == lane, 1.0, 0.0).astype(BF16)
    flane = lax.broadcasted_iota(jnp.int32, (1, AUG_DEPTH - LANES - HEAD_DIM), 1)
    in_block = jnp.bitwise_and(pos, SEL_BLOCK - 1).astype(F32)
    in_tile = jnp.bitwise_and(jnp.right_shift(pos, 6), tk_slc // SEL_BLOCK - 1).astype(F32)
    feat = jnp.where(flane < N_SPLIT, in_block, jnp.where(flane < 2 * N_SPLIT, in_tile, 0.0)).astype(BF16)
    vt = kv[:, LANES:2 * LANES].T.astype(BF16)
    ones_row = (lax.broadcasted_iota(jnp.int32, (VAUG_ROWS - HEAD_DIM, qc), 0) == 0).astype(BF16)
    for g in range(A_KV_HEADS):
        kaug_ref[0, g, :, 0:LANES] = onehot
        kaug_ref[0, g, :, LANES:LANES + HEAD_DIM] = kv[:, g * HEAD_DIM:(g + 1) * HEAD_DIM].astype(BF16)
        kaug_ref[0, g, :, LANES + HEAD_DIM:AUG_DEPTH] = feat
        vaugt_ref[0, g, 0:HEAD_DIM, :] = vt[g * HEAD_DIM:(g + 1) * HEAD_DIM, :]
        vaugt_ref[0, g, HEAD_DIM:VAUG_ROWS, :] = ones_row


def cmp_select(proj3, kcmp, vcmp_t, overlap_t, *, qc, n_top, tk_slc):
    bsz, T, _ = proj3.shape
    ncmp = kcmp.shape[1]
    G = A_KV_HEADS
    assert n_top > SEL_LOCAL + 1 and (tk_slc // SEL_BLOCK) & (tk_slc // SEL_BLOCK - 1) == 0
    n_buckets = max(1, min(MAX_SEL_BLOCKS // (2 * n_top), ncmp // LANES))
    assert ncmp % (n_buckets * LANES) == 0 and ncmp == T // CMP_STRIDE and MAX_SEL_BLOCKS // n_buckets > n_top
    return pl.pallas_call(
        functools.partial(_cmp_select_kernel, qc=qc, n_top=n_top, tk_slc=tk_slc, n_buckets=n_buckets),
        grid=(bsz, T // qc),
        in_specs=[
            pl.BlockSpec((1, qc, A_WIDTH), lambda b, c: (b, c, OFF_A_Q // A_WIDTH)),
            pl.BlockSpec((1, ncmp, LANES), lambda b, c: (b, 0, 0)),
            pl.BlockSpec((1, LANES, ncmp), lambda b, c: (b, 0, 0)),
            pl.BlockSpec((MAX_SEL_BLOCKS, ncmp), lambda b, c: (0, 0)),
            pl.BlockSpec((1, qc, 2 * LANES), lambda b, c: (b, c, (OFF_A_KV + 2 * LANES) // (2 * LANES))),
        ],
        out_specs=[
            pl.BlockSpec((1, qc, A_WIDTH), lambda b, c: (b, c, 0)),
            pl.BlockSpec((1, G, MAX_SEL_BLOCKS, qc), lambda b, c: (b, 0, 0, c)),
            pl.BlockSpec((1, G, qc, AUG_DEPTH), lambda b, c: (b, 0, c, 0)),
            pl.BlockSpec((1, G, VAUG_ROWS, qc), lambda b, c: (b, 0, 0, c)),
        ],
        out_shape=[jax.ShapeDtypeStruct((bsz, T, A_WIDTH), F32),
                   jax.ShapeDtypeStruct((bsz, G, MAX_SEL_BLOCKS, T), BF16),
                   jax.ShapeDtypeStruct((bsz, G, T, AUG_DEPTH), BF16),
                   jax.ShapeDtypeStruct((bsz, G, VAUG_ROWS, T), BF16)],
        scratch_shapes=[
            pltpu.VMEM((G, LANES, HEADS_PER_GROUP * qc), BF16),
            pltpu.VMEM((G, ncmp, HEADS_PER_GROUP * qc), F32),
            pltpu.VMEM((G, ncmp, HEADS_PER_GROUP * qc), BF16),
        ],
        compiler_params=_cparams(("parallel", "parallel")),
        name="nsa_cmp_select",
    )(proj3, kcmp, vcmp_t, overlap_t, proj3)


def _slc_kernel(q_ref, selbt_ref, kaug_ref, vaugt_ref, arow_ref, slope_ref, dbias_ref, o_ref,
                qaug_ref, m_ref, acc_ref, *s_refs, qc, tk):
    t0 = pl.multiple_of(pl.program_id(2) * qc, qc)
    selbt = selbt_ref[0, 0]
    qt = (q_ref[0] * Q_SCALE).T.astype(BF16)
    for hh in range(HEADS_PER_GROUP):
        cols = slice(hh * qc, (hh + 1) * qc)
        qaug_ref[0:LANES, cols] = selbt
        qaug_ref[LANES:LANES + HEAD_DIM, cols] = qt[hh * HEAD_DIM:(hh + 1) * HEAD_DIM, :]
    qaug_ref[LANES + HEAD_DIM:AUG_DEPTH, :] = arow_ref[0]
    m_ref[...] = jnp.full(m_ref.shape, NEG_INF, F32)
    acc_ref[...] = jnp.zeros(acc_ref.shape, F32)

    bufs = (s_refs[:HEADS_PER_GROUP], s_refs[HEADS_PER_GROUP:])

    def scores(kt, hh, dst):
        s0 = pl.multiple_of(kt * tk, tk)
        dst[hh][...] = _dot(kaug_ref[0, 0, pl.ds(s0, tk), :], qaug_ref[:, hh * qc:(hh + 1) * qc])

    def diag_scores(hh, dst):
        ka = kaug_ref[0, 0, pl.ds(t0, qc), LANES:AUG_DEPTH]
        st = _dot(ka, qaug_ref[LANES:AUG_DEPTH, hh * qc:(hh + 1) * qc])
        dst[hh][0:qc, :] = st + dbias_ref[...]

    def softmax_pv(hh, src, rows, vt, off):
        c = slope_ref[0, hh] * off
        m_old = m_ref[hh]
        m_new = jnp.maximum(m_old, jnp.max(src[hh][0:rows, :], axis=0, keepdims=True) + c)
        alpha = jnp.exp2(m_old - m_new)
        p = jnp.exp2(src[hh][0:rows, :] - (m_new - c)).astype(BF16)
        acc_ref[hh] = alpha * acc_ref[hh] + _dot(vt, p)
        m_ref[hh] = m_new

    def step(kt, src, next_scores):
        s0 = pl.multiple_of(kt * tk, tk)
        vt = vaugt_ref[0, 0, :, pl.ds(s0, tk)]
        off = (s0 - t0).astype(F32)
        for hh in range(HEADS_PER_GROUP):
            next_scores(hh)
            softmax_pv(hh, src, tk, vt, off)

    def main_step(kt, src, dst):
        step(kt, src, lambda hh: scores(kt + 1, hh, dst))

    n_full = lax.div(t0, tk)
    odd = lax.rem(n_full, 2)

    @pl.when(odd == 0)
    def _():
        for hh in range(HEADS_PER_GROUP):
            scores(0, hh, bufs[0])

    @pl.when(odd == 1)
    def _():
        for hh in range(HEADS_PER_GROUP):
            scores(0, hh, bufs[1])
        main_step(0, bufs[1], bufs[0])

    def body(i, carry):
        kt = odd + 2 * i
        main_step(kt, bufs[0], bufs[1])
        main_step(kt + 1, bufs[1], bufs[0])
        return carry

    lax.fori_loop(0, lax.div(n_full, 2), body, 0)
    step(n_full, bufs[0], lambda hh: diag_scores(hh, bufs[1]))
    base = pl.multiple_of(n_full * tk, tk)
    vt_d = vaugt_ref[0, 0, :, pl.ds(t0, qc)]
    for hh in range(HEADS_PER_GROUP):
        softmax_pv(hh, bufs[1], qc, vt_d, (base - t0).astype(F32))
    ot = jnp.concatenate([acc_ref[hh, 0:HEAD_DIM, :] / acc_ref[hh, HEAD_DIM:HEAD_DIM + 1, :]
                          for hh in range(HEADS_PER_GROUP)], axis=0)
    o_ref[0] = ot.T


def slc_attention(proj3, selbt, kaug, vaugt, arows, slopes, *, qc, tk):
    bsz, T, _ = proj3.shape
    G = A_KV_HEADS
    assert tk % qc == 0 and qc % SEL_BLOCK == 0
    ki = np.arange(qc)[:, None]
    qi = np.arange(qc)[None, :]
    dbias = jnp.asarray(np.where((ki // SEL_BLOCK == qi // SEL_BLOCK) & (ki <= qi), 0.0, NEG_INF), dtype=F32)
    return pl.pallas_call(
        functools.partial(_slc_kernel, qc=qc, tk=tk),
        grid=(bsz, G, T // qc),
        in_specs=[
            pl.BlockSpec((1, qc, GROUP_WIDTH), lambda b, g, c: (b, c, OFF_A_Q // GROUP_WIDTH + g)),
            pl.BlockSpec((1, 1, LANES, qc), lambda b, g, c: (b, g, 0, c)),
            pl.BlockSpec((1, 1, T, AUG_DEPTH), lambda b, g, c: (b, g, 0, 0)),
            pl.BlockSpec((1, 1, VAUG_ROWS, T), lambda b, g, c: (b, g, 0, 0)),
            pl.BlockSpec((1, AUG_DEPTH - LANES - HEAD_DIM, HEADS_PER_GROUP * qc), lambda b, g, c: (g, 0, 0)),
            pl.BlockSpec((1, HEADS_PER_GROUP, 1, qc), lambda b, g, c: (g, 0, 0, 0)),
            pl.BlockSpec((qc, qc), lambda b, g, c: (0, 0)),
        ],
        out_specs=pl.BlockSpec((1, qc, GROUP_WIDTH), lambda b, g, c: (b, c, g)),
        out_shape=jax.ShapeDtypeStruct((bsz, T, A_WIDTH), F32),
        scratch_shapes=[
            pltpu.VMEM((AUG_DEPTH, HEADS_PER_GROUP * qc), BF16),
            pltpu.VMEM((HEADS_PER_GROUP, 1, qc), F32),
            pltpu.VMEM((HEADS_PER_GROUP, VAUG_ROWS, qc), F32),
        ] + [pltpu.VMEM((tk, qc), F32)] * (2 * HEADS_PER_GROUP),
        compiler_params=_cparams(("parallel", "parallel", "arbitrary")),
        name="nsa_slc_attention",
    )(proj3, selbt, kaug, vaugt, arows, slopes, dbias)


def _band_kernel(*refs, qc, window, has_sinks, mix):
    q_refs, kv_ref, bias_ref = refs[0:2], refs[2], refs[3]
    pos = 4
    sink_ref = None
    if has_sinks:
        sink_ref = refs[pos]
        pos += 1
    z_refs = refs[pos:pos + 2]
    pos += 2
    if mix:
        ocmp_ref, oslc_ref, gate_ref = refs[pos:pos + 3]
        pos += 3
    o_ref, w_ref, kwin_ref, vt_ref, p_ref = refs[pos:pos + 5]
    s_refs = refs[pos + 5:pos + 7]
    G = len(q_refs)
    span = qc + window
    lanes = HEADS_PER_GROUP * qc
    n_pad_chunks = -(-window // qc)
    c = pl.program_id(1)
    ones_row = (lax.broadcasted_iota(jnp.int32, (VAUG_ROWS - HEAD_DIM, span), 0) == 0).astype(BF16)

    def assemble(kv):
        kwin_ref[...] = kv[:, 0:LANES].astype(BF16)
        vt = kv[:, LANES:2 * LANES].T.astype(BF16)
        for g in range(G):
            vt_ref[g, 0:HEAD_DIM, :] = vt[g * HEAD_DIM:(g + 1) * HEAD_DIM, :]
            vt_ref[g, HEAD_DIM:VAUG_ROWS, :] = ones_row

    @pl.when(c >= n_pad_chunks)
    def _():
        start = pl.multiple_of(c * qc - window, math.gcd(qc, window))
        assemble(kv_ref[0, pl.ds(start, span), :])

    for j in range(n_pad_chunks):
        @pl.when(c == j)
        def _(j=j):
            n_pad = window - j * qc
            assemble(jnp.concatenate([jnp.zeros((n_pad, 2 * LANES), F32), kv_ref[0, 0:span - n_pad, :]], axis=0))

    zero = jnp.zeros((HEAD_DIM, qc), BF16)
    for g in range(G):
        qt = (q_refs[g][0] * Q_SCALE).T.astype(BF16)
        for hh in range(HEADS_PER_GROUP):
            cols = slice(hh * qc, (hh + 1) * qc)
            w_ref[g, g * HEAD_DIM:(g + 1) * HEAD_DIM, cols] = qt[hh * HEAD_DIM:(hh + 1) * HEAD_DIM, :]
            w_ref[g, (1 - g) * HEAD_DIM:(2 - g) * HEAD_DIM, cols] = zero
    pieces = [slice(r, r + LANES) for r in range(0, span, LANES)]
    for g in range(G):
        for rows in pieces:
            s_refs[g][rows, :] = _dot(kwin_ref[rows, :], w_ref[g])
    if mix:
        gate = _sigmoid(gate_ref[0])
    for g in range(G):
        mx8 = None
        for rows in pieces:
            t = s_refs[g][rows, :] + bias_ref[0, rows, g * lanes:(g + 1) * lanes]
            s_refs[g][rows, :] = t
            m8 = jnp.max(t.reshape(LANES // 8, 8, lanes), axis=0)
            mx8 = m8 if mx8 is None else jnp.maximum(mx8, m8)
        mx = jnp.max(mx8, axis=0, keepdims=True)
        if has_sinks:
            sk = sink_ref[:, g * lanes:(g + 1) * lanes]
            mx = jnp.maximum(mx, sk)
        for rows in pieces:
            p_ref[rows, :] = jnp.exp2(s_refs[g][rows, :] - mx).astype(BF16)
        ot = _dot(vt_ref[g], p_ref[...])
        den = ot[HEAD_DIM:HEAD_DIM + 1, :]
        if has_sinks:
            den = den + jnp.exp2(sk - mx)
        o = _heads_to_rows(ot[0:HEAD_DIM, :] / den, qc)
        z = z_refs[g][0]
        for hh in range(HEADS_PER_GROUP):
            h = g * HEADS_PER_GROUP + hh
            loc = slice(hh * HEAD_DIM, (hh + 1) * HEAD_DIM)
            glob = slice(h * HEAD_DIM, (h + 1) * HEAD_DIM)
            oh = o[:, loc]
            if mix:
                oh = (gate[:, 3 * h:3 * h + 1] * ocmp_ref[0, :, glob]
                      + gate[:, 3 * h + 1:3 * h + 2] * oslc_ref[0, :, glob]
                      + gate[:, 3 * h + 2:3 * h + 3] * oh)
            o_ref[0, :, glob] = (oh * _silu(z[:, loc])).astype(o_ref.dtype)


def _band_bias(qc, window, n_heads):
    span = qc + window
    n_var = -(-window // qc) + 1
    slopes = jnp.asarray(_alibi_np(n_heads)) * LOG2E
    row = lax.broadcasted_iota(jnp.int32, (span, qc), 0)
    dist = lax.broadcasted_iota(jnp.int32, (span, qc), 1) + window - row
    band = (dist >= 0) & (dist < window)
    term = -(slopes[:, None, None] * dist.astype(F32)[None])
    n_pad = window - jnp.arange(n_var, dtype=jnp.int32)[:, None, None, None] * qc
    ok = band[None, None] & (row[None, None] >= n_pad)
    bias = jnp.where(ok, term[None], NEG_INF)
    return bias.transpose(0, 2, 1, 3).reshape(n_var, span, n_heads * qc)


def band_attention(proj3, sinks, mix_in, *, q_off, kv_off, z_off, qc, window):
    bsz, T, _ = proj3.shape
    G = 2
    n_heads = G * HEADS_PER_GROUP
    span = qc + window
    assert span % LANES == 0 and math.gcd(qc, window) % 8 == 0 and T >= span
    has_sinks = sinks is not None
    mix = mix_in is not None
    bias = _band_bias(qc, window, n_heads)
    n_var = bias.shape[0]
    grp = lambda off: [pl.BlockSpec((1, qc, GROUP_WIDTH), lambda b, c, g=g: (b, c, off // GROUP_WIDTH + g))
                       for g in range(G)]
    in_specs = grp(q_off) + [
        pl.BlockSpec((1, T, 2 * LANES), lambda b, c: (b, 0, kv_off // (2 * LANES))),
        pl.BlockSpec((1, span, n_heads * qc), lambda b, c: (jnp.minimum(c, n_var - 1), 0, 0)),
    ]
    args = [proj3, proj3, proj3, bias]
    if has_sinks:
        in_specs.append(pl.BlockSpec((1, n_heads * qc), lambda b, c: (0, 0)))
        args.append(jnp.repeat(sinks.astype(F32) * LOG2E, qc).reshape(1, n_heads * qc))
    in_specs += grp(z_off)
    args += [proj3, proj3]
    if mix:
        full = pl.BlockSpec((1, qc, A_WIDTH), lambda b, c: (b, c, 0))
        in_specs += [full, full, pl.BlockSpec((1, qc, LANES), lambda b, c: (b, c, OFF_A_GATE // LANES))]
        args += [mix_in[0], mix_in[1], proj3]
    return pl.pallas_call(
        functools.partial(_band_kernel, qc=qc, window=window, has_sinks=has_sinks, mix=mix),
        grid=(bsz, T // qc),
        in_specs=in_specs,
        out_specs=pl.BlockSpec((1, qc, G * GROUP_WIDTH), lambda b, c: (b, c, 0)),
        out_shape=jax.ShapeDtypeStruct((bsz, T, G * GROUP_WIDTH), BF16),
        scratch_shapes=[
            pltpu.VMEM((G, LANES, HEADS_PER_GROUP * qc), BF16),
            pltpu.VMEM((span, LANES), BF16),
            pltpu.VMEM((G, VAUG_ROWS, span), BF16),
            pltpu.VMEM((span, HEADS_PER_GROUP * qc), BF16),
        ] + [pltpu.VMEM((span, HEADS_PER_GROUP * qc), F32)] * G,
        compiler_params=_cparams(("parallel", "parallel")),
        name="band_attention_w%d" % window,
    )(*args)


def _log_sigmoid(x):
    return jnp.minimum(x, 0.0) - jnp.log(1.0 + jnp.exp(-jnp.abs(x)))


def _mlstm_kernel(*refs, L, hps):
    per_head_in = [refs[5 * j:5 * j + 5] for j in range(hps)]
    if_ref, ifb_ref = refs[5 * hps:5 * hps + 2]
    base = 5 * hps + 2
    per_head_par = [refs[base + 5 * j:base + 5 * j + 5] for j in range(hps)]
    y_ref, xq_ref, xk_ref, c_ref, n_ref, m_ref, ift_ref = refs[base + 5 * hps:]

    @pl.when(pl.program_id(2) == 0)
    def _():
        xq_ref[:, 0:8, :] = jnp.zeros((hps, 8, B_HEAD_DIM), F32)
        xk_ref[:, 0:8, :] = jnp.zeros((hps, 8, B_HEAD_DIM), F32)
        c_ref[...] = jnp.zeros(c_ref.shape, F32)
        n_ref[...] = jnp.zeros(n_ref.shape, F32)
        m_ref[...] = jnp.zeros(m_ref.shape, F32)

    def conv_silu(x_ref, hist_ref, j, w_ref, b_ref):
        hist_ref[j, 8:8 + L, :] = x_ref[0]
        xx = hist_ref[j]
        y = b_ref[...] + w_ref[CONV_WIDTH - 1:CONV_WIDTH, :] * xx[8:8 + L, :]
        for back in range(1, CONV_WIDTH):
            tap = CONV_WIDTH - 1 - back
            y = y + w_ref[tap:tap + 1, :] * pltpu.roll(xx, back, axis=0)[8:8 + L, :]
        hist_ref[j, 0:8, :] = xx[L:L + 8, :]
        return _silu(y)

    gates = if_ref[0] + ifb_ref[...]
    ift_ref[...] = gates.T
    lane = lax.broadcasted_iota(jnp.int32, (1, LANES), 1)
    r_i = lax.broadcasted_iota(jnp.int32, (L, L), 0)
    c_i = lax.broadcasted_iota(jnp.int32, (L, L), 1)
    causal = c_i <= r_i

    for j in range(hps):
        q_ref, k_ref, v_ref, og_ref, z_ref = per_head_in[j]
        cwq_ref, cwk_ref, cbq_ref, cbk_ref, g_ref = per_head_par[j]
        head = pl.program_id(1) * hps + j
        q = conv_silu(q_ref, xq_ref, j, cwq_ref, cbq_ref)
        k = conv_silu(k_ref, xk_ref, j, cwk_ref, cbk_ref) * (B_HEAD_DIM ** -0.5)
        v = v_ref[0]
        i_col = jnp.sum(jnp.where(lane == head, gates, 0.0), axis=1, keepdims=True)
        f_col = jnp.sum(jnp.where(lane == B_HEADS + head, gates, 0.0), axis=1, keepdims=True)
        i_row = ift_ref[pl.ds(head, 1), :]
        f_row = ift_ref[pl.ds(B_HEADS + head, 1), :]
        lf_col = _log_sigmoid(f_col)
        lf_row = _log_sigmoid(f_row)
        b_col = jnp.sum(jnp.where(causal, lf_row, 0.0), axis=1, keepdims=True)
        b_row = jnp.sum(jnp.where(r_i <= c_i, lf_col, 0.0), axis=0, keepdims=True)
        b_last = jnp.sum(lf_row, axis=1, keepdims=True)
        m_prev = m_ref[j]
        log_d = jnp.where(causal, b_col - b_row + i_row, NEG_INF)
        log_inter = b_col + m_prev
        m_t = jnp.maximum(log_inter, jnp.max(log_d, axis=1, keepdims=True))
        w_intra = jnp.exp(log_d - m_t)
        w_inter = jnp.exp(log_inter - m_t)
        qb = q.astype(BF16)
        vb = v.astype(BF16)
        qk = _dot_nt(qb, k.astype(BF16)) * w_intra
        num = w_inter * _dot(qb, c_ref[j].astype(BF16)) + _dot(qk.astype(BF16), vb)
        den = (w_inter * jnp.sum(q * n_ref[j], axis=1, keepdims=True)
               + jnp.sum(qk, axis=1, keepdims=True))
        h = num / jnp.maximum(jnp.abs(den), jnp.exp(-m_t))
        log_g_row = b_last - b_row + i_row
        m_new = jnp.maximum(b_last + m_prev, jnp.max(log_g_row, axis=1, keepdims=True))
        w_g = jnp.exp(b_last - b_col + i_col - m_new)
        decay = jnp.exp(b_last + m_prev - m_new)
        kw = k * w_g
        c_ref[j] = decay * c_ref[j] + _dot_tn(kw.astype(BF16), vb)
        n_ref[j] = decay * n_ref[j] + jnp.sum(kw, axis=0, keepdims=True)
        m_ref[j] = m_new
        hb = _sigmoid(og_ref[0]) * h
        ms = jnp.mean(hb * hb, axis=-1, keepdims=True)
        hb = hb * lax.rsqrt(ms + EPS) * g_ref[...]
        y_ref[0, :, j * B_HEAD_DIM:(j + 1) * B_HEAD_DIM] = (hb * _silu(z_ref[0])).astype(y_ref.dtype)


def mlstm(proj3, if_bias, conv_w, conv_b, norm_g, *, L, hps):
    bsz, T, _ = proj3.shape
    H = B_HEADS
    D = B_HEAD_DIM
    assert H % hps == 0

    def col(off, j, k_half=False):
        base = off // D + (H if k_half else 0) + j
        return pl.BlockSpec((1, L, D), lambda b, h, c: (b, c, base + h * hps))

    def par(rows, j, k_half=False):
        base = (H if k_half else 0) + j
        return pl.BlockSpec((rows, D), lambda b, h, c: (0, base + h * hps))

    in_specs, args = [], []
    for j in range(hps):
        in_specs += [col(OFF_B_QK, j), col(OFF_B_QK, j, True), col(OFF_B_V, j), col(OFF_B_O, j), col(OFF_B_Z, j)]
        args += [proj3] * 5
    in_specs += [pl.BlockSpec((1, L, LANES), lambda b, h, c: (b, c, OFF_B_IF // LANES)),
                 pl.BlockSpec((1, LANES), lambda b, h, c: (0, 0))]
    args += [proj3, if_bias]
    for j in range(hps):
        in_specs += [par(CONV_WIDTH, j), par(CONV_WIDTH, j, True), par(1, j), par(1, j, True),
                     pl.BlockSpec((1, D), lambda b, h, c, j=j: (0, h * hps + j))]
        args += [conv_w, conv_w, conv_b, conv_b, norm_g]
    return pl.pallas_call(
        functools.partial(_mlstm_kernel, L=L, hps=hps),
        grid=(bsz, H // hps, T // L),
        in_specs=in_specs,
        out_specs=pl.BlockSpec((1, L, hps * D), lambda b, h, c: (b, c, h)),
        out_shape=jax.ShapeDtypeStruct((bsz, T, B_WIDTH), BF16),
        scratch_shapes=[
            pltpu.VMEM((hps, L + 8, D), F32), pltpu.VMEM((hps, L + 8, D), F32),
            pltpu.VMEM((hps, D, D), F32), pltpu.VMEM((hps, 1, D), F32), pltpu.VMEM((hps, 1, 1), F32),
            pltpu.VMEM((LANES, L), F32),
        ],
        compiler_params=_cparams(("parallel", "parallel", "arbitrary")),
        name="mlstm_chunkwise",
    )(*args)


def _alibi_np(n_heads):
    return (2.0 ** (-8.0 * np.arange(1, n_heads + 1, dtype=np.float64) / n_heads)).astype(np.float32)


def _bf16_pieces(c):
    pieces, rest = [], np.asarray(c, np.float64)
    for _ in range(N_SPLIT):
        piece = rest.astype(np.float32).astype(jnp.bfloat16).astype(np.float64)
        pieces.append(piece)
        rest = rest - piece
    return np.stack(pieces, axis=1).astype(np.float32)


def _lane_rep(v, groups, width=LANES):
    return jnp.broadcast_to(v.astype(F32).reshape(groups, -1, 1, 1), (groups, v.shape[0] // groups, 1, width))


def _block_diag2(w):
    z = jnp.zeros_like(w)
    return jnp.concatenate([jnp.concatenate([w, z], axis=-1), jnp.concatenate([z, w], axis=-1)], axis=-2)


def _mixers(x2d, bsz, T, p, layer, cfg):
    n, d = x2d.shape
    G = A_KV_HEADS
    w_packed = repack_w_in(jnp.swapaxes(p["w_in"], 1, 2), layer, tc=cfg["tc_repack"])
    proj = norm_matmul(x2d, p["norm_g"].reshape(1, d), w_packed, tm=cfg["tm_in"], tn=cfg["tn_in"])
    proj3 = proj.reshape(bsz, T, PROJ_COLS)

    nseg = T // CMP_STRIDE
    pe = jnp.stack([p["cmp_pe_k"], p["cmp_pe_v"]])
    pe = jnp.concatenate([pe, pe], axis=-1).reshape(2, CMP_BLOCK, 1, LANES)
    w1 = jnp.stack([p["cmp_w1_k"], p["cmp_w1_v"]]).reshape(2, CMP_BLOCK, HEAD_DIM, HEAD_DIM)
    w2 = jnp.stack([p["cmp_w2_k"], p["cmp_w2_v"]])
    k_cmp, v_cmp_t = nsa_compress(proj3, pe, _block_diag2(w1).astype(BF16), _block_diag2(w2).astype(BF16))

    n_sel = T // SEL_BLOCK
    assert n_sel <= MAX_SEL_BLOCKS
    qc_s, tk_s = cfg["qc_slc"], cfg["tk_slc"]
    ci = np.arange(nseg)[None, :] * CMP_STRIDE
    sj = np.arange(MAX_SEL_BLOCKS)[:, None] * SEL_BLOCK
    overlap_t = jnp.asarray(((ci < sj + SEL_BLOCK) & (ci + CMP_BLOCK > sj)).astype(np.float32), dtype=BF16)
    o_cmp, selbt, kaug, vaugt = cmp_select(proj3, k_cmp, v_cmp_t, overlap_t, qc=cfg["qc_cmp"],
                                           n_top=min(SEL_TOPK, n_sel), tk_slc=tk_s)

    sl2 = _alibi_np(A_HEADS).astype(np.float64) * LOG2E
    arows = np.zeros((A_HEADS, AUG_DEPTH - LANES - HEAD_DIM, qc_s), np.float32)
    arows[:, 0:N_SPLIT, :] = _bf16_pieces(sl2)[:, :, None]
    arows[:, N_SPLIT:2 * N_SPLIT, :] = _bf16_pieces(SEL_BLOCK * sl2)[:, :, None]
    arows = arows.reshape(G, HEADS_PER_GROUP, -1, qc_s).transpose(0, 2, 1, 3).reshape(G, -1, HEADS_PER_GROUP * qc_s)
    o_slc = slc_attention(proj3, selbt, kaug, vaugt, jnp.asarray(arows, dtype=BF16),
                          _lane_rep(jnp.asarray(sl2.astype(np.float32)), G, qc_s), qc=qc_s, tk=tk_s)

    y_a = band_attention(proj3, None, (o_cmp, o_slc), q_off=OFF_A_Q, kv_off=OFF_A_KV + 4 * LANES, z_off=OFF_A_Z,
                         qc=cfg["qc_win"], window=NSA_WINDOW)

    if_bias = jnp.concatenate([p["mlstm_i_bias"], p["mlstm_f_bias"], jnp.zeros((LANES - N_IF,), F32)]).reshape(1, LANES)
    y_b = mlstm(proj3, if_bias, p["mlstm_conv_w"], p["mlstm_conv_b"].reshape(1, -1),
                p["mlstm_norm_g"].reshape(1, -1), L=cfg["mlstm_chunk"], hps=cfg["mlstm_heads_per_step"])

    y_c = band_attention(proj3, p["swa_sinks"], None, q_off=OFF_C_Q, kv_off=OFF_C_KV, z_off=OFF_C_Z,
                         qc=cfg["qc_swa"], window=SWA_WINDOW)
    return y_a.reshape(n, A_WIDTH), y_b.reshape(n, B_WIDTH), y_c.reshape(n, C_WIDTH), (o_cmp, o_slc, selbt)


def _layer(x2d, bsz, T, p, layer, final_g, cfg):
    n, d = x2d.shape
    y_a, y_b, y_c, _ = _mixers(x2d, bsz, T, p, layer, cfg)
    w_out = p["w_out"].astype(BF16)
    g = final_g if final_g is not None else p["norm_g"]
    return out_proj(y_a, y_b, y_c, x2d,
                    w_out[:A_WIDTH], w_out[A_WIDTH:A_WIDTH + B_WIDTH], w_out[A_WIDTH + B_WIDTH:],
                    g.reshape(1, d), tm=cfg["tm_out"], final_norm=final_g is not None)


def _config(T):
    return dict(tc_repack=256, tm_in=min(1024, T), tn_in=1408, tm_out=min(512, T),
                qc_cmp=256, qc_slc=512, tk_slc=512, qc_win=128, qc_swa=min(256, T), mlstm_chunk=min(256, T), mlstm_heads_per_step=4)


def kernel(x, norm_g, w_in, w_out, cmp_pe_k, cmp_w1_k, cmp_w2_k, cmp_pe_v, cmp_w1_v, cmp_w2_v, mlstm_conv_w, mlstm_conv_b, mlstm_i_bias, mlstm_f_bias, mlstm_norm_g, swa_sinks, final_norm_g):
    bsz, T, d = x.shape
    depth = w_in.shape[0]
    cfg = _config(T)
    x2d = x.reshape(bsz * T, d)
    for l in range(depth):
        p = dict(norm_g=norm_g[l], w_in=w_in, w_out=w_out[l],
                 cmp_pe_k=cmp_pe_k[l], cmp_w1_k=cmp_w1_k[l], cmp_w2_k=cmp_w2_k[l],
                 cmp_pe_v=cmp_pe_v[l], cmp_w1_v=cmp_w1_v[l], cmp_w2_v=cmp_w2_v[l],
                 mlstm_conv_w=mlstm_conv_w[l], mlstm_conv_b=mlstm_conv_b[l],
                 mlstm_i_bias=mlstm_i_bias[l], mlstm_f_bias=mlstm_f_bias[l],
                 mlstm_norm_g=mlstm_norm_g[l], swa_sinks=swa_sinks[l])
        x2d = _layer(x2d, bsz, T, p, l, final_norm_g if l == depth - 1 else None, cfg)
    return x2d.reshape(bsz, T, d)
```
